```python
import jax, jax.numpy as jnp
from jax import lax
import numpy as np

D_MODEL = 1024
BATCH = 2
SEQ = 8192
DEPTH = 1

GRID_W = 64
CTX_LEN = 256
CONV_WIDTH = D_MODEL // 2
RET_HEADS = 4
RET_HEAD_DIM = (D_MODEL - CONV_WIDTH) // RET_HEADS
RET_WIDTH = RET_HEADS * RET_HEAD_DIM
MIX_WIDTH = CONV_WIDTH + RET_WIDTH
RET_CHUNK = 128
ROPE_BASE = 10000.0
N_EXPERTS = 16
EC_CAPACITY_FACTOR = 2
D_EXPERT = 2 * D_MODEL
RMS_EPS = 1e-6
GN_EPS = 1e-6
SPLITS = (CONV_WIDTH, CONV_WIDTH, CONV_WIDTH, RET_WIDTH, RET_WIDTH, RET_WIDTH, RET_WIDTH)
IN_COLS = sum(SPLITS)
SPLIT_IDX = tuple(int(s) for s in np.cumsum(SPLITS)[:-1])

kernel_name = "hybrid_conv_retention_ec_moe_dit"


def _rms_norm(x, g):
    xf = x.astype(jnp.float32)
    y = xf * lax.rsqrt(jnp.mean(xf * xf, axis=-1, keepdims=True) + RMS_EPS)
    return (y * g.astype(jnp.float32)).astype(x.dtype)


def _modulation(cond, w_ada, b_ada):
    m = jax.nn.silu(cond) @ w_ada + b_ada
    return jnp.split(m[:, None, :], 6, axis=-1)


def _rotary(t, pos):
    half = t.shape[-1] // 2
    inv = 1.0 / (ROPE_BASE ** (jnp.arange(half, dtype=jnp.float32) / half))
    ang = pos[:, None] * inv[None, :]
    cos = jnp.cos(ang)[None, :, None, :]
    sin = jnp.sin(ang)[None, :, None, :]
    t1, t2 = t[..., :half], t[..., half:]
    return jnp.concatenate([t1 * cos - t2 * sin, t1 * sin + t2 * cos], axis=-1)


def _ret_qkv(q, k, v, pos):
    b, n, _ = q.shape
    shp = (b, n, RET_HEADS, RET_HEAD_DIM)
    q = _rotary(q.reshape(shp).astype(jnp.float32), pos)
    k = _rotary(k.reshape(shp).astype(jnp.float32), pos) * (RET_HEAD_DIM ** -0.5)
    v = v.reshape(shp).astype(jnp.float32)
    return q, k, v


def _retention_scan(q, k, v, log_g, s0, include_self):
    b, n, h, d = q.shape
    nc = n // RET_CHUNK
    qc = q.reshape(b, nc, RET_CHUNK, h, d)
    kc = k.reshape(b, nc, RET_CHUNK, h, d)
    vc = v.reshape(b, nc, RET_CHUNK, h, d)
    idx = jnp.arange(RET_CHUNK, dtype=jnp.float32)
    rel = idx[:, None] - idx[None, :]
    mask = (rel >= 0) if include_self else (rel > 0)
    decay = jnp.where(mask[None], jnp.exp(log_g[:, None, None] * jnp.where(mask, rel, 0.0)[None]), 0.0)
    scores = jnp.einsum('bcihd,bcjhd->bchij', qc, kc) * decay[None, None]
    intra = jnp.einsum('bchij,bcjhd->bcihd', scores, vc)
    zeta = jnp.exp(log_g[None, :] * (RET_CHUNK - 1.0 - idx)[:, None])
    u = jnp.einsum('bcjhd,bcjhe->bchde', kc * zeta[None, None, :, :, None], vc)
    g_chunk = jnp.exp(log_g * RET_CHUNK)[None, :, None, None]

    def step(s, u_c):
        return g_chunk * s + u_c, s

    s_final, s_prev = lax.scan(step, s0, jnp.moveaxis(u, 1, 0))
    s_prev = jnp.moveaxis(s_prev, 0, 1)
    xi = jnp.exp(log_g[None, :] * (idx + 1.0)[:, None])
    inter = jnp.einsum('bcihd,bchde->bcihe', qc, s_prev) * xi[None, None, :, :, None]
    return (intra + inter).reshape(b, n, h, d), s_final


def _bidir_retention(q, k, v, log_gf, log_gb, s0_f, s0_b):
    y_f, s_f = _retention_scan(q, k, v, log_gf, s0_f, True)
    flip = lambda t: jnp.flip(t, axis=1)
    y_b, s_b = _retention_scan(flip(q), flip(k), flip(v), log_gb, s0_b, False)
    return y_f + flip(y_b), s_f, s_b


def _context_states(k, v, log_gf, log_gb):
    n = k.shape[1]
    pos = jnp.arange(n, dtype=jnp.float32)
    wf = jnp.exp(log_gf[None, :] * (n - 1.0 - pos)[:, None])
    wb = jnp.exp(log_gb[None, :] * pos[:, None])
    s_f = jnp.einsum('bnhd,nh,bnhe->bhde', k, wf, v)
    s_b = jnp.einsum('bnhd,nh,bnhe->bhde', k, wb, v)
    return s_f, s_b


def _ret_out(y, g, norm_g):
    b, n, _, _ = y.shape
    mu = jnp.mean(y, axis=-1, keepdims=True)
    var = jnp.mean(jnp.square(y - mu), axis=-1, keepdims=True)
    yn = ((y - mu) * lax.rsqrt(var + GN_EPS)).reshape(b, n, RET_WIDTH) * norm_g.astype(jnp.float32)
    return (jax.nn.silu(g.astype(jnp.float32)) * yn).astype(g.dtype)


def _conv_branch(gate_b, gate_c, u_in, w, bias, rows, width):
    bsz, n, ch = u_in.shape
    u = (gate_c * u_in).reshape(bsz, rows, width, ch)
    up = jnp.pad(u, ((0, 0), (0, 0), (1, 1), (0, 0)))
    y = up[:, :, :-2] * w[0] + up[:, :, 1:-1] * w[1] + up[:, :, 2:] * w[2] + bias
    return gate_b * y.reshape(bsz, n, ch)


def _expert_choice_ffn(h, w_router, b_router, w_gate, w_up, w_down):
    b, n, d = h.shape
    cap = max(1, EC_CAPACITY_FACTOR * n // N_EXPERTS)
    logits = h.astype(jnp.float32) @ w_router.astype(jnp.float32) + b_router.astype(jnp.float32)
    aff = jax.nn.softmax(logits, axis=-1)
    gate, idx = lax.top_k(jnp.swapaxes(aff, 1, 2), cap)
    xs = jax.vmap(lambda hb, ib: hb[ib])(h, idx)
    a = jnp.einsum('becd,edf->becf', xs, w_gate)
    u = jnp.einsum('becd,edf->becf', xs, w_up)
    y = jnp.einsum('becf,efd->becd', jax.nn.silu(a) * u, w_down) * gate[..., None].astype(h.dtype)
    return jax.vmap(lambda yb, ib: jnp.zeros((n, d), yb.dtype).at[ib.reshape(-1)].add(yb.reshape(-1, d)))(y, idx)


def setup_inputs(seed: int = 0) -> dict:
    key = jax.random.key(seed)
    ks = jax.random.split(key, 22)
    f32 = jnp.float32

    def nrm(k, shape, scale):
        return jax.random.normal(k, shape, f32) * scale

    decay_base = jnp.log(2.0 ** (5.0 + jnp.arange(RET_HEADS, dtype=f32)) - 1.0)
    return {
        "x": nrm(ks[0], (BATCH, SEQ, D_MODEL), 1.0),
        "c": nrm(ks[1], (BATCH, D_MODEL), 1.0),
        "ctx": nrm(ks[2], (BATCH, CTX_LEN, D_MODEL), 1.0),
        "c_ctx": nrm(ks[3], (D_MODEL,), 1.0),
        "w_ada": nrm(ks[4], (DEPTH, D_MODEL, 6 * D_MODEL), 0.5 * D_MODEL ** -0.5),
        "b_ada": nrm(ks[5], (DEPTH, 6 * D_MODEL), 0.02),
        "pre_mix_g": 1.0 + nrm(ks[6], (DEPTH, D_MODEL), 0.05),
        "post_mix_g": 1.0 + nrm(ks[7], (DEPTH, D_MODEL), 0.05),
        "pre_ffn_g": 1.0 + nrm(ks[8], (DEPTH, D_MODEL), 0.05),
        "post_ffn_g": 1.0 + nrm(ks[9], (DEPTH, D_MODEL), 0.05),
        "w_in": nrm(ks[10], (DEPTH, D_MODEL, IN_COLS), D_MODEL ** -0.5),
        "conv_w": nrm(ks[11], (DEPTH, 3, CONV_WIDTH), 3.0 ** -0.5),
        "conv_b": nrm(ks[12], (DEPTH, CONV_WIDTH), 0.02),
        "ret_decay_logit": decay_base[None, None, :] + nrm(ks[13], (DEPTH, 2, RET_HEADS), 0.1),
        "ret_norm_g": 1.0 + nrm(ks[14], (DEPTH, RET_WIDTH), 0.05),
        "w_out": nrm(ks[15], (DEPTH, MIX_WIDTH, D_MODEL), MIX_WIDTH ** -0.5),
        "w_router": nrm(ks[16], (DEPTH, D_MODEL, N_EXPERTS), D_MODEL ** -0.5),
        "b_router": nrm(ks[17], (DEPTH, N_EXPERTS), 0.01),
        "w_gate": nrm(ks[18], (DEPTH, N_EXPERTS, D_MODEL, D_EXPERT), D_MODEL ** -0.5),
        "w_up": nrm(ks[19], (DEPTH, N_EXPERTS, D_MODEL, D_EXPERT), D_MODEL ** -0.5),
        "w_down": nrm(ks[20], (DEPTH, N_EXPERTS, D_EXPERT, D_MODEL), D_EXPERT ** -0.5),
    }


def reference(x, c, ctx, c_ctx, w_ada, b_ada, pre_mix_g, post_mix_g, pre_ffn_g, post_ffn_g,
              w_in, conv_w, conv_b, ret_decay_logit, ret_norm_g, w_out,
              w_router, b_router, w_gate, w_up, w_down):
    n_lat = x.shape[1]
    ctx_len = ctx.shape[1]
    rows = n_lat // GRID_W
    pos_ctx = jnp.arange(ctx_len, dtype=jnp.float32)
    pos_lat = ctx_len + jnp.arange(n_lat, dtype=jnp.float32)
    bsz = x.shape[0]
    zero_state = jnp.zeros((bsz, RET_HEADS, RET_HEAD_DIM, RET_HEAD_DIM), jnp.float32)

    for i in range(DEPTH):
        last = i == DEPTH - 1
        sh1, sc1, g1, sh2, sc2, g2 = _modulation(c, w_ada[i], b_ada[i])
        csh1, csc1, cg1, csh2, csc2, cg2 = _modulation(c_ctx[None, :], w_ada[i], b_ada[i])
        log_gf = jax.nn.log_sigmoid(ret_decay_logit[i, 0].astype(jnp.float32))
        log_gb = jax.nn.log_sigmoid(ret_decay_logit[i, 1].astype(jnp.float32))

        hc = _rms_norm(ctx, pre_mix_g[i]) * (1.0 + csc1) + csh1
        c_b, c_c, c_x, c_q, c_k, c_v, c_g = jnp.split(hc @ w_in[i], SPLIT_IDX, axis=-1)
        qc, kc, vc = _ret_qkv(c_q, c_k, c_v, pos_ctx)
        if last:
            s_f, s_b = _context_states(kc, vc, log_gf, log_gb)
            ctx_next = ctx
        else:
            yc, s_f, s_b = _bidir_retention(qc, kc, vc, log_gf, log_gb, zero_state, zero_state)
            mix_c = jnp.concatenate([
                _conv_branch(c_b, c_c, c_x, conv_w[i], conv_b[i], 1, ctx_len),
                _ret_out(yc, c_g, ret_norm_g[i])], axis=-1) @ w_out[i]
            ctx_mid = ctx + cg1 * _rms_norm(mix_c, post_mix_g[i])
            hc2 = _rms_norm(ctx_mid, pre_ffn_g[i]) * (1.0 + csc2) + csh2
            ffn_c = _expert_choice_ffn(hc2, w_router[i], b_router[i], w_gate[i], w_up[i], w_down[i])
            ctx_next = ctx_mid + cg2 * _rms_norm(ffn_c, post_ffn_g[i])

        hx = _rms_norm(x, pre_mix_g[i]) * (1.0 + sc1) + sh1
        x_b, x_c, x_x, x_q, x_k, x_v, x_g = jnp.split(hx @ w_in[i], SPLIT_IDX, axis=-1)
        qx, kx, vx = _ret_qkv(x_q, x_k, x_v, pos_lat)
        yx, _, _ = _bidir_retention(qx, kx, vx, log_gf, log_gb, s_f, s_b)
        mix_x = jnp.concatenate([
            _conv_branch(x_b, x_c, x_x, conv_w[i], conv_b[i], rows, GRID_W),
            _ret_out(yx, x_g, ret_norm_g[i])], axis=-1) @ w_out[i]
        x = x + g1 * _rms_norm(mix_x, post_mix_g[i])
        hx2 = _rms_norm(x, pre_ffn_g[i]) * (1.0 + sc2) + sh2
        ffn_x = _expert_choice_ffn(hx2, w_router[i], b_router[i], w_gate[i], w_up[i], w_down[i])
        x = x + g2 * _rms_norm(ffn_x, post_ffn_g[i])
        ctx = ctx_next
    return x
```

```python
import functools

import jax
import jax.numpy as jnp
from jax import lax
from jax.experimental import pallas as pl
from jax.experimental.pallas import tpu as pltpu

F32 = jnp.float32
BF16 = jnp.bfloat16

GRID_W = 64
RET_HEADS = 4
RET_CHUNK = 128
ROPE_BASE = 10000.0
EC_CAPACITY_FACTOR = 2
RMS_EPS = 1e-6
GN_EPS = 1e-6

LANES = 128
SUBLANES = 8
ROW_TILE = SUBLANES
VMEM_CAP = 60000 * 1024

TOKEN_BLOCK = 256
FFN_TILE = 512
GATHER_UNROLL = 8


def _vmem(nbytes):
    return min(int(nbytes), VMEM_CAP)


def _rms(x, g):
    return x * lax.rsqrt(jnp.mean(x * x, axis=-1, keepdims=True) + RMS_EPS) * g


def _silu(x):
    return x * (1.0 / (1.0 + jnp.exp(-x)))


def _rot_half(t):
    return pltpu.roll(t, LANES // 2, axis=1)


def _modulation_kernel(cond_ref, w_ref, b_ref, o_ref):
    s = _silu(cond_ref[...])
    o_ref[...] = jnp.dot(s, w_ref[...], preferred_element_type=F32,
                         precision=lax.Precision.HIGHEST) + b_ref[...]


def _modulation(cond, w_ada, b_ada):
    rows, d = cond.shape
    cols = w_ada.shape[1]
    tile = cols // 4
    return pl.pallas_call(
        _modulation_kernel,
        grid=(cols // tile,),
        in_specs=[
            pl.BlockSpec((rows, d), lambda j: (0, 0)),
            pl.BlockSpec((d, tile), lambda j: (0, j)),
            pl.BlockSpec((1, tile), lambda j: (0, j)),
        ],
        out_specs=pl.BlockSpec((rows, tile), lambda j: (0, j)),
        out_shape=jax.ShapeDtypeStruct((rows, cols), F32),
        compiler_params=pltpu.CompilerParams(
            dimension_semantics=("arbitrary",),
            vmem_limit_bytes=_vmem(3 * d * tile * 4)),
        name="modulation",
    )(cond, w_ada, b_ada.reshape(1, cols))


def _context_kernel(lg_ref, ctx_ref, gpre_ref, sc_ref, sh_ref, wk_ref, wv_ref,
                    cos_ref, sin_ref, sf_ref, sb_ref):
    n = ctx_ref.shape[1]
    hd = LANES
    hc = _rms(ctx_ref[0], gpre_ref[...]) * (1.0 + sc_ref[...]) + sh_ref[...]
    hc = hc.astype(BF16)
    k = jnp.dot(hc, wk_ref[...], preferred_element_type=F32)
    v = jnp.dot(hc, wv_ref[...], preferred_element_type=F32)
    cos = cos_ref[...]
    sin = sin_ref[...]
    pos = lax.broadcasted_iota(jnp.int32, (n, 1), 0).astype(F32)
    for h in range(RET_HEADS):
        kh = k[:, h * hd:(h + 1) * hd]
        kr = (kh * cos + _rot_half(kh) * sin) * (hd ** -0.5)
        vh = v[:, h * hd:(h + 1) * hd].astype(BF16)
        wf = jnp.exp(lg_ref[0, h] * (n - 1.0 - pos))
        wb = jnp.exp(lg_ref[1, h] * pos)
        kf = (kr * wf).T.astype(BF16)
        kb = (kr * wb).T.astype(BF16)
        sf_ref[0, h] = jnp.dot(kf, vh, preferred_element_type=F32)
        sb_ref[0, h] = jnp.dot(kb, vh, preferred_element_type=F32)


def _context_states(lg, ctx, gpre, csc1, csh1, wk, wv, cos, sin):
    b, n, d = ctx.shape
    hd = LANES
    st = jax.ShapeDtypeStruct((b, RET_HEADS, hd, hd), F32)
    vec = pl.BlockSpec((1, d), lambda i: (0, 0))
    return pl.pallas_call(
        _context_kernel,
        grid=(b,),
        in_specs=[
            pl.BlockSpec(memory_space=pltpu.SMEM),
            pl.BlockSpec((1, n, d), lambda i: (i, 0, 0)),
            vec, vec, vec,
            pl.BlockSpec(wk.shape, lambda i: (0, 0)),
            pl.BlockSpec(wv.shape, lambda i: (0, 0)),
            pl.BlockSpec((n, hd), lambda i: (0, 0)),
            pl.BlockSpec((n, hd), lambda i: (0, 0)),
        ],
        out_specs=[pl.BlockSpec((1, RET_HEADS, hd, hd), lambda i: (i, 0, 0, 0))] * 2,
        out_shape=[st, st],
        compiler_params=pltpu.CompilerParams(dimension_semantics=("arbitrary",)),
        name="context",
    )(lg, ctx, gpre, csc1, csh1, wk, wv, cos, sin)


def _chunk_tables(lg, forward):
    c = RET_CHUNK
    i = lax.broadcasted_iota(jnp.int32, (c, c), 0)
    j = lax.broadcasted_iota(jnp.int32, (c, c), 1)
    col = lax.broadcasted_iota(jnp.int32, (c, 1), 0).astype(F32)
    if forward:
        rel = (i - j).astype(F32)
        mask = i >= j
        xi = jnp.exp(lg * (col + 1.0))
        zeta = jnp.exp(lg * (c - 1.0 - col))
    else:
        rel = (j - i).astype(F32)
        mask = j > i
        xi = jnp.exp(lg * (c - col))
        zeta = jnp.exp(lg * col)
    decay = jnp.where(mask, jnp.exp(lg * jnp.where(mask, rel, 0.0)), 0.0)
    return decay, xi, zeta, jnp.exp(lg * c)


def _retention_chunk(qh, kh_f32, vh, s_ref, h, tables):
    decay, xi, zeta, g_chunk = tables
    kh = kh_f32.astype(BF16)
    scores = lax.dot_general(qh, kh, (((1,), (1,)), ((), ())), preferred_element_type=F32)
    intra = jnp.dot((scores * decay).astype(BF16), vh, preferred_element_type=F32)
    s_prev = s_ref[h]
    inter = jnp.dot(qh, s_prev.astype(BF16), preferred_element_type=F32) * xi
    kz = (kh_f32 * zeta).T.astype(BF16)
    s_ref[h] = g_chunk * s_prev + jnp.dot(kz, vh, preferred_element_type=F32)
    return intra + inter


def _mix_bwd_kernel(lg_ref, x_ref, gpre_ref, sc_ref, sh_ref, win_ref, cw_ref, cb_ref,
                    cos_ref, sin_ref, s0_ref,
                    q_ref, k_ref, v_ref, g_ref, yb_ref, conv_ref, s_ref):
    t = x_ref.shape[1]
    hd = LANES
    cw = conv_ref.shape[2]
    rw = RET_HEADS * hd

    @pl.when(pl.program_id(1) == 0)
    def _():
        s_ref[...] = s0_ref[0]

    hx = _rms(x_ref[0], gpre_ref[...]) * (1.0 + sc_ref[0]) + sh_ref[0]
    proj = jnp.dot(hx.astype(BF16), win_ref[...], preferred_element_type=F32)
    x_b = proj[:, 0:cw]
    x_c = proj[:, cw:2 * cw]
    x_x = proj[:, 2 * cw:3 * cw]
    off = 3 * cw
    q = proj[:, off:off + rw]
    k = proj[:, off + rw:off + 2 * rw]
    v = proj[:, off + 2 * rw:off + 3 * rw]
    g_ref[0] = proj[:, off + 3 * rw:off + 4 * rw]

    u = x_c * x_x
    col = lax.broadcasted_iota(jnp.int32, (t, cw), 0) & (GRID_W - 1)
    u_prev = jnp.where(col == 0, 0.0, pltpu.roll(u, 1, axis=0))
    u_next = jnp.where(col == GRID_W - 1, 0.0, pltpu.roll(u, t - 1, axis=0))
    y = u_prev * cw_ref[0:1, :] + u * cw_ref[1:2, :] + u_next * cw_ref[2:3, :] + cb_ref[...]
    conv_ref[0] = (x_b * y).astype(BF16)

    cos = cos_ref[...]
    sin = sin_ref[...]
    v_ref[0] = v.astype(BF16)
    for h in range(RET_HEADS):
        sl = slice(h * hd, (h + 1) * hd)
        qh = q[:, sl]
        kh = k[:, sl]
        qr = qh * cos + _rot_half(qh) * sin
        kr = (kh * cos + _rot_half(kh) * sin) * (hd ** -0.5)
        q_ref[0, :, sl] = qr.astype(BF16)
        k_ref[0, :, sl] = kr.astype(BF16)
        tables = _chunk_tables(lg_ref[1, h], forward=False)
        for ci in reversed(range(t // RET_CHUNK)):
            rows = slice(ci * RET_CHUNK, (ci + 1) * RET_CHUNK)
            yb_ref[0, rows, sl] = _retention_chunk(
                qr[rows].astype(BF16), kr[rows], v[rows, sl].astype(BF16), s_ref, h, tables)


def _mix_bwd(lg, x, gpre, sc1, sh1, w_in, conv_w, conv_b, cos, sin, s0_b):
    b, n, d = x.shape
    t = TOKEN_BLOCK
    nb = n // t
    hd = LANES
    cw = conv_w.shape[1]
    rw = RET_HEADS * hd
    rev = lambda i, j: (i, nb - 1 - j, 0)
    vec = pl.BlockSpec((1, d), lambda i, j: (0, 0))
    bvec = pl.BlockSpec((1, 1, d), lambda i, j: (i, 0, 0))
    tok = lambda w: pl.BlockSpec((1, t, w), rev)
    out = lambda w, dt: jax.ShapeDtypeStruct((b, n, w), dt)
    return pl.pallas_call(
        _mix_bwd_kernel,
        grid=(b, nb),
        in_specs=[
            pl.BlockSpec(memory_space=pltpu.SMEM),
            tok(d), vec, bvec, bvec,
            pl.BlockSpec(w_in.shape, lambda i, j: (0, 0)),
            pl.BlockSpec(conv_w.shape, lambda i, j: (0, 0)),
            pl.BlockSpec((1, cw), lambda i, j: (0, 0)),
            pl.BlockSpec((t, hd), lambda i, j: (nb - 1 - j, 0)),
            pl.BlockSpec((t, hd), lambda i, j: (nb - 1 - j, 0)),
            pl.BlockSpec((1, RET_HEADS, hd, hd), lambda i, j: (i, 0, 0, 0)),
        ],
        out_specs=[tok(rw), tok(rw), tok(rw), tok(rw), tok(rw), tok(cw)],
        out_shape=[out(rw, BF16), out(rw, BF16), out(rw, BF16), out(rw, F32),
                   out(rw, F32), out(cw, BF16)],
        scratch_shapes=[pltpu.VMEM((RET_HEADS, hd, hd), F32)],
        compiler_params=pltpu.CompilerParams(
            dimension_semantics=("arbitrary", "arbitrary"),
            vmem_limit_bytes=_vmem(48 * 2 ** 20)),
        name="mix_bwd",
    )(lg, x, gpre, sc1, sh1, w_in, conv_w, conv_b, cos, sin, s0_b)


def _mix_fwd_kernel(lg_ref, x_ref, q_ref, k_ref, v_ref, g_ref, yb_ref, conv_ref,
                    wout_ref, gn_ref, gpost_ref, gffn_ref, g1_ref, sc_ref, sh_ref,
                    wr_ref, br_ref, s0_ref,
                    xmid_ref, h2_ref, logit_ref, s_ref, mix_ref):
    t = x_ref.shape[1]
    hd = LANES
    cw = conv_ref.shape[2]

    @pl.when(pl.program_id(1) == 0)
    def _():
        s_ref[...] = s0_ref[0]

    mix_ref[:, 0:cw] = conv_ref[0]
    for h in range(RET_HEADS):
        sl = slice(h * hd, (h + 1) * hd)
        tables = _chunk_tables(lg_ref[0, h], forward=True)
        for ci in range(t // RET_CHUNK):
            rows = slice(ci * RET_CHUNK, (ci + 1) * RET_CHUNK)
            y = _retention_chunk(q_ref[0, rows, sl], k_ref[0, rows, sl].astype(F32),
                                 v_ref[0, rows, sl], s_ref, h, tables)
            y = y + yb_ref[0, rows, sl]
            mu = jnp.mean(y, axis=-1, keepdims=True)
            yc = y - mu
            var = jnp.mean(yc * yc, axis=-1, keepdims=True)
            yn = yc * lax.rsqrt(var + GN_EPS) * gn_ref[:, sl]
            mix_ref[rows, cw + h * hd:cw + (h + 1) * hd] = (
                _silu(g_ref[0, rows, sl]) * yn).astype(BF16)

    mix = jnp.dot(mix_ref[...], wout_ref[...], preferred_element_type=F32)
    x_mid = x_ref[0] + g1_ref[0] * _rms(mix, gpost_ref[...])
    xmid_ref[0] = x_mid
    h2 = _rms(x_mid, gffn_ref[...]) * (1.0 + sc_ref[0]) + sh_ref[0]
    logit_ref[0] = jnp.dot(h2, wr_ref[...], preferred_element_type=F32,
                           precision=lax.Precision.HIGHEST) + br_ref[...]
    for c in range(ROW_TILE):
        h2_ref[0, pl.ds(c, t, stride=ROW_TILE), :] = h2[:, c * LANES:(c + 1) * LANES]


def _mix_fwd(lg, x, q, k, v, g, yb, conv, w_out, gn, gpost, gffn, g1, sc2, sh2, wr, br, s0_f):
    b, n, d = x.shape
    t = TOKEN_BLOCK
    nb = n // t
    hd = LANES
    cw = conv.shape[2]
    rw = RET_HEADS * hd
    fwd = lambda i, j: (i, j, 0)
    vec = lambda w: pl.BlockSpec((1, w), lambda i, j: (0, 0))
    bvec = pl.BlockSpec((1, 1, d), lambda i, j: (i, 0, 0))
    tok = lambda w: pl.BlockSpec((1, t, w), fwd)
    return pl.pallas_call(
        _mix_fwd_kernel,
        grid=(b, nb),
        in_specs=[
            pl.BlockSpec(memory_space=pltpu.SMEM),
            tok(d), tok(rw), tok(rw), tok(rw), tok(rw), tok(rw), tok(cw),
            pl.BlockSpec(w_out.shape, lambda i, j: (0, 0)),
            vec(rw), vec(d), vec(d), bvec, bvec, bvec,
            pl.BlockSpec(wr.shape, lambda i, j: (0, 0)),
            vec(LANES),
            pl.BlockSpec((1, RET_HEADS, hd, hd), lambda i, j: (i, 0, 0, 0)),
        ],
        out_specs=[tok(d), pl.BlockSpec((1, t * ROW_TILE, LANES), fwd), tok(LANES)],
        out_shape=[jax.ShapeDtypeStruct((b, n, d), F32),
                   jax.ShapeDtypeStruct((b, n * ROW_TILE, LANES), F32),
                   jax.ShapeDtypeStruct((b, n, LANES), F32)],
        scratch_shapes=[pltpu.VMEM((RET_HEADS, hd, hd), F32),
                        pltpu.VMEM((t, cw + rw), BF16)],
        compiler_params=pltpu.CompilerParams(
            dimension_semantics=("arbitrary", "arbitrary"),
            vmem_limit_bytes=_vmem(32 * 2 ** 20)),
        name="mix_fwd",
    )(lg, x, q, k, v, g, yb, conv, w_out, gn, gpost, gffn, g1, sc2, sh2, wr, br, s0_f)


def _swap_halves(a, dv):
    nv = a.shape[0]
    a5 = a.reshape(nv // (2 * dv), 2, dv, a.shape[1], a.shape[2])
    p = jnp.concatenate([a5[:, 1:2], a5[:, 0:1]], axis=1)
    return p.reshape(a.shape)


def _route_kernel(logit_ref, gate_ref, idx_ref, *, cap):
    nv, ne, _ = logit_ref.shape[1:]
    grp = SUBLANES
    x = logit_ref[0]
    m = jnp.max(x, axis=1, keepdims=True)
    s = jnp.sum(jnp.exp(x - m), axis=1, keepdims=True)
    e0 = pl.multiple_of(pl.program_id(1) * grp, grp)
    xg = logit_ref[0, :, pl.ds(e0, grp), :]
    key = jnp.exp(xg - m) / s

    shape = (nv, grp, LANES)
    vi = lax.broadcasted_iota(jnp.int32, shape, 0)
    li = lax.broadcasted_iota(jnp.int32, shape, 2)
    tok = vi * LANES + li
    idx = tok
    n = nv * LANES
    span = 2
    while span <= n:
        dist = span // 2
        while dist >= 1:
            if dist >= LANES:
                pk = _swap_halves(key, dist // LANES)
                pi = _swap_halves(idx, dist // LANES)
            else:
                low = (li & dist) == 0
                pk = jnp.where(low, pltpu.roll(key, LANES - dist, axis=2),
                               pltpu.roll(key, dist, axis=2))
                pi = jnp.where(low, pltpu.roll(idx, LANES - dist, axis=2),
                               pltpu.roll(idx, dist, axis=2))
            better = (pk > key) | ((pk == key) & (pi < idx))
            is_low = (tok & dist) == 0
            if span < n:
                want_best = is_low == ((tok & span) == 0)
            else:
                want_best = is_low
            take = better == want_best
            key = jnp.where(take, pk, key)
            idx = jnp.where(take, pi, idx)
            dist //= 2
        span *= 2
    top = cap // LANES
    gate_ref[0, 0] = key[0:top]
    idx_ref[0, 0] = idx[0:top]


def _route(logits_t, cap):
    b, nv, ne, _ = logits_t.shape
    grp = SUBLANES
    top = cap // LANES
    out_block = pl.BlockSpec((1, 1, top, grp, LANES), lambda i, j: (i, j, 0, 0, 0))
    return pl.pallas_call(
        functools.partial(_route_kernel, cap=cap),
        grid=(b, ne // grp),
        in_specs=[pl.BlockSpec((1, nv, ne, LANES), lambda i, j: (i, 0, 0, 0))],
        out_specs=[out_block, out_block],
        out_shape=[jax.ShapeDtypeStruct((b, ne // grp, top, grp, LANES), F32),
                   jax.ShapeDtypeStruct((b, ne // grp, top, grp, LANES), jnp.int32)],
        compiler_params=pltpu.CompilerParams(dimension_semantics=("arbitrary", "arbitrary")),
        name="route",
    )(logits_t)


def _dispatch_kernel(idx_ref, h_ref, xs_ref):
    cap = idx_ref.shape[2]

    def body(r, carry):
        base = r * GATHER_UNROLL
        for u in range(GATHER_UNROLL):
            tok = idx_ref[0, 0, base + u]
            src = pl.multiple_of(tok * ROW_TILE, ROW_TILE)
            dst = pl.multiple_of((base + u) * ROW_TILE, ROW_TILE)
            xs_ref[0, 0, pl.ds(dst, ROW_TILE), :] = h_ref[0, pl.ds(src, ROW_TILE), :]
        return carry

    lax.fori_loop(0, cap // GATHER_UNROLL, body, 0)


def _dispatch(idx, h_rows, n_exp, cap):
    b = h_rows.shape[0]
    rows = h_rows.shape[1]
    return pl.pallas_call(
        _dispatch_kernel,
        grid=(b, n_exp),
        in_specs=[
            pl.BlockSpec((1, 1, cap), lambda i, e: (i * n_exp + e, 0, 0),
                         memory_space=pltpu.SMEM),
            pl.BlockSpec((1, rows, LANES), lambda i, e: (i, 0, 0),
                         pipeline_mode=pl.Buffered(1)),
        ],
        out_specs=pl.BlockSpec((1, 1, cap * ROW_TILE, LANES), lambda i, e: (i, e, 0, 0)),
        out_shape=jax.ShapeDtypeStruct((b, n_exp, cap * ROW_TILE, LANES), F32),
        compiler_params=pltpu.CompilerParams(
            dimension_semantics=("arbitrary", "arbitrary"),
            vmem_limit_bytes=_vmem(rows * LANES * 4 + 3 * cap * ROW_TILE * LANES * 4)),
        name="dispatch",
    )(idx, h_rows)


def _ffn_kernel(xs_ref, wg_ref, wu_ref, wd_ref, gate_ref, y_ref, xm_ref, acc_ref):
    cap = xm_ref.shape[0]
    f = pl.program_id(2)

    @pl.when(f == 0)
    def _():
        for c in range(ROW_TILE):
            xm_ref[:, c * LANES:(c + 1) * LANES] = (
                xs_ref[0, 0, pl.ds(c, cap, stride=ROW_TILE), :].astype(BF16))

    xm = xm_ref[...]
    a = jnp.dot(xm, wg_ref[0].astype(BF16), preferred_element_type=F32)
    u = jnp.dot(xm, wu_ref[0].astype(BF16), preferred_element_type=F32)
    hmid = (_silu(a) * u).astype(BF16)
    contrib = jnp.dot(hmid, wd_ref[0].astype(BF16), preferred_element_type=F32)

    @pl.when(f == 0)
    def _():
        acc_ref[...] = contrib

    @pl.when(f > 0)
    def _():
        acc_ref[...] += contrib

    @pl.when(f == pl.num_programs(2) - 1)
    def _():
        y = acc_ref[...] * gate_ref[0, 0]
        for c in range(ROW_TILE):
            y_ref[0, 0, pl.ds(c, cap, stride=ROW_TILE), :] = y[:, c * LANES:(c + 1) * LANES]


def _ffn(xs, w_gate, w_up, w_down, gate):
    b, n_exp, rows, _ = xs.shape
    cap = rows // ROW_TILE
    d, dff = w_gate.shape[1:]
    tf = FFN_TILE
    row_block = pl.BlockSpec((1, 1, rows, LANES), lambda e, i, f: (i, e, 0, 0))
    return pl.pallas_call(
        _ffn_kernel,
        grid=(n_exp, b, dff // tf),
        in_specs=[
            row_block,
            pl.BlockSpec((1, d, tf), lambda e, i, f: (e, 0, f)),
            pl.BlockSpec((1, d, tf), lambda e, i, f: (e, 0, f)),
            pl.BlockSpec((1, tf, d), lambda e, i, f: (e, f, 0)),
            pl.BlockSpec((1, 1, cap, 1), lambda e, i, f: (i, e, 0, 0)),
        ],
        out_specs=row_block,
        out_shape=jax.ShapeDtypeStruct(xs.shape, F32),
        scratch_shapes=[pltpu.VMEM((cap, d), BF16), pltpu.VMEM((cap, d), F32)],
        compiler_params=pltpu.CompilerParams(
            dimension_semantics=("arbitrary", "arbitrary", "arbitrary"),
            vmem_limit_bytes=_vmem(52 * 2 ** 20)),
        name="ffn",
    )(xs, w_gate, w_up, w_down, gate)


def _combine_kernel(idx_ref, y_ref, acc_ref):
    cap = idx_ref.shape[2]

    @pl.when(pl.program_id(1) == 0)
    def _():
        zrows = GATHER_UNROLL * ROW_TILE

        def zero(r, carry):
            acc_ref[0, pl.ds(pl.multiple_of(r * zrows, zrows), zrows), :] = jnp.zeros(
                (zrows, LANES), F32)
            return carry

        lax.fori_loop(0, acc_ref.shape[1] // zrows, zero, 0)

    def body(r, carry):
        base = r * GATHER_UNROLL
        offs, vals = [], []
        for u in range(GATHER_UNROLL):
            tok = idx_ref[0, 0, base + u]
            dst = pl.multiple_of(tok * ROW_TILE, ROW_TILE)
            src = pl.multiple_of((base + u) * ROW_TILE, ROW_TILE)
            offs.append(dst)
            vals.append(acc_ref[0, pl.ds(dst, ROW_TILE), :] + y_ref[0, 0, pl.ds(src, ROW_TILE), :])
        for dst, val in zip(offs, vals):
            acc_ref[0, pl.ds(dst, ROW_TILE), :] = val
        return carry

    lax.fori_loop(0, cap // GATHER_UNROLL, body, 0)


def _combine(idx, y, n_tok):
    b, n_exp, rows, _ = y.shape
    cap = rows // ROW_TILE
    acc_rows = n_tok * ROW_TILE
    return pl.pallas_call(
        _combine_kernel,
        grid=(b, n_exp),
        in_specs=[
            pl.BlockSpec((1, 1, cap), lambda i, e: (i * n_exp + e, 0, 0),
                         memory_space=pltpu.SMEM),
            pl.BlockSpec((1, 1, rows, LANES), lambda i, e: (i, e, 0, 0)),
        ],
        out_specs=pl.BlockSpec((1, acc_rows, LANES), lambda i, e: (i, 0, 0),
                               pipeline_mode=pl.Buffered(1)),
        out_shape=jax.ShapeDtypeStruct((b, acc_rows, LANES), F32),
        compiler_params=pltpu.CompilerParams(
            dimension_semantics=("arbitrary", "arbitrary"),
            vmem_limit_bytes=_vmem(acc_rows * LANES * 4 + 3 * rows * LANES * 4)),
        name="combine",
    )(idx, y)


def _final_kernel(acc_ref, xmid_ref, gpost_ref, g2_ref, o_ref):
    t = xmid_ref.shape[1]
    ffn = jnp.concatenate(
        [acc_ref[0, pl.ds(c, t, stride=ROW_TILE), :] for c in range(ROW_TILE)], axis=-1)
    o_ref[0] = xmid_ref[0] + g2_ref[0] * _rms(ffn, gpost_ref[...])


def _final(acc, x_mid, gpost, g2):
    b, n, d = x_mid.shape
    t = TOKEN_BLOCK
    return pl.pallas_call(
        _final_kernel,
        grid=(b, n // t),
        in_specs=[
            pl.BlockSpec((1, t * ROW_TILE, LANES), lambda i, j: (i, j, 0)),
            pl.BlockSpec((1, t, d), lambda i, j: (i, j, 0)),
            pl.BlockSpec((1, d), lambda i, j: (0, 0)),
            pl.BlockSpec((1, 1, d), lambda i, j: (i, 0, 0)),
        ],
        out_specs=pl.BlockSpec((1, t, d), lambda i, j: (i, j, 0)),
        out_shape=jax.ShapeDtypeStruct((b, n, d), F32),
        compiler_params=pltpu.CompilerParams(dimension_semantics=("arbitrary", "arbitrary")),
        name="final",
    )(acc, x_mid, gpost, g2)


def _rope_tables(pos):
    half = LANES // 2
    inv = 1.0 / (ROPE_BASE ** (jnp.arange(half, dtype=F32) / half))
    ang = pos[:, None] * inv[None, :]
    cos = jnp.cos(ang)
    sin = jnp.sin(ang)
    return jnp.concatenate([cos, cos], axis=-1), jnp.concatenate([-sin, sin], axis=-1)


def kernel(x, c, ctx, c_ctx, w_ada, b_ada, pre_mix_g, post_mix_g, pre_ffn_g, post_ffn_g,
           w_in, conv_w, conv_b, ret_decay_logit, ret_norm_g, w_out,
           w_router, b_router, w_gate, w_up, w_down):
    bsz, n, d = x.shape
    ctx_len = ctx.shape[1]
    depth = w_ada.shape[0]
    n_exp = w_router.shape[2]
    cw = conv_w.shape[2]
    rw = RET_HEADS * LANES
    cap = max(1, EC_CAPACITY_FACTOR * n // n_exp)
    assert depth == 1, "only the single-layer block is implemented"
    assert d == ROW_TILE * LANES and n % TOKEN_BLOCK == 0 and TOKEN_BLOCK % RET_CHUNK == 0
    assert n % GRID_W == 0 and TOKEN_BLOCK % GRID_W == 0 and ctx_len % SUBLANES == 0
    assert n_exp % SUBLANES == 0 and cap % LANES == 0 and (n // LANES) & (n // LANES - 1) == 0
    assert w_in.shape[2] == 3 * cw + 4 * rw

    cos_c, sin_c = _rope_tables(jnp.arange(ctx_len, dtype=F32))
    cos_x, sin_x = _rope_tables(ctx_len + jnp.arange(n, dtype=F32))
    i = 0

    pad = -(bsz + 1) % SUBLANES
    cond = jnp.concatenate([c, c_ctx[None, :], jnp.zeros((pad, d), F32)], axis=0)
    mod = _modulation(cond, w_ada[i], b_ada[i])
    sh1, sc1, g1, sh2, sc2, g2 = [mod[:bsz, j * d:(j + 1) * d].reshape(bsz, 1, d) for j in range(6)]
    csh1 = mod[bsz:bsz + 1, 0:d]
    csc1 = mod[bsz:bsz + 1, d:2 * d]

    lg = jax.nn.log_sigmoid(ret_decay_logit[i].astype(F32))
    w_in_b = w_in[i].astype(BF16)
    w_out_b = w_out[i].astype(BF16)
    k_off = 3 * cw + rw
    row = lambda a: a.reshape(1, -1)

    s_f, s_b = _context_states(lg, ctx, row(pre_mix_g[i]), csc1, csh1,
                               w_in_b[:, k_off:k_off + rw], w_in_b[:, k_off + rw:k_off + 2 * rw],
                               cos_c, sin_c)

    q, k, v, g, yb, conv = _mix_bwd(lg, x, row(pre_mix_g[i]), sc1, sh1, w_in_b,
                                    conv_w[i], row(conv_b[i]), cos_x, sin_x, s_b)

    wr = jnp.pad(w_router[i].astype(F32), ((0, 0), (0, LANES - n_exp)))
    br = jnp.pad(b_router[i].astype(F32), (0, LANES - n_exp)).reshape(1, LANES)
    x_mid, h_rows, logits = _mix_fwd(lg, x, q, k, v, g, yb, conv, w_out_b, row(ret_norm_g[i]),
                                     row(post_mix_g[i]), row(pre_ffn_g[i]), g1, sc2, sh2,
                                     wr, br, s_f)

    logits_t = logits[:, :, :n_exp].reshape(bsz, n // LANES, LANES, n_exp).transpose(0, 1, 3, 2)
    gate5, idx5 = _route(logits_t, cap)
    gate = gate5.transpose(0, 1, 3, 2, 4).reshape(bsz, n_exp, cap)
    idx = idx5.transpose(0, 1, 3, 2, 4).reshape(bsz * n_exp, 1, cap)

    xs = _dispatch(idx, h_rows, n_exp, cap)
    y = _ffn(xs, w_gate[i], w_up[i], w_down[i], gate.reshape(bsz, n_exp, cap, 1))
    acc = _combine(idx, y, n)
    return _final(acc, x_mid, row(post_ffn_g[i]), g2)
```

```python
import functools

import jax
import jax.numpy as jnp
from jax import lax
from jax.experimental import pallas as pl
from jax.experimental.pallas import tpu as pltpu

F32 = jnp.float32
BF16 = jnp.bfloat16

GRID_W = 64
RET_HEADS = 4
RET_CHUNK = 128
ROPE_BASE = 10000.0
EC_CAPACITY_FACTOR = 2
RMS_EPS = 1e-6
GN_EPS = 1e-6

LANES = 128
SUBLANES = 8
ROW_TILE = SUBLANES
GROUP = SUBLANES * ROW_TILE
VMEM_CAP = 60000 * 1024

TOKEN_BLOCK = 512
FFN_TILE = 512
GATHER_UNROLL = SUBLANES


def _vmem(nbytes):
    return min(int(nbytes), VMEM_CAP)


def _rms(x, g):
    return x * lax.rsqrt(jnp.mean(x * x, axis=-1, keepdims=True) + RMS_EPS) * g


def _silu(x):
    return x * (1.0 / (1.0 + jnp.exp(-x)))


def _rot_half(t):
    return pltpu.roll(t, LANES // 2, axis=1)


def _store_tile_major(ref, lead, val):
    rows = val.shape[0]
    for c in range(ROW_TILE):
        ref[lead + (slice(None), slice(c * SUBLANES, (c + 1) * SUBLANES), slice(None))] = (
            val[:, c * LANES:(c + 1) * LANES].reshape(rows // SUBLANES, SUBLANES, LANES))


def _load_tile_major(ref, lead):
    rows = ref.shape[len(lead)] * SUBLANES
    return jnp.concatenate(
        [ref[lead + (slice(None), slice(c * SUBLANES, (c + 1) * SUBLANES), slice(None))]
         .reshape(rows, LANES) for c in range(ROW_TILE)], axis=-1)


def _row_base(r):
    return (r >> 3) * GROUP + (r & (SUBLANES - 1))


def _modulation_kernel(cond_ref, w_ref, b_ref, o_ref):
    s = _silu(cond_ref[...])
    o_ref[...] = jnp.dot(s, w_ref[...], preferred_element_type=F32,
                         precision=lax.Precision.HIGHEST) + b_ref[...]


def _modulation(cond, w_ada, b_ada):
    rows, d = cond.shape
    cols = w_ada.shape[1]
    tile = cols // 4
    return pl.pallas_call(
        _modulation_kernel,
        grid=(cols // tile,),
        in_specs=[
            pl.BlockSpec((rows, d), lambda j: (0, 0)),
            pl.BlockSpec((d, tile), lambda j: (0, j)),
            pl.BlockSpec((1, tile), lambda j: (0, j)),
        ],
        out_specs=pl.BlockSpec((rows, tile), lambda j: (0, j)),
        out_shape=jax.ShapeDtypeStruct((rows, cols), F32),
        compiler_params=pltpu.CompilerParams(
            dimension_semantics=("arbitrary",),
            vmem_limit_bytes=_vmem(3 * d * tile * 4)),
        name="modulation",
    )(cond, w_ada, b_ada.reshape(1, cols))


def _context_kernel(lg_ref, ctx_ref, gpre_ref, sc_ref, sh_ref, wk_ref, wv_ref,
                    cos_ref, sin_ref, sf_ref, sb_ref):
    n = ctx_ref.shape[1]
    hd = LANES
    hc = _rms(ctx_ref[0], gpre_ref[...]) * (1.0 + sc_ref[...]) + sh_ref[...]
    hc = hc.astype(BF16)
    k = jnp.dot(hc, wk_ref[...], preferred_element_type=F32)
    v = jnp.dot(hc, wv_ref[...], preferred_element_type=F32)
    cos = cos_ref[...]
    sin = sin_ref[...]
    pos = lax.broadcasted_iota(jnp.int32, (n, 1), 0).astype(F32)
    for h in range(RET_HEADS):
        kh = k[:, h * hd:(h + 1) * hd]
        kr = (kh * cos + _rot_half(kh) * sin) * (hd ** -0.5)
        vh = v[:, h * hd:(h + 1) * hd].astype(BF16)
        wf = jnp.exp(lg_ref[0, h] * (n - 1.0 - pos))
        wb = jnp.exp(lg_ref[1, h] * pos)
        kf = (kr * wf).T.astype(BF16)
        kb = (kr * wb).T.astype(BF16)
        sf_ref[0, h] = jnp.dot(kf, vh, preferred_element_type=F32)
        sb_ref[0, h] = jnp.dot(kb, vh, preferred_element_type=F32)


def _context_states(lg, ctx, gpre, csc1, csh1, wk, wv, cos, sin):
    b, n, d = ctx.shape
    hd = LANES
    st = jax.ShapeDtypeStruct((b, RET_HEADS, hd, hd), F32)
    vec = pl.BlockSpec((1, d), lambda i: (0, 0))
    return pl.pallas_call(
        _context_kernel,
        grid=(b,),
        in_specs=[
            pl.BlockSpec(memory_space=pltpu.SMEM),
            pl.BlockSpec((1, n, d), lambda i: (i, 0, 0)),
            vec, vec, vec,
            pl.BlockSpec(wk.shape, lambda i: (0, 0)),
            pl.BlockSpec(wv.shape, lambda i: (0, 0)),
            pl.BlockSpec((n, hd), lambda i: (0, 0)),
            pl.BlockSpec((n, hd), lambda i: (0, 0)),
        ],
        out_specs=[pl.BlockSpec((1, RET_HEADS, hd, hd), lambda i: (i, 0, 0, 0))] * 2,
        out_shape=[st, st],
        compiler_params=pltpu.CompilerParams(dimension_semantics=("arbitrary",)),
        name="context",
    )(lg, ctx, gpre, csc1, csh1, wk, wv, cos, sin)


def _chunk_tables(lg, forward):
    c = RET_CHUNK
    i = lax.broadcasted_iota(jnp.int32, (c, c), 0)
    j = lax.broadcasted_iota(jnp.int32, (c, c), 1)
    col = lax.broadcasted_iota(jnp.int32, (c, 1), 0).astype(F32)
    if forward:
        rel = (i - j).astype(F32)
        mask = i >= j
        xi = jnp.exp(lg * (col + 1.0))
        zeta = jnp.exp(lg * (c - 1.0 - col))
    else:
        rel = (j - i).astype(F32)
        mask = j > i
        xi = jnp.exp(lg * (c - col))
        zeta = jnp.exp(lg * col)
    decay = jnp.where(mask, jnp.exp(lg * jnp.where(mask, rel, 0.0)), 0.0)
    return decay, xi, zeta, jnp.exp(lg * c)


def _retention_chunk(qh, kh_f32, vh, s_ref, h, tables):
    decay, xi, zeta, g_chunk = tables
    kh = kh_f32.astype(BF16)
    scores = lax.dot_general(qh, kh, (((1,), (1,)), ((), ())), preferred_element_type=F32)
    intra = jnp.dot((scores * decay).astype(BF16), vh, preferred_element_type=F32)
    s_prev = s_ref[h]
    inter = jnp.dot(qh, s_prev.astype(BF16), preferred_element_type=F32) * xi
    kz = (kh_f32 * zeta).T.astype(BF16)
    s_ref[h] = g_chunk * s_prev + jnp.dot(kz, vh, preferred_element_type=F32)
    return intra + inter


def _mix_bwd_kernel(lg_ref, x_ref, gpre_ref, sc_ref, sh_ref, win_ref, cw_ref, cb_ref,
                    cos_ref, sin_ref, s0_ref,
                    q_ref, k_ref, v_ref, g_ref, yb_ref, conv_ref, s_ref):
    t = x_ref.shape[1]
    hd = LANES
    cw = conv_ref.shape[2]
    rw = RET_HEADS * hd

    @pl.when(pl.program_id(1) == 0)
    def _():
        s_ref[...] = s0_ref[0]

    hx = _rms(x_ref[0], gpre_ref[...]) * (1.0 + sc_ref[0]) + sh_ref[0]
    proj = jnp.dot(hx.astype(BF16), win_ref[...], preferred_element_type=F32)
    x_b = proj[:, 0:cw]
    x_c = proj[:, cw:2 * cw]
    x_x = proj[:, 2 * cw:3 * cw]
    off = 3 * cw
    q = proj[:, off:off + rw]
    k = proj[:, off + rw:off + 2 * rw]
    v = proj[:, off + 2 * rw:off + 3 * rw]
    g_ref[0] = proj[:, off + 3 * rw:off + 4 * rw]

    u = x_c * x_x
    col = lax.broadcasted_iota(jnp.int32, (t, cw), 0) & (GRID_W - 1)
    u_prev = jnp.where(col == 0, 0.0, pltpu.roll(u, 1, axis=0))
    u_next = jnp.where(col == GRID_W - 1, 0.0, pltpu.roll(u, t - 1, axis=0))
    y = u_prev * cw_ref[0:1, :] + u * cw_ref[1:2, :] + u_next * cw_ref[2:3, :] + cb_ref[...]
    conv_ref[0] = (x_b * y).astype(BF16)

    cos = cos_ref[...]
    sin = sin_ref[...]
    v_ref[0] = v.astype(BF16)
    for h in range(RET_HEADS):
        sl = slice(h * hd, (h + 1) * hd)
        qh = q[:, sl]
        kh = k[:, sl]
        qr = qh * cos + _rot_half(qh) * sin
        kr = (kh * cos + _rot_half(kh) * sin) * (hd ** -0.5)
        q_ref[0, :, sl] = qr.astype(BF16)
        k_ref[0, :, sl] = kr.astype(BF16)
        tables = _chunk_tables(lg_ref[1, h], forward=False)
        for ci in reversed(range(t // RET_CHUNK)):
            rows = slice(ci * RET_CHUNK, (ci + 1) * RET_CHUNK)
            yb_ref[0, rows, sl] = _retention_chunk(
                qr[rows].astype(BF16), kr[rows], v[rows, sl].astype(BF16), s_ref, h, tables)


def _mix_bwd(lg, x, gpre, sc1, sh1, w_in, conv_w, conv_b, cos, sin, s0_b):
    b, n, d = x.shape
    t = TOKEN_BLOCK
    nb = n // t
    hd = LANES
    cw = conv_w.shape[1]
    rw = RET_HEADS * hd
    rev = lambda i, j: (i, nb - 1 - j, 0)
    vec = pl.BlockSpec((1, d), lambda i, j: (0, 0))
    bvec = pl.BlockSpec((1, 1, d), lambda i, j: (i, 0, 0))
    tok = lambda w: pl.BlockSpec((1, t, w), rev)
    out = lambda w, dt: jax.ShapeDtypeStruct((b, n, w), dt)
    return pl.pallas_call(
        _mix_bwd_kernel,
        grid=(b, nb),
        in_specs=[
            pl.BlockSpec(memory_space=pltpu.SMEM),
            tok(d), vec, bvec, bvec,
            pl.BlockSpec(w_in.shape, lambda i, j: (0, 0)),
            pl.BlockSpec(conv_w.shape, lambda i, j: (0, 0)),
            pl.BlockSpec((1, cw), lambda i, j: (0, 0)),
            pl.BlockSpec((t, hd), lambda i, j: (nb - 1 - j, 0)),
            pl.BlockSpec((t, hd), lambda i, j: (nb - 1 - j, 0)),
            pl.BlockSpec((1, RET_HEADS, hd, hd), lambda i, j: (i, 0, 0, 0)),
        ],
        out_specs=[tok(rw), tok(rw), tok(rw), tok(rw), tok(rw), tok(cw)],
        out_shape=[out(rw, BF16), out(rw, BF16), out(rw, BF16), out(rw, F32),
                   out(rw, F32), out(cw, BF16)],
        scratch_shapes=[pltpu.VMEM((RET_HEADS, hd, hd), F32)],
        compiler_params=pltpu.CompilerParams(
            dimension_semantics=("arbitrary", "arbitrary"),
            vmem_limit_bytes=_vmem(48 * 2 ** 20)),
        name="mix_bwd",
    )(lg, x, gpre, sc1, sh1, w_in, conv_w, conv_b, cos, sin, s0_b)


def _mix_fwd_kernel(lg_ref, x_ref, q_ref, k_ref, v_ref, g_ref, yb_ref, conv_ref,
                    wout_ref, gn_ref, gpost_ref, gffn_ref, g1_ref, sc_ref, sh_ref,
                    wr_ref, br_ref, s0_ref,
                    xmid_ref, h2_ref, logit_ref, s_ref, mix_ref):
    t = x_ref.shape[1]
    hd = LANES
    cw = conv_ref.shape[2]

    @pl.when(pl.program_id(1) == 0)
    def _():
        s_ref[...] = s0_ref[0]

    mix_ref[:, 0:cw] = conv_ref[0]
    for h in range(RET_HEADS):
        sl = slice(h * hd, (h + 1) * hd)
        tables = _chunk_tables(lg_ref[0, h], forward=True)
        for ci in range(t // RET_CHUNK):
            rows = slice(ci * RET_CHUNK, (ci + 1) * RET_CHUNK)
            y = _retention_chunk(q_ref[0, rows, sl], k_ref[0, rows, sl].astype(F32),
                                 v_ref[0, rows, sl], s_ref, h, tables)
            y = y + yb_ref[0, rows, sl]
            mu = jnp.mean(y, axis=-1, keepdims=True)
            yc = y - mu
            var = jnp.mean(yc * yc, axis=-1, keepdims=True)
            yn = yc * lax.rsqrt(var + GN_EPS) * gn_ref[:, sl]
            mix_ref[rows, cw + h * hd:cw + (h + 1) * hd] = (
                _silu(g_ref[0, rows, sl]) * yn).astype(BF16)

    mix = jnp.dot(mix_ref[...], wout_ref[...], preferred_element_type=F32)
    x_mid = x_ref[0] + g1_ref[0] * _rms(mix, gpost_ref[...])
    xmid_ref[0] = x_mid
    h2 = _rms(x_mid, gffn_ref[...]) * (1.0 + sc_ref[0]) + sh_ref[0]
    ne = logit_ref.shape[2]
    h_hi = h2.astype(BF16)
    h_lo = (h2 - h_hi.astype(F32)).astype(BF16)
    p_hi = jnp.dot(h_hi, wr_ref[...], preferred_element_type=F32)
    p_lo = jnp.dot(h_lo, wr_ref[:, 0:ne], preferred_element_type=F32)
    logit_ref[0] = p_hi[:, 0:ne] + p_hi[:, ne:2 * ne] + p_lo + br_ref[...]
    _store_tile_major(h2_ref, (0,), h2)


def _mix_fwd(lg, x, q, k, v, g, yb, conv, w_out, gn, gpost, gffn, g1, sc2, sh2, wr, br, s0_f):
    b, n, d = x.shape
    t = TOKEN_BLOCK
    nb = n // t
    hd = LANES
    cw = conv.shape[2]
    rw = RET_HEADS * hd
    ne = br.shape[1]
    fwd = lambda i, j: (i, j, 0)
    vec = lambda w: pl.BlockSpec((1, w), lambda i, j: (0, 0))
    bvec = pl.BlockSpec((1, 1, d), lambda i, j: (i, 0, 0))
    tok = lambda w: pl.BlockSpec((1, t, w), fwd)
    return pl.pallas_call(
        _mix_fwd_kernel,
        grid=(b, nb),
        in_specs=[
            pl.BlockSpec(memory_space=pltpu.SMEM),
            tok(d), tok(rw), tok(rw), tok(rw), tok(rw), tok(rw), tok(cw),
            pl.BlockSpec(w_out.shape, lambda i, j: (0, 0)),
            vec(rw), vec(d), vec(d), bvec, bvec, bvec,
            pl.BlockSpec(wr.shape, lambda i, j: (0, 0)),
            vec(ne),
            pl.BlockSpec((1, RET_HEADS, hd, hd), lambda i, j: (i, 0, 0, 0)),
        ],
        out_specs=[tok(d),
                   pl.BlockSpec((1, t // SUBLANES, GROUP, LANES), lambda i, j: (i, j, 0, 0)),
                   tok(ne)],
        out_shape=[jax.ShapeDtypeStruct((b, n, d), F32),
                   jax.ShapeDtypeStruct((b, n // SUBLANES, GROUP, LANES), F32),
                   jax.ShapeDtypeStruct((b, n, ne), F32)],
        scratch_shapes=[pltpu.VMEM((RET_HEADS, hd, hd), F32),
                        pltpu.VMEM((t, cw + rw), BF16)],
        compiler_params=pltpu.CompilerParams(
            dimension_semantics=("arbitrary", "arbitrary"),
            vmem_limit_bytes=_vmem(48 * 2 ** 20)),
        name="mix_fwd",
    )(lg, x, q, k, v, g, yb, conv, w_out, gn, gpost, gffn, g1, sc2, sh2, wr, br, s0_f)


def _swap_halves(a, dv):
    nv = a.shape[0]
    a5 = a.reshape(nv // (2 * dv), 2, dv, a.shape[1], a.shape[2])
    p = jnp.concatenate([a5[:, 1:2], a5[:, 0:1]], axis=1)
    return p.reshape(a.shape)


def _route_kernel(logit_ref, gate_ref, idx_ref, *, cap):
    nv, ne, _ = logit_ref.shape[1:]
    grp = SUBLANES
    x = logit_ref[0]
    m = jnp.max(x, axis=1, keepdims=True)
    s = jnp.sum(jnp.exp(x - m), axis=1, keepdims=True)
    e0 = pl.multiple_of(pl.program_id(1) * grp, grp)
    xg = logit_ref[0, :, pl.ds(e0, grp), :]
    key = jnp.exp(xg - m) / s

    shape = (nv, grp, LANES)
    vi = lax.broadcasted_iota(jnp.int32, shape, 0)
    li = lax.broadcasted_iota(jnp.int32, shape, 2)
    tok = vi * LANES + li
    idx = tok
    n = nv * LANES
    span = 2
    while span <= n:
        dist = span // 2
        while dist >= 1:
            if dist >= LANES:
                pk = _swap_halves(key, dist // LANES)
                pi = _swap_halves(idx, dist // LANES)
            else:
                low = (li & dist) == 0
                pk = jnp.where(low, pltpu.roll(key, LANES - dist, axis=2),
                               pltpu.roll(key, dist, axis=2))
                pi = jnp.where(low, pltpu.roll(idx, LANES - dist, axis=2),
                               pltpu.roll(idx, dist, axis=2))
            better = (pk > key) | ((pk == key) & (pi < idx))
            is_low = (tok & dist) == 0
            if span < n:
                want_best = is_low == ((tok & span) == 0)
            else:
                want_best = is_low
            take = better == want_best
            key = jnp.where(take, pk, key)
            idx = jnp.where(take, pi, idx)
            dist //= 2
        span *= 2
    top = cap // LANES
    gate_ref[0, 0] = key[0:top]
    idx_ref[0, 0] = idx[0:top]


def _route(logits_t, cap):
    b, nv, ne, _ = logits_t.shape
    grp = SUBLANES
    top = cap // LANES
    out_block = pl.BlockSpec((1, 1, top, grp, LANES), lambda i, j: (i, j, 0, 0, 0))
    return pl.pallas_call(
        functools.partial(_route_kernel, cap=cap),
        grid=(b, ne // grp),
        in_specs=[pl.BlockSpec((1, nv, ne, LANES), lambda i, j: (i, 0, 0, 0))],
        out_specs=[out_block, out_block],
        out_shape=[jax.ShapeDtypeStruct((b, ne // grp, top, grp, LANES), F32),
                   jax.ShapeDtypeStruct((b, ne // grp, top, grp, LANES), jnp.int32)],
        compiler_params=pltpu.CompilerParams(dimension_semantics=("arbitrary", "arbitrary")),
        name="route",
    )(logits_t)


def _dispatch_kernel(idx_ref, h_ref, xs_ref):
    cap = idx_ref.shape[2]

    def body(r, carry):
        for u in range(GATHER_UNROLL):
            tok = idx_ref[0, 0, r * GATHER_UNROLL + u]
            xs_ref[0, 0, pl.ds(r * GROUP + u, ROW_TILE, stride=SUBLANES), :] = (
                h_ref[0, pl.ds(_row_base(tok), ROW_TILE, stride=SUBLANES), :])
        return carry

    lax.fori_loop(0, cap // GATHER_UNROLL, body, 0)


def _dispatch(idx, h_rows, n_exp, cap):
    b = h_rows.shape[0]
    rows = h_rows.shape[1]
    return pl.pallas_call(
        _dispatch_kernel,
        grid=(b, n_exp),
        in_specs=[
            pl.BlockSpec((1, 1, cap), lambda i, e: (i * n_exp + e, 0, 0),
                         memory_space=pltpu.SMEM),
            pl.BlockSpec((1, rows, LANES), lambda i, e: (i, 0, 0),
                         pipeline_mode=pl.Buffered(1)),
        ],
        out_specs=pl.BlockSpec((1, 1, cap * ROW_TILE, LANES), lambda i, e: (i, e, 0, 0)),
        out_shape=jax.ShapeDtypeStruct((b, n_exp, cap * ROW_TILE, LANES), F32),
        compiler_params=pltpu.CompilerParams(
            dimension_semantics=("arbitrary", "arbitrary"),
            vmem_limit_bytes=_vmem(rows * LANES * 4 + 3 * cap * ROW_TILE * LANES * 4)),
        name="dispatch",
    )(idx, h_rows)


def _ffn_kernel(xs_ref, wg_ref, wu_ref, wd_ref, gate_ref, y_ref, xm_ref, acc_ref):
    cap = xm_ref.shape[0]
    f = pl.program_id(2)

    @pl.when(f == 0)
    def _():
        for c in range(ROW_TILE):
            xm_ref[:, c * LANES:(c + 1) * LANES] = (
                xs_ref[0, 0, :, c * SUBLANES:(c + 1) * SUBLANES, :]
                .reshape(cap, LANES).astype(BF16))

    xm = xm_ref[...]
    a = jnp.dot(xm, wg_ref[0].astype(BF16), preferred_element_type=F32)
    u = jnp.dot(xm, wu_ref[0].astype(BF16), preferred_element_type=F32)
    hmid = (_silu(a) * u).astype(BF16)
    contrib = jnp.dot(hmid, wd_ref[0].astype(BF16), preferred_element_type=F32)

    @pl.when(f == 0)
    def _():
        acc_ref[...] = contrib

    @pl.when(f > 0)
    def _():
        acc_ref[...] += contrib

    @pl.when(f == pl.num_programs(2) - 1)
    def _():
        _store_tile_major(y_ref, (0, 0), acc_ref[...] * gate_ref[0, 0])


def _ffn(xs, w_gate, w_up, w_down, gate):
    b, n_exp, groups, _, _ = xs.shape
    cap = groups * SUBLANES
    d, dff = w_gate.shape[1:]
    tf = FFN_TILE
    row_block = pl.BlockSpec((1, 1, groups, GROUP, LANES), lambda e, i, f: (i, e, 0, 0, 0))
    return pl.pallas_call(
        _ffn_kernel,
        grid=(n_exp, b, dff // tf),
        in_specs=[
            row_block,
            pl.BlockSpec((1, d, tf), lambda e, i, f: (e, 0, f)),
            pl.BlockSpec((1, d, tf), lambda e, i, f: (e, 0, f)),
            pl.BlockSpec((1, tf, d), lambda e, i, f: (e, f, 0)),
            pl.BlockSpec((1, 1, cap, 1), lambda e, i, f: (i, e, 0, 0)),
        ],
        out_specs=row_block,
        out_shape=jax.ShapeDtypeStruct(xs.shape, F32),
        scratch_shapes=[pltpu.VMEM((cap, d), BF16), pltpu.VMEM((cap, d), F32)],
        compiler_params=pltpu.CompilerParams(
            dimension_semantics=("arbitrary", "arbitrary", "arbitrary"),
            vmem_limit_bytes=_vmem(52 * 2 ** 20)),
        name="ffn",
    )(xs, w_gate, w_up, w_down, gate)


def _combine_kernel(idx_ref, y_ref, acc_ref):
    cap = idx_ref.shape[2]

    @pl.when(pl.program_id(1) == 0)
    def _():
        zrows = GATHER_UNROLL * ROW_TILE

        def zero(r, carry):
            acc_ref[0, pl.ds(pl.multiple_of(r * zrows, zrows), zrows), :] = jnp.zeros(
                (zrows, LANES), F32)
            return carry

        lax.fori_loop(0, acc_ref.shape[1] // zrows, zero, 0)

    def body(r, carry):
        offs, vals = [], []
        for u in range(GATHER_UNROLL):
            dst = _row_base(idx_ref[0, 0, r * GATHER_UNROLL + u])
            offs.append(dst)
            vals.append(acc_ref[0, pl.ds(dst, ROW_TILE, stride=SUBLANES), :]
                        + y_ref[0, 0, pl.ds(r * GROUP + u, ROW_TILE, stride=SUBLANES), :])
        for dst, val in zip(offs, vals):
            acc_ref[0, pl.ds(dst, ROW_TILE, stride=SUBLANES), :] = val
        return carry

    lax.fori_loop(0, cap // GATHER_UNROLL, body, 0)


def _combine(idx, y, n_tok):
    b, n_exp, rows, _ = y.shape
    cap = rows // ROW_TILE
    acc_rows = n_tok * ROW_TILE
    return pl.pallas_call(
        _combine_kernel,
        grid=(b, n_exp),
        in_specs=[
            pl.BlockSpec((1, 1, cap), lambda i, e: (i * n_exp + e, 0, 0),
                         memory_space=pltpu.SMEM),
            pl.BlockSpec((1, 1, rows, LANES), lambda i, e: (i, e, 0, 0)),
        ],
        out_specs=pl.BlockSpec((1, acc_rows, LANES), lambda i, e: (i, 0, 0),
                               pipeline_mode=pl.Buffered(1)),
        out_shape=jax.ShapeDtypeStruct((b, acc_rows, LANES), F32),
        compiler_params=pltpu.CompilerParams(
            dimension_semantics=("arbitrary", "arbitrary"),
            vmem_limit_bytes=_vmem(acc_rows * LANES * 4 + 3 * rows * LANES * 4)),
        name="combine",
    )(idx, y)


def _final_kernel(acc_ref, xmid_ref, gpost_ref, g2_ref, o_ref):
    ffn = _load_tile_major(acc_ref, (0,))
    o_ref[0] = xmid_ref[0] + g2_ref[0] * _rms(ffn, gpost_ref[...])


def _final(acc, x_mid, gpost, g2):
    b, n, d = x_mid.shape
    t = TOKEN_BLOCK
    return pl.pallas_call(
        _final_kernel,
        grid=(b, n // t),
        in_specs=[
            pl.BlockSpec((1, t // SUBLANES, GROUP, LANES), lambda i, j: (i, j, 0, 0)),
            pl.BlockSpec((1, t, d), lambda i, j: (i, j, 0)),
            pl.BlockSpec((1, d), lambda i, j: (0, 0)),
            pl.BlockSpec((1, 1, d), lambda i, j: (i, 0, 0)),
        ],
        out_specs=pl.BlockSpec((1, t, d), lambda i, j: (i, j, 0)),
        out_shape=jax.ShapeDtypeStruct((b, n, d), F32),
        compiler_params=pltpu.CompilerParams(dimension_semantics=("arbitrary", "arbitrary")),
        name="final",
    )(acc, x_mid, gpost, g2)


def _rope_tables(n_pos):
    half = LANES // 2
    inv = 1.0 / (ROPE_BASE ** (jnp.arange(half, dtype=F32) / half))
    ang_a = (jnp.arange(n_pos // LANES, dtype=F32) * LANES)[:, None] * inv[None, :]
    ang_b = jnp.arange(LANES, dtype=F32)[:, None] * inv[None, :]
    ca, sa = jnp.cos(ang_a)[:, None, :], jnp.sin(ang_a)[:, None, :]
    cb, sb = jnp.cos(ang_b)[None, :, :], jnp.sin(ang_b)[None, :, :]
    cos = (ca * cb - sa * sb).reshape(n_pos, half)
    sin = (sa * cb + ca * sb).reshape(n_pos, half)
    return jnp.concatenate([cos, cos], axis=-1), jnp.concatenate([-sin, sin], axis=-1)


def kernel(x, c, ctx, c_ctx, w_ada, b_ada, pre_mix_g, post_mix_g, pre_ffn_g, post_ffn_g,
           w_in, conv_w, conv_b, ret_decay_logit, ret_norm_g, w_out,
           w_router, b_router, w_gate, w_up, w_down):
    bsz, n, d = x.shape
    ctx_len = ctx.shape[1]
    depth = w_ada.shape[0]
    n_exp = w_router.shape[2]
    cw = conv_w.shape[2]
    rw = RET_HEADS * LANES
    cap = max(1, EC_CAPACITY_FACTOR * n // n_exp)
    assert depth == 1, "only the single-layer block is implemented"
    assert d == ROW_TILE * LANES and n % TOKEN_BLOCK == 0 and TOKEN_BLOCK % RET_CHUNK == 0
    assert n % GRID_W == 0 and TOKEN_BLOCK % GRID_W == 0 and ctx_len % SUBLANES == 0
    assert n_exp % SUBLANES == 0 and cap % LANES == 0 and (n // LANES) & (n // LANES - 1) == 0
    assert w_in.shape[2] == 3 * cw + 4 * rw

    assert (ctx_len + n) % LANES == 0
    cos_all, sin_all = _rope_tables(ctx_len + n)
    cos_c, sin_c = cos_all[:ctx_len], sin_all[:ctx_len]
    cos_x, sin_x = cos_all[ctx_len:], sin_all[ctx_len:]
    i = 0

    pad = -(bsz + 1) % SUBLANES
    cond = jnp.concatenate([c, c_ctx[None, :], jnp.zeros((pad, d), F32)], axis=0)
    mod = _modulation(cond, w_ada[i], b_ada[i])
    sh1, sc1, g1, sh2, sc2, g2 = [mod[:bsz, j * d:(j + 1) * d].reshape(bsz, 1, d) for j in range(6)]
    csh1 = mod[bsz:bsz + 1, 0:d]
    csc1 = mod[bsz:bsz + 1, d:2 * d]

    lg = jax.nn.log_sigmoid(ret_decay_logit[i].astype(F32))
    w_in_b = w_in[i].astype(BF16)
    w_out_b = w_out[i].astype(BF16)
    k_off = 3 * cw + rw
    row = lambda a: a.reshape(1, -1)

    s_f, s_b = _context_states(lg, ctx, row(pre_mix_g[i]), csc1, csh1,
                               w_in_b[:, k_off:k_off + rw], w_in_b[:, k_off + rw:k_off + 2 * rw],
                               cos_c, sin_c)

    q, k, v, g, yb, conv = _mix_bwd(lg, x, row(pre_mix_g[i]), sc1, sh1, w_in_b,
                                    conv_w[i], row(conv_b[i]), cos_x, sin_x, s_b)

    wr = w_router[i].astype(F32)
    wr_hi = wr.astype(BF16)
    wr_lo = (wr - wr_hi.astype(F32)).astype(BF16)
    wr_cat = jnp.concatenate([wr_hi, wr_lo], axis=1)
    br = b_router[i].astype(F32).reshape(1, n_exp)
    x_mid, h_tiles, logits = _mix_fwd(lg, x, q, k, v, g, yb, conv, w_out_b, row(ret_norm_g[i]),
                                      row(post_mix_g[i]), row(pre_ffn_g[i]), g1, sc2, sh2,
                                      wr_cat, br, s_f)

    logits_t = logits.reshape(bsz, n // LANES, LANES, n_exp).transpose(0, 1, 3, 2)
    gate5, idx5 = _route(logits_t, cap)
    gate = gate5.transpose(0, 1, 3, 2, 4).reshape(bsz, n_exp, cap)
    idx = idx5.transpose(0, 1, 3, 2, 4).reshape(bsz * n_exp, 1, cap)

    groups = cap // SUBLANES
    xs = _dispatch(idx, h_tiles.reshape(bsz, n * ROW_TILE, LANES), n_exp, cap)
    y = _ffn(xs.reshape(bsz, n_exp, groups, GROUP, LANES), w_gate[i], w_up[i], w_down[i],
             gate.reshape(bsz, n_exp, cap, 1))
    acc = _combine(idx, y.reshape(bsz, n_exp, cap * ROW_TILE, LANES), n)
    return _final(acc.reshape(bsz, n // SUBLANES, GROUP, LANES), x_mid, row(post_ffn_g[i]), g2)
```

```python
import functools

import jax
import jax.numpy as jnp
from jax import lax
from jax.experimental import pallas as pl
from jax.experimental.pallas import tpu as pltpu

F32 = jnp.float32
BF16 = jnp.bfloat16

GRID_W = 64
RET_HEADS = 4
RET_CHUNK = 128
ROPE_BASE = 10000.0
EC_CAPACITY_FACTOR = 2
RMS_EPS = 1e-6
GN_EPS = 1e-6

LANES = 128
SUBLANES = 8
ROW_TILE = SUBLANES
GROUP = SUBLANES * ROW_TILE
VMEM_CAP = 60000 * 1024

TOKEN_BLOCK = 512
FFN_TILE = 512
GATHER_UNROLL = SUBLANES


def _vmem(nbytes):
    return min(int(nbytes), VMEM_CAP)


def _rms(x, g):
    return x * lax.rsqrt(jnp.mean(x * x, axis=-1, keepdims=True) + RMS_EPS) * g


def _silu(x):
    return x * (1.0 / (1.0 + jnp.exp(-x)))


def _rot_half(t):
    return pltpu.roll(t, LANES // 2, axis=1)


def _store_tile_major(ref, lead, val):
    rows = val.shape[0]
    for c in range(ROW_TILE):
        ref[lead + (slice(None), slice(c * SUBLANES, (c + 1) * SUBLANES), slice(None))] = (
            val[:, c * LANES:(c + 1) * LANES].reshape(rows // SUBLANES, SUBLANES, LANES))


def _row_base(r):
    return (r >> 3) * GROUP + (r & (SUBLANES - 1))


def _modulation_kernel(cond_ref, w_ref, b_ref, o_ref):
    s = _silu(cond_ref[...])
    o_ref[...] = jnp.dot(s, w_ref[...], preferred_element_type=F32,
                         precision=lax.Precision.HIGHEST) + b_ref[...]


def _modulation(cond, w_ada, b_ada):
    rows, d = cond.shape
    cols = w_ada.shape[1]
    tile = cols // 4
    return pl.pallas_call(
        _modulation_kernel,
        grid=(cols // tile,),
        in_specs=[
            pl.BlockSpec((rows, d), lambda j: (0, 0)),
            pl.BlockSpec((d, tile), lambda j: (0, j)),
            pl.BlockSpec((1, tile), lambda j: (0, j)),
        ],
        out_specs=pl.BlockSpec((rows, tile), lambda j: (0, j)),
        out_shape=jax.ShapeDtypeStruct((rows, cols), F32),
        compiler_params=pltpu.CompilerParams(
            dimension_semantics=("arbitrary",),
            vmem_limit_bytes=_vmem(3 * d * tile * 4)),
        name="modulation",
    )(cond, w_ada, b_ada.reshape(1, cols))


def _context_kernel(lg_ref, ctx_ref, gpre_ref, sc_ref, sh_ref, wk_ref, wv_ref,
                    cos_ref, sin_ref, sf_ref, sb_ref):
    n = ctx_ref.shape[1]
    hd = LANES
    hc = _rms(ctx_ref[0], gpre_ref[...]) * (1.0 + sc_ref[...]) + sh_ref[...]
    hc = hc.astype(BF16)
    k = jnp.dot(hc, wk_ref[...], preferred_element_type=F32)
    v = jnp.dot(hc, wv_ref[...], preferred_element_type=F32)
    cos = cos_ref[...]
    sin = sin_ref[...]
    pos = lax.broadcasted_iota(jnp.int32, (n, 1), 0).astype(F32)
    for h in range(RET_HEADS):
        kh = k[:, h * hd:(h + 1) * hd]
        kr = (kh * cos + _rot_half(kh) * sin) * (hd ** -0.5)
        vh = v[:, h * hd:(h + 1) * hd].astype(BF16)
        wf = jnp.exp(lg_ref[0, h] * (n - 1.0 - pos))
        wb = jnp.exp(lg_ref[1, h] * pos)
        kf = (kr * wf).T.astype(BF16)
        kb = (kr * wb).T.astype(BF16)
        sf_ref[0, h] = jnp.dot(kf, vh, preferred_element_type=F32)
        sb_ref[0, h] = jnp.dot(kb, vh, preferred_element_type=F32)


def _context_states(lg, ctx, gpre, csc1, csh1, wk, wv, cos, sin):
    b, n, d = ctx.shape
    hd = LANES
    st = jax.ShapeDtypeStruct((b, RET_HEADS, hd, hd), F32)
    vec = pl.BlockSpec((1, d), lambda i: (0, 0))
    return pl.pallas_call(
        _context_kernel,
        grid=(b,),
        in_specs=[
            pl.BlockSpec(memory_space=pltpu.SMEM),
            pl.BlockSpec((1, n, d), lambda i: (i, 0, 0)),
            vec, vec, vec,
            pl.BlockSpec(wk.shape, lambda i: (0, 0)),
            pl.BlockSpec(wv.shape, lambda i: (0, 0)),
            pl.BlockSpec((n, hd), lambda i: (0, 0)),
            pl.BlockSpec((n, hd), lambda i: (0, 0)),
        ],
        out_specs=[pl.BlockSpec((1, RET_HEADS, hd, hd), lambda i: (i, 0, 0, 0))] * 2,
        out_shape=[st, st],
        compiler_params=pltpu.CompilerParams(dimension_semantics=("arbitrary",)),
        name="context",
    )(lg, ctx, gpre, csc1, csh1, wk, wv, cos, sin)


def _chunk_tables(lg, forward):
    c = RET_CHUNK
    i = lax.broadcasted_iota(jnp.int32, (c, c), 0)
    j = lax.broadcasted_iota(jnp.int32, (c, c), 1)
    col = lax.broadcasted_iota(jnp.int32, (c, 1), 0).astype(F32)
    if forward:
        rel = (i - j).astype(F32)
        mask = i >= j
        xi = jnp.exp(lg * (col + 1.0))
        zeta = jnp.exp(lg * (c - 1.0 - col))
    else:
        rel = (j - i).astype(F32)
        mask = j > i
        xi = jnp.exp(lg * (c - col))
        zeta = jnp.exp(lg * col)
    decay = jnp.where(mask, jnp.exp(lg * jnp.where(mask, rel, 0.0)), 0.0)
    return decay, xi, zeta, jnp.exp(lg * c)


def _retention_chunk(qh, kh_f32, vh, s_ref, h, tables):
    decay, xi, zeta, g_chunk = tables
    kh = kh_f32.astype(BF16)
    scores = lax.dot_general(qh, kh, (((1,), (1,)), ((), ())), preferred_element_type=F32)
    intra = jnp.dot((scores * decay).astype(BF16), vh, preferred_element_type=F32)
    s_prev = s_ref[h]
    inter = jnp.dot(qh, s_prev.astype(BF16), preferred_element_type=F32) * xi
    kz = (kh_f32 * zeta).T.astype(BF16)
    s_ref[h] = g_chunk * s_prev + jnp.dot(kz, vh, preferred_element_type=F32)
    return intra + inter


def _mix_bwd_kernel(lg_ref, x_ref, gpre_ref, sc_ref, sh_ref, win_ref, cw_ref, cb_ref,
                    cos_ref, sin_ref, s0_ref,
                    q_ref, k_ref, v_ref, g_ref, yb_ref, conv_ref, s_ref):
    t = x_ref.shape[1]
    hd = LANES
    cw = conv_ref.shape[2]
    rw = RET_HEADS * hd

    @pl.when(pl.program_id(1) == 0)
    def _():
        s_ref[...] = s0_ref[0]

    hx = _rms(x_ref[0], gpre_ref[...]) * (1.0 + sc_ref[0]) + sh_ref[0]
    proj = jnp.dot(hx.astype(BF16), win_ref[...], preferred_element_type=F32)
    x_b = proj[:, 0:cw]
    x_c = proj[:, cw:2 * cw]
    x_x = proj[:, 2 * cw:3 * cw]
    off = 3 * cw
    q = proj[:, off:off + rw]
    k = proj[:, off + rw:off + 2 * rw]
    v = proj[:, off + 2 * rw:off + 3 * rw]
    g_ref[0] = proj[:, off + 3 * rw:off + 4 * rw]

    u = x_c * x_x
    col = lax.broadcasted_iota(jnp.int32, (t, cw), 0) & (GRID_W - 1)
    u_prev = jnp.where(col == 0, 0.0, pltpu.roll(u, 1, axis=0))
    u_next = jnp.where(col == GRID_W - 1, 0.0, pltpu.roll(u, t - 1, axis=0))
    y = u_prev * cw_ref[0:1, :] + u * cw_ref[1:2, :] + u_next * cw_ref[2:3, :] + cb_ref[...]
    conv_ref[0] = (x_b * y).astype(BF16)

    cos = cos_ref[...]
    sin = sin_ref[...]
    v_ref[0] = v.astype(BF16)
    for h in range(RET_HEADS):
        sl = slice(h * hd, (h + 1) * hd)
        qh = q[:, sl]
        kh = k[:, sl]
        qr = qh * cos + _rot_half(qh) * sin
        kr = (kh * cos + _rot_half(kh) * sin) * (hd ** -0.5)
        q_ref[0, :, sl] = qr.astype(BF16)
        k_ref[0, :, sl] = kr.astype(BF16)
        tables = _chunk_tables(lg_ref[1, h], forward=False)
        for ci in reversed(range(t // RET_CHUNK)):
            rows = slice(ci * RET_CHUNK, (ci + 1) * RET_CHUNK)
            yb_ref[0, rows, sl] = _retention_chunk(
                qr[rows].astype(BF16), kr[rows], v[rows, sl].astype(BF16), s_ref, h, tables)


def _mix_bwd(lg, x, gpre, sc1, sh1, w_in, conv_w, conv_b, cos, sin, s0_b):
    b, n, d = x.shape
    t = TOKEN_BLOCK
    nb = n // t
    hd = LANES
    cw = conv_w.shape[1]
    rw = RET_HEADS * hd
    rev = lambda i, j: (i, nb - 1 - j, 0)
    vec = pl.BlockSpec((1, d), lambda i, j: (0, 0))
    bvec = pl.BlockSpec((1, 1, d), lambda i, j: (i, 0, 0))
    tok = lambda w: pl.BlockSpec((1, t, w), rev)
    out = lambda w, dt: jax.ShapeDtypeStruct((b, n, w), dt)
    return pl.pallas_call(
        _mix_bwd_kernel,
        grid=(b, nb),
        in_specs=[
            pl.BlockSpec(memory_space=pltpu.SMEM),
            tok(d), vec, bvec, bvec,
            pl.BlockSpec(w_in.shape, lambda i, j: (0, 0)),
            pl.BlockSpec(conv_w.shape, lambda i, j: (0, 0)),
            pl.BlockSpec((1, cw), lambda i, j: (0, 0)),
            pl.BlockSpec((t, hd), lambda i, j: (nb - 1 - j, 0)),
            pl.BlockSpec((t, hd), lambda i, j: (nb - 1 - j, 0)),
            pl.BlockSpec((1, RET_HEADS, hd, hd), lambda i, j: (i, 0, 0, 0)),
        ],
        out_specs=[tok(rw), tok(rw), tok(rw), tok(rw), tok(rw), tok(cw)],
        out_shape=[out(rw, BF16), out(rw, BF16), out(rw, BF16), out(rw, F32),
                   out(rw, F32), out(cw, BF16)],
        scratch_shapes=[pltpu.VMEM((RET_HEADS, hd, hd), F32)],
        compiler_params=pltpu.CompilerParams(
            dimension_semantics=("arbitrary", "arbitrary"),
            vmem_limit_bytes=_vmem(48 * 2 ** 20)),
        name="mix_bwd",
    )(lg, x, gpre, sc1, sh1, w_in, conv_w, conv_b, cos, sin, s0_b)


def _mix_fwd_kernel(lg_ref, x_ref, q_ref, k_ref, v_ref, g_ref, yb_ref, conv_ref,
                    wout_ref, gn_ref, gpost_ref, gffn_ref, g1_ref, sc_ref, sh_ref,
                    wr_ref, br_ref, s0_ref,
                    xmid_ref, h2_ref, logit_ref, s_ref, mix_ref):
    t = x_ref.shape[1]
    hd = LANES
    cw = conv_ref.shape[2]

    @pl.when(pl.program_id(1) == 0)
    def _():
        s_ref[...] = s0_ref[0]

    mix_ref[:, 0:cw] = conv_ref[0]
    for h in range(RET_HEADS):
        sl = slice(h * hd, (h + 1) * hd)
        tables = _chunk_tables(lg_ref[0, h], forward=True)
        for ci in range(t // RET_CHUNK):
            rows = slice(ci * RET_CHUNK, (ci + 1) * RET_CHUNK)
            y = _retention_chunk(q_ref[0, rows, sl], k_ref[0, rows, sl].astype(F32),
                                 v_ref[0, rows, sl], s_ref, h, tables)
            y = y + yb_ref[0, rows, sl]
            mu = jnp.mean(y, axis=-1, keepdims=True)
            yc = y - mu
            var = jnp.mean(yc * yc, axis=-1, keepdims=True)
            yn = yc * lax.rsqrt(var + GN_EPS) * gn_ref[:, sl]
            mix_ref[rows, cw + h * hd:cw + (h + 1) * hd] = (
                _silu(g_ref[0, rows, sl]) * yn).astype(BF16)

    mix = jnp.dot(mix_ref[...], wout_ref[...], preferred_element_type=F32)
    x_mid = x_ref[0] + g1_ref[0] * _rms(mix, gpost_ref[...])
    xmid_ref[0] = x_mid
    h2 = _rms(x_mid, gffn_ref[...]) * (1.0 + sc_ref[0]) + sh_ref[0]
    ne = logit_ref.shape[2]
    h_hi = h2.astype(BF16)
    h_lo = (h2 - h_hi.astype(F32)).astype(BF16)
    p_hi = jnp.dot(h_hi, wr_ref[...], preferred_element_type=F32)
    p_lo = jnp.dot(h_lo, wr_ref[:, 0:ne], preferred_element_type=F32)
    logit_ref[0] = p_hi[:, 0:ne] + p_hi[:, ne:2 * ne] + p_lo + br_ref[...]
    _store_tile_major(h2_ref, (0,), h2)


def _mix_fwd(lg, x, q, k, v, g, yb, conv, w_out, gn, gpost, gffn, g1, sc2, sh2, wr, br, s0_f):
    b, n, d = x.shape
    t = TOKEN_BLOCK
    nb = n // t
    hd = LANES
    cw = conv.shape[2]
    rw = RET_HEADS * hd
    ne = br.shape[1]
    fwd = lambda i, j: (i, j, 0)
    vec = lambda w: pl.BlockSpec((1, w), lambda i, j: (0, 0))
    bvec = pl.BlockSpec((1, 1, d), lambda i, j: (i, 0, 0))
    tok = lambda w: pl.BlockSpec((1, t, w), fwd)
    return pl.pallas_call(
        _mix_fwd_kernel,
        grid=(b, nb),
        in_specs=[
            pl.BlockSpec(memory_space=pltpu.SMEM),
            tok(d), tok(rw), tok(rw), tok(rw), tok(rw), tok(rw), tok(cw),
            pl.BlockSpec(w_out.shape, lambda i, j: (0, 0)),
            vec(rw), vec(d), vec(d), bvec, bvec, bvec,
            pl.BlockSpec(wr.shape, lambda i, j: (0, 0)),
            vec(ne),
            pl.BlockSpec((1, RET_HEADS, hd, hd), lambda i, j: (i, 0, 0, 0)),
        ],
        out_specs=[tok(d),
                   pl.BlockSpec((1, t // SUBLANES, GROUP, LANES), lambda i, j: (i, j, 0, 0)),
                   tok(ne)],
        out_shape=[jax.ShapeDtypeStruct((b, n, d), F32),
                   jax.ShapeDtypeStruct((b, n // SUBLANES, GROUP, LANES), F32),
                   jax.ShapeDtypeStruct((b, n, ne), F32)],
        scratch_shapes=[pltpu.VMEM((RET_HEADS, hd, hd), F32),
                        pltpu.VMEM((t, cw + rw), BF16)],
        compiler_params=pltpu.CompilerParams(
            dimension_semantics=("arbitrary", "arbitrary"),
            vmem_limit_bytes=_vmem(48 * 2 ** 20)),
        name="mix_fwd",
    )(lg, x, q, k, v, g, yb, conv, w_out, gn, gpost, gffn, g1, sc2, sh2, wr, br, s0_f)


def _swap_halves(a, dv):
    nv = a.shape[0]
    a5 = a.reshape(nv // (2 * dv), 2, dv, a.shape[1], a.shape[2])
    p = jnp.concatenate([a5[:, 1:2], a5[:, 0:1]], axis=1)
    return p.reshape(a.shape)


def _route_kernel(logit_ref, gate_ref, idx_ref, *, cap):
    nv, ne, _ = logit_ref.shape[1:]
    grp = SUBLANES
    x = logit_ref[0]
    m = jnp.max(x, axis=1, keepdims=True)
    s = jnp.sum(jnp.exp(x - m), axis=1, keepdims=True)
    e0 = pl.multiple_of(pl.program_id(1) * grp, grp)
    xg = logit_ref[0, :, pl.ds(e0, grp), :]
    key = jnp.exp(xg - m) / s

    shape = (nv, grp, LANES)
    vi = lax.broadcasted_iota(jnp.int32, shape, 0)
    li = lax.broadcasted_iota(jnp.int32, shape, 2)
    tok = vi * LANES + li
    idx = tok
    n = nv * LANES
    span = 2
    while span <= n:
        dist = span // 2
        while dist >= 1:
            if dist >= LANES:
                pk = _swap_halves(key, dist // LANES)
                pi = _swap_halves(idx, dist // LANES)
            else:
                low = (li & dist) == 0
                pk = jnp.where(low, pltpu.roll(key, LANES - dist, axis=2),
                               pltpu.roll(key, dist, axis=2))
                pi = jnp.where(low, pltpu.roll(idx, LANES - dist, axis=2),
                               pltpu.roll(idx, dist, axis=2))
            better = (pk > key) | ((pk == key) & (pi < idx))
            is_low = (tok & dist) == 0
            if span < n:
                want_best = is_low == ((tok & span) == 0)
            else:
                want_best = is_low
            take = better == want_best
            key = jnp.where(take, pk, key)
            idx = jnp.where(take, pi, idx)
            dist //= 2
        span *= 2
    top = cap // LANES
    gate_ref[0, 0] = key[0:top]
    idx_ref[0, 0] = _row_base(idx[0:top])


def _route(logits_t, cap):
    b, nv, ne, _ = logits_t.shape
    grp = SUBLANES
    top = cap // LANES
    out_block = pl.BlockSpec((1, 1, top, grp, LANES), lambda i, j: (i, j, 0, 0, 0))
    return pl.pallas_call(
        functools.partial(_route_kernel, cap=cap),
        grid=(b, ne // grp),
        in_specs=[pl.BlockSpec((1, nv, ne, LANES), lambda i, j: (i, 0, 0, 0))],
        out_specs=[out_block, out_block],
        out_shape=[jax.ShapeDtypeStruct((b, ne // grp, top, grp, LANES), F32),
                   jax.ShapeDtypeStruct((b, ne // grp, top, grp, LANES), jnp.int32)],
        compiler_params=pltpu.CompilerParams(dimension_semantics=("arbitrary", "arbitrary")),
        name="route",
    )(logits_t)


def _dispatch_kernel(idx_ref, h_ref, xs_ref):
    cap = idx_ref.shape[2]

    def body(r, carry):
        for u in range(GATHER_UNROLL):
            src = idx_ref[0, 0, r * GATHER_UNROLL + u]
            xs_ref[0, 0, pl.ds(r * GROUP + u, ROW_TILE, stride=SUBLANES), :] = (
                h_ref[0, pl.ds(src, ROW_TILE, stride=SUBLANES), :])
        return carry

    lax.fori_loop(0, cap // GATHER_UNROLL, body, 0)


def _dispatch(idx, h_rows, n_exp, cap):
    b = h_rows.shape[0]
    rows = h_rows.shape[1]
    return pl.pallas_call(
        _dispatch_kernel,
        grid=(b, n_exp),
        in_specs=[
            pl.BlockSpec((1, 1, cap), lambda i, e: (i * n_exp + e, 0, 0),
                         memory_space=pltpu.SMEM),
            pl.BlockSpec((1, rows, LANES), lambda i, e: (i, 0, 0),
                         pipeline_mode=pl.Buffered(1)),
        ],
        out_specs=pl.BlockSpec((1, 1, cap * ROW_TILE, LANES), lambda i, e: (i, e, 0, 0)),
        out_shape=jax.ShapeDtypeStruct((b, n_exp, cap * ROW_TILE, LANES), F32),
        compiler_params=pltpu.CompilerParams(
            dimension_semantics=("arbitrary", "arbitrary"),
            vmem_limit_bytes=_vmem(rows * LANES * 4 + 3 * cap * ROW_TILE * LANES * 4)),
        name="dispatch",
    )(idx, h_rows)


def _ffn_kernel(xs_ref, wg_ref, wu_ref, wd_ref, gate_ref, y_ref, xm_ref, acc_ref):
    cap = xm_ref.shape[0]
    f = pl.program_id(2)

    @pl.when(f == 0)
    def _():
        for c in range(ROW_TILE):
            xm_ref[:, c * LANES:(c + 1) * LANES] = (
                xs_ref[0, 0, :, c * SUBLANES:(c + 1) * SUBLANES, :]
                .reshape(cap, LANES).astype(BF16))

    xm = xm_ref[...]
    a = jnp.dot(xm, wg_ref[0].astype(BF16), preferred_element_type=F32)
    u = jnp.dot(xm, wu_ref[0].astype(BF16), preferred_element_type=F32)
    hmid = (_silu(a) * u).astype(BF16)
    contrib = jnp.dot(hmid, wd_ref[0].astype(BF16), preferred_element_type=F32)

    @pl.when(f == 0)
    def _():
        acc_ref[...] = contrib

    @pl.when(f > 0)
    def _():
        acc_ref[...] += contrib

    @pl.when(f == pl.num_programs(2) - 1)
    def _():
        _store_tile_major(y_ref, (0, 0), acc_ref[...] * gate_ref[0, 0])


def _ffn(xs, w_gate, w_up, w_down, gate):
    b, n_exp, groups, _, _ = xs.shape
    cap = groups * SUBLANES
    d, dff = w_gate.shape[1:]
    tf = FFN_TILE
    row_block = pl.BlockSpec((1, 1, groups, GROUP, LANES), lambda e, i, f: (i, e, 0, 0, 0))
    return pl.pallas_call(
        _ffn_kernel,
        grid=(n_exp, b, dff // tf),
        in_specs=[
            row_block,
            pl.BlockSpec((1, d, tf), lambda e, i, f: (e, 0, f)),
            pl.BlockSpec((1, d, tf), lambda e, i, f: (e, 0, f)),
            pl.BlockSpec((1, tf, d), lambda e, i, f: (e, f, 0)),
            pl.BlockSpec((1, 1, cap, 1), lambda e, i, f: (i, e, 0, 0)),
        ],
        out_specs=row_block,
        out_shape=jax.ShapeDtypeStruct(xs.shape, F32),
        scratch_shapes=[pltpu.VMEM((cap, d), BF16), pltpu.VMEM((cap, d), F32)],
        compiler_params=pltpu.CompilerParams(
            dimension_semantics=("arbitrary", "arbitrary", "arbitrary"),
            vmem_limit_bytes=_vmem(52 * 2 ** 20)),
        name="ffn",
    )(xs, w_gate, w_up, w_down, gate)


def _combine_kernel(idx_ref, y_ref, xmid_ref, gpost_ref, g2_ref, o_ref, acc_ref, *, n_exp):
    cap = idx_ref.shape[2]
    t = xmid_ref.shape[1]
    step = pl.program_id(1)

    @pl.when(step == 0)
    def _():
        zrows = GROUP

        def zero(r, carry):
            acc_ref[pl.ds(pl.multiple_of(r * zrows, zrows), zrows), :] = jnp.zeros(
                (zrows, LANES), F32)
            return carry

        lax.fori_loop(0, acc_ref.shape[0] // zrows, zero, 0)

    @pl.when(step < n_exp)
    def _():
        def body(r, carry):
            offs, vals = [], []
            for u in range(GATHER_UNROLL):
                dst = idx_ref[0, 0, r * GATHER_UNROLL + u]
                offs.append(dst)
                vals.append(acc_ref[pl.ds(dst, ROW_TILE, stride=SUBLANES), :]
                            + y_ref[0, 0, pl.ds(r * GROUP + u, ROW_TILE, stride=SUBLANES), :])
            for dst, val in zip(offs, vals):
                acc_ref[pl.ds(dst, ROW_TILE, stride=SUBLANES), :] = val
            return carry

        lax.fori_loop(0, cap // GATHER_UNROLL, body, 0)

    @pl.when(step >= n_exp)
    def _():
        rows = t * ROW_TILE
        start = pl.multiple_of((step - n_exp) * rows, rows)
        blk = acc_ref[pl.ds(start, rows), :].reshape(t // SUBLANES, GROUP, LANES)
        ffn = jnp.concatenate(
            [blk[:, c * SUBLANES:(c + 1) * SUBLANES, :].reshape(t, LANES)
             for c in range(ROW_TILE)], axis=-1)
        o_ref[0] = xmid_ref[0] + g2_ref[0] * _rms(ffn, gpost_ref[...])


def _combine(idx, y, x_mid, gpost, g2):
    b, n_exp, rows, _ = y.shape
    n, d = x_mid.shape[1:]
    cap = rows // ROW_TILE
    t = TOKEN_BLOCK
    last = n_exp - 1
    tok_block = pl.BlockSpec((1, t, d), lambda i, s: (i, jnp.maximum(s - n_exp, 0), 0))
    return pl.pallas_call(
        functools.partial(_combine_kernel, n_exp=n_exp),
        grid=(b, n_exp + n // t),
        in_specs=[
            pl.BlockSpec((1, 1, cap), lambda i, s: (i * n_exp + jnp.minimum(s, last), 0, 0),
                         memory_space=pltpu.SMEM),
            pl.BlockSpec((1, 1, rows, LANES), lambda i, s: (i, jnp.minimum(s, last), 0, 0)),
            tok_block,
            pl.BlockSpec((1, d), lambda i, s: (0, 0)),
            pl.BlockSpec((1, 1, d), lambda i, s: (i, 0, 0)),
        ],
        out_specs=tok_block,
        out_shape=jax.ShapeDtypeStruct((b, n, d), F32),
        scratch_shapes=[pltpu.VMEM((n * ROW_TILE, LANES), F32)],
        compiler_params=pltpu.CompilerParams(
            dimension_semantics=("arbitrary", "arbitrary"),
            vmem_limit_bytes=_vmem(n * d * 4 + 2 * rows * LANES * 4 + 8 * t * d * 4)),
        name="combine",
    )(idx, y, x_mid, gpost, g2)


def _rope_tables(n_pos):
    half = LANES // 2
    inv = 1.0 / (ROPE_BASE ** (jnp.arange(half, dtype=F32) / half))
    ang_a = (jnp.arange(n_pos // LANES, dtype=F32) * LANES)[:, None] * inv[None, :]
    ang_b = jnp.arange(LANES, dtype=F32)[:, None] * inv[None, :]
    ca, sa = jnp.cos(ang_a)[:, None, :], jnp.sin(ang_a)[:, None, :]
    cb, sb = jnp.cos(ang_b)[None, :, :], jnp.sin(ang_b)[None, :, :]
    cos = (ca * cb - sa * sb).reshape(n_pos, half)
    sin = (sa * cb + ca * sb).reshape(n_pos, half)
    return jnp.concatenate([cos, cos], axis=-1), jnp.concatenate([-sin, sin], axis=-1)


def kernel(x, c, ctx, c_ctx, w_ada, b_ada, pre_mix_g, post_mix_g, pre_ffn_g, post_ffn_g,
           w_in, conv_w, conv_b, ret_decay_logit, ret_norm_g, w_out,
           w_router, b_router, w_gate, w_up, w_down):
    bsz, n, d = x.shape
    ctx_len = ctx.shape[1]
    depth = w_ada.shape[0]
    n_exp = w_router.shape[2]
    cw = conv_w.shape[2]
    rw = RET_HEADS * LANES
    cap = max(1, EC_CAPACITY_FACTOR * n // n_exp)
    assert depth == 1, "only the single-layer block is implemented"
    assert d == ROW_TILE * LANES and n % TOKEN_BLOCK == 0 and TOKEN_BLOCK % RET_CHUNK == 0
    assert n % GRID_W == 0 and TOKEN_BLOCK % GRID_W == 0 and ctx_len % SUBLANES == 0
    assert n_exp % SUBLANES == 0 and cap % LANES == 0 and (n // LANES) & (n // LANES - 1) == 0
    assert w_in.shape[2] == 3 * cw + 4 * rw

    assert (ctx_len + n) % LANES == 0
    cos_all, sin_all = _rope_tables(ctx_len + n)
    cos_c, sin_c = cos_all[:ctx_len], sin_all[:ctx_len]
    cos_x, sin_x = cos_all[ctx_len:], sin_all[ctx_len:]
    i = 0

    pad = -(bsz + 1) % SUBLANES
    cond = jnp.concatenate([c, c_ctx[None, :], jnp.zeros((pad, d), F32)], axis=0)
    mod = _modulation(cond, w_ada[i], b_ada[i])
    sh1, sc1, g1, sh2, sc2, g2 = [mod[:bsz, j * d:(j + 1) * d].reshape(bsz, 1, d) for j in range(6)]
    csh1 = mod[bsz:bsz + 1, 0:d]
    csc1 = mod[bsz:bsz + 1, d:2 * d]

    lg = jax.nn.log_sigmoid(ret_decay_logit[i].astype(F32))
    w_in_b = w_in[i].astype(BF16)
    w_out_b = w_out[i].astype(BF16)
    k_off = 3 * cw + rw
    row = lambda a: a.reshape(1, -1)

    s_f, s_b = _context_states(lg, ctx, row(pre_mix_g[i]), csc1, csh1,
                               w_in_b[:, k_off:k_off + rw], w_in_b[:, k_off + rw:k_off + 2 * rw],
                               cos_c, sin_c)

    q, k, v, g, yb, conv = _mix_bwd(lg, x, row(pre_mix_g[i]), sc1, sh1, w_in_b,
                                    conv_w[i], row(conv_b[i]), cos_x, sin_x, s_b)

    wr = w_router[i].astype(F32)
    wr_hi = wr.astype(BF16)
    wr_lo = (wr - wr_hi.astype(F32)).astype(BF16)
    wr_cat = jnp.concatenate([wr_hi, wr_lo], axis=1)
    br = b_router[i].astype(F32).reshape(1, n_exp)
    x_mid, h_tiles, logits = _mix_fwd(lg, x, q, k, v, g, yb, conv, w_out_b, row(ret_norm_g[i]),
                                      row(post_mix_g[i]), row(pre_ffn_g[i]), g1, sc2, sh2,
                                      wr_cat, br, s_f)

    logits_t = logits.reshape(bsz, n // LANES, LANES, n_exp).transpose(0, 1, 3, 2)
    gate5, idx5 = _route(logits_t, cap)
    gate = gate5.transpose(0, 1, 3, 2, 4).reshape(bsz, n_exp, cap)
    idx = idx5.transpose(0, 1, 3, 2, 4).reshape(bsz * n_exp, 1, cap)

    groups = cap // SUBLANES
    xs = _dispatch(idx, h_tiles.reshape(bsz, n * ROW_TILE, LANES), n_exp, cap)
    y = _ffn(xs.reshape(bsz, n_exp, groups, GROUP, LANES), w_gate[i], w_up[i], w_down[i],
             gate.reshape(bsz, n_exp, cap, 1))
    return _combine(idx, y.reshape(bsz, n_exp, cap * ROW_TILE, LANES), x_mid,
                    row(post_ffn_g[i]), g2)
```

```python
import functools

import jax
import jax.numpy as jnp
from jax import lax
from jax.experimental import pallas as pl
from jax.experimental.pallas import tpu as pltpu

F32 = jnp.float32
BF16 = jnp.bfloat16

GRID_W = 64
RET_HEADS = 4
RET_CHUNK = 128
ROPE_BASE = 10000.0
EC_CAPACITY_FACTOR = 2
RMS_EPS = 1e-6
GN_EPS = 1e-6

LANES = 128
SUBLANES = 8
ROW_TILE = SUBLANES
GROUP = SUBLANES * ROW_TILE
VMEM_CAP = 60000 * 1024

TOKEN_BLOCK = 512
FFN_TILE = 512
GATHER_UNROLL = SUBLANES


def _vmem(nbytes):
    return min(int(nbytes), VMEM_CAP)


def _rms(x, g):
    return x * lax.rsqrt(jnp.mean(x * x, axis=-1, keepdims=True) + RMS_EPS) * g


def _silu(x):
    return x * (1.0 / (1.0 + jnp.exp(-x)))


def _rot_half(t):
    return pltpu.roll(t, LANES // 2, axis=1)


def _store_tile_major(ref, lead, val):
    rows = val.shape[0]
    for c in range(ROW_TILE):
        ref[lead + (slice(None), slice(c * SUBLANES, (c + 1) * SUBLANES), slice(None))] = (
            val[:, c * LANES:(c + 1) * LANES].reshape(rows // SUBLANES, SUBLANES, LANES))


def _row_base(r):
    return (r >> 3) * GROUP + (r & (SUBLANES - 1))


def _modulation_kernel(cond_ref, w_ref, b_ref, o_ref):
    s = _silu(cond_ref[...])
    o_ref[...] = jnp.dot(s, w_ref[...], preferred_element_type=F32,
                         precision=lax.Precision.HIGHEST) + b_ref[...]


def _modulation(cond, w_ada, b_ada):
    rows, d = cond.shape
    cols = w_ada.shape[1]
    tile = cols // 4
    return pl.pallas_call(
        _modulation_kernel,
        grid=(cols // tile,),
        in_specs=[
            pl.BlockSpec((rows, d), lambda j: (0, 0)),
            pl.BlockSpec((d, tile), lambda j: (0, j)),
            pl.BlockSpec((1, tile), lambda j: (0, j)),
        ],
        out_specs=pl.BlockSpec((rows, tile), lambda j: (0, j)),
        out_shape=jax.ShapeDtypeStruct((rows, cols), F32),
        compiler_params=pltpu.CompilerParams(
            dimension_semantics=("arbitrary",),
            vmem_limit_bytes=_vmem(3 * d * tile * 4)),
        name="modulation",
    )(cond, w_ada, b_ada.reshape(1, cols))


def _context_kernel(lg_ref, ctx_ref, gpre_ref, sc_ref, sh_ref, wk_ref, wv_ref,
                    cos_ref, sin_ref, sf_ref, sb_ref):
    n = ctx_ref.shape[1]
    hd = LANES
    hc = _rms(ctx_ref[0], gpre_ref[...]) * (1.0 + sc_ref[...]) + sh_ref[...]
    hc = hc.astype(BF16)
    k = jnp.dot(hc, wk_ref[...], preferred_element_type=F32)
    v = jnp.dot(hc, wv_ref[...], preferred_element_type=F32)
    cos = cos_ref[...]
    sin = sin_ref[...]
    pos = lax.broadcasted_iota(jnp.int32, (n, 1), 0).astype(F32)
    for h in range(RET_HEADS):
        kh = k[:, h * hd:(h + 1) * hd]
        kr = (kh * cos + _rot_half(kh) * sin) * (hd ** -0.5)
        vh = v[:, h * hd:(h + 1) * hd].astype(BF16)
        wf = jnp.exp(lg_ref[0, h] * (n - 1.0 - pos))
        wb = jnp.exp(lg_ref[1, h] * pos)
        kf = (kr * wf).T.astype(BF16)
        kb = (kr * wb).T.astype(BF16)
        sf_ref[0, h] = jnp.dot(kf, vh, preferred_element_type=F32)
        sb_ref[0, h] = jnp.dot(kb, vh, preferred_element_type=F32)


def _context_states(lg, ctx, gpre, csc1, csh1, wk, wv, cos, sin):
    b, n, d = ctx.shape
    hd = LANES
    st = jax.ShapeDtypeStruct((b, RET_HEADS, hd, hd), F32)
    vec = pl.BlockSpec((1, d), lambda i: (0, 0))
    return pl.pallas_call(
        _context_kernel,
        grid=(b,),
        in_specs=[
            pl.BlockSpec(memory_space=pltpu.SMEM),
            pl.BlockSpec((1, n, d), lambda i: (i, 0, 0)),
            vec, vec, vec,
            pl.BlockSpec(wk.shape, lambda i: (0, 0)),
            pl.BlockSpec(wv.shape, lambda i: (0, 0)),
            pl.BlockSpec((n, hd), lambda i: (0, 0)),
            pl.BlockSpec((n, hd), lambda i: (0, 0)),
        ],
        out_specs=[pl.BlockSpec((1, RET_HEADS, hd, hd), lambda i: (i, 0, 0, 0))] * 2,
        out_shape=[st, st],
        compiler_params=pltpu.CompilerParams(dimension_semantics=("arbitrary",)),
        name="context",
    )(lg, ctx, gpre, csc1, csh1, wk, wv, cos, sin)


def _chunk_tables(lg, forward):
    c = RET_CHUNK
    i = lax.broadcasted_iota(jnp.int32, (c, c), 0)
    j = lax.broadcasted_iota(jnp.int32, (c, c), 1)
    col = lax.broadcasted_iota(jnp.int32, (c, 1), 0).astype(F32)
    if forward:
        rel = (i - j).astype(F32)
        mask = i >= j
        xi = jnp.exp(lg * (col + 1.0))
        zeta = jnp.exp(lg * (c - 1.0 - col))
    else:
        rel = (j - i).astype(F32)
        mask = j > i
        xi = jnp.exp(lg * (c - col))
        zeta = jnp.exp(lg * col)
    decay = jnp.where(mask, jnp.exp(lg * jnp.where(mask, rel, 0.0)), 0.0)
    return decay, xi, zeta, jnp.exp(lg * c)


def _retention_chunk(qh, kh_f32, vh, s_ref, h, tables):
    decay, xi, zeta, g_chunk = tables
    kh = kh_f32.astype(BF16)
    scores = lax.dot_general(qh, kh, (((1,), (1,)), ((), ())), preferred_element_type=F32)
    intra = jnp.dot((scores * decay).astype(BF16), vh, preferred_element_type=F32)
    s_prev = s_ref[h]
    inter = jnp.dot(qh, s_prev.astype(BF16), preferred_element_type=F32) * xi
    kz = (kh_f32 * zeta).T.astype(BF16)
    s_ref[h] = g_chunk * s_prev + jnp.dot(kz, vh, preferred_element_type=F32)
    return intra + inter


def _mix_bwd_kernel(lg_ref, x_ref, gpre_ref, sc_ref, sh_ref, win_ref, cw_ref, cb_ref,
                    cos_ref, sin_ref, s0_ref,
                    q_ref, k_ref, v_ref, g_ref, yb_ref, conv_ref, s_ref):
    t = x_ref.shape[1]
    hd = LANES
    cw = conv_ref.shape[2]
    rw = RET_HEADS * hd

    @pl.when(pl.program_id(1) == 0)
    def _():
        s_ref[...] = s0_ref[0]

    hx = _rms(x_ref[0], gpre_ref[...]) * (1.0 + sc_ref[0]) + sh_ref[0]
    proj = jnp.dot(hx.astype(BF16), win_ref[...], preferred_element_type=F32)
    x_b = proj[:, 0:cw]
    x_c = proj[:, cw:2 * cw]
    x_x = proj[:, 2 * cw:3 * cw]
    off = 3 * cw
    q = proj[:, off:off + rw]
    k = proj[:, off + rw:off + 2 * rw]
    v = proj[:, off + 2 * rw:off + 3 * rw]
    g_ref[0] = proj[:, off + 3 * rw:off + 4 * rw]

    u = x_c * x_x
    col = lax.broadcasted_iota(jnp.int32, (t, cw), 0) & (GRID_W - 1)
    u_prev = jnp.where(col == 0, 0.0, pltpu.roll(u, 1, axis=0))
    u_next = jnp.where(col == GRID_W - 1, 0.0, pltpu.roll(u, t - 1, axis=0))
    y = u_prev * cw_ref[0:1, :] + u * cw_ref[1:2, :] + u_next * cw_ref[2:3, :] + cb_ref[...]
    conv_ref[0] = (x_b * y).astype(BF16)

    cos = cos_ref[...]
    sin = sin_ref[...]
    v_ref[0] = v.astype(BF16)
    for h in range(RET_HEADS):
        sl = slice(h * hd, (h + 1) * hd)
        qh = q[:, sl]
        kh = k[:, sl]
        qr = qh * cos + _rot_half(qh) * sin
        kr = (kh * cos + _rot_half(kh) * sin) * (hd ** -0.5)
        q_ref[0, :, sl] = qr.astype(BF16)
        k_ref[0, :, sl] = kr.astype(BF16)
        tables = _chunk_tables(lg_ref[1, h], forward=False)
        for ci in reversed(range(t // RET_CHUNK)):
            rows = slice(ci * RET_CHUNK, (ci + 1) * RET_CHUNK)
            yb_ref[0, rows, sl] = _retention_chunk(
                qr[rows].astype(BF16), kr[rows], v[rows, sl].astype(BF16), s_ref, h, tables)


def _mix_bwd(lg, x, gpre, sc1, sh1, w_in, conv_w, conv_b, cos, sin, s0_b):
    b, n, d = x.shape
    t = TOKEN_BLOCK
    nb = n // t
    hd = LANES
    cw = conv_w.shape[1]
    rw = RET_HEADS * hd
    rev = lambda i, j: (i, nb - 1 - j, 0)
    vec = pl.BlockSpec((1, d), lambda i, j: (0, 0))
    bvec = pl.BlockSpec((1, 1, d), lambda i, j: (i, 0, 0))
    tok = lambda w: pl.BlockSpec((1, t, w), rev)
    out = lambda w, dt: jax.ShapeDtypeStruct((b, n, w), dt)
    return pl.pallas_call(
        _mix_bwd_kernel,
        grid=(b, nb),
        in_specs=[
            pl.BlockSpec(memory_space=pltpu.SMEM),
            tok(d), vec, bvec, bvec,
            pl.BlockSpec(w_in.shape, lambda i, j: (0, 0)),
            pl.BlockSpec(conv_w.shape, lambda i, j: (0, 0)),
            pl.BlockSpec((1, cw), lambda i, j: (0, 0)),
            pl.BlockSpec((t, hd), lambda i, j: (nb - 1 - j, 0)),
            pl.BlockSpec((t, hd), lambda i, j: (nb - 1 - j, 0)),
            pl.BlockSpec((1, RET_HEADS, hd, hd), lambda i, j: (i, 0, 0, 0)),
        ],
        out_specs=[tok(rw), tok(rw), tok(rw), tok(rw), tok(rw), tok(cw)],
        out_shape=[out(rw, BF16), out(rw, BF16), out(rw, BF16), out(rw, F32),
                   out(rw, F32), out(cw, BF16)],
        scratch_shapes=[pltpu.VMEM((RET_HEADS, hd, hd), F32)],
        compiler_params=pltpu.CompilerParams(
            dimension_semantics=("arbitrary", "arbitrary"),
            vmem_limit_bytes=_vmem(48 * 2 ** 20)),
        name="mix_bwd",
    )(lg, x, gpre, sc1, sh1, w_in, conv_w, conv_b, cos, sin, s0_b)


def _mix_fwd_kernel(lg_ref, x_ref, q_ref, k_ref, v_ref, g_ref, yb_ref, conv_ref,
                    wout_ref, gn_ref, gpost_ref, gffn_ref, g1_ref, sc_ref, sh_ref,
                    wr_ref, br_ref, s0_ref,
                    xmid_ref, h2_ref, logit_ref, s_ref, mix_ref):
    t = x_ref.shape[1]
    hd = LANES
    cw = conv_ref.shape[2]

    @pl.when(pl.program_id(1) == 0)
    def _():
        s_ref[...] = s0_ref[0]

    mix_ref[:, 0:cw] = conv_ref[0]
    for h in range(RET_HEADS):
        sl = slice(h * hd, (h + 1) * hd)
        tables = _chunk_tables(lg_ref[0, h], forward=True)
        for ci in range(t // RET_CHUNK):
            rows = slice(ci * RET_CHUNK, (ci + 1) * RET_CHUNK)
            y = _retention_chunk(q_ref[0, rows, sl], k_ref[0, rows, sl].astype(F32),
                                 v_ref[0, rows, sl], s_ref, h, tables)
            y = y + yb_ref[0, rows, sl]
            mu = jnp.mean(y, axis=-1, keepdims=True)
            yc = y - mu
            var = jnp.mean(yc * yc, axis=-1, keepdims=True)
            yn = yc * lax.rsqrt(var + GN_EPS) * gn_ref[:, sl]
            mix_ref[rows, cw + h * hd:cw + (h + 1) * hd] = (
                _silu(g_ref[0, rows, sl]) * yn).astype(BF16)

    mix = jnp.dot(mix_ref[...], wout_ref[...], preferred_element_type=F32)
    x_mid = x_ref[0] + g1_ref[0] * _rms(mix, gpost_ref[...])
    xmid_ref[0] = x_mid
    h2 = _rms(x_mid, gffn_ref[...]) * (1.0 + sc_ref[0]) + sh_ref[0]
    ne = logit_ref.shape[2]
    h_hi = h2.astype(BF16)
    h_lo = (h2 - h_hi.astype(F32)).astype(BF16)
    p_hi = jnp.dot(h_hi, wr_ref[...], preferred_element_type=F32)
    p_lo = jnp.dot(h_lo, wr_ref[...], preferred_element_type=F32)
    p = p_hi + p_lo
    logit_ref[0] = p[:, 0:ne] + p[:, ne:2 * ne] + br_ref[...]
    _store_tile_major(h2_ref, (0,), h2)


def _mix_fwd(lg, x, q, k, v, g, yb, conv, w_out, gn, gpost, gffn, g1, sc2, sh2, wr, br, s0_f):
    b, n, d = x.shape
    t = TOKEN_BLOCK
    nb = n // t
    hd = LANES
    cw = conv.shape[2]
    rw = RET_HEADS * hd
    ne = br.shape[1]
    fwd = lambda i, j: (i, j, 0)
    vec = lambda w: pl.BlockSpec((1, w), lambda i, j: (0, 0))
    bvec = pl.BlockSpec((1, 1, d), lambda i, j: (i, 0, 0))
    tok = lambda w: pl.BlockSpec((1, t, w), fwd)
    return pl.pallas_call(
        _mix_fwd_kernel,
        grid=(b, nb),
        in_specs=[
            pl.BlockSpec(memory_space=pltpu.SMEM),
            tok(d), tok(rw), tok(rw), tok(rw), tok(rw), tok(rw), tok(cw),
            pl.BlockSpec(w_out.shape, lambda i, j: (0, 0)),
            vec(rw), vec(d), vec(d), bvec, bvec, bvec,
            pl.BlockSpec(wr.shape, lambda i, j: (0, 0)),
            vec(ne),
            pl.BlockSpec((1, RET_HEADS, hd, hd), lambda i, j: (i, 0, 0, 0)),
        ],
        out_specs=[tok(d),
                   pl.BlockSpec((1, t // SUBLANES, GROUP, LANES), lambda i, j: (i, j, 0, 0)),
                   tok(ne)],
        out_shape=[jax.ShapeDtypeStruct((b, n, d), F32),
                   jax.ShapeDtypeStruct((b, n // SUBLANES, GROUP, LANES), F32),
                   jax.ShapeDtypeStruct((b, n, ne), F32)],
        scratch_shapes=[pltpu.VMEM((RET_HEADS, hd, hd), F32),
                        pltpu.VMEM((t, cw + rw), BF16)],
        compiler_params=pltpu.CompilerParams(
            dimension_semantics=("arbitrary", "arbitrary"),
            vmem_limit_bytes=_vmem(48 * 2 ** 20)),
        name="mix_fwd",
    )(lg, x, q, k, v, g, yb, conv, w_out, gn, gpost, gffn, g1, sc2, sh2, wr, br, s0_f)


def _swap_halves(a, dv):
    nv = a.shape[0]
    a5 = a.reshape(nv // (2 * dv), 2, dv, a.shape[1], a.shape[2])
    p = jnp.concatenate([a5[:, 1:2], a5[:, 0:1]], axis=1)
    return p.reshape(a.shape)


def _route_kernel(logit_ref, gate_ref, idx_ref, *, cap):
    nv, ne, _ = logit_ref.shape[1:]
    grp = SUBLANES
    x = logit_ref[0]
    m = jnp.max(x, axis=1, keepdims=True)
    s = jnp.sum(jnp.exp(x - m), axis=1, keepdims=True)
    e0 = pl.multiple_of(pl.program_id(1) * grp, grp)
    xg = logit_ref[0, :, pl.ds(e0, grp), :]
    key = jnp.exp(xg - m) / s

    shape = (nv, grp, LANES)
    vi = lax.broadcasted_iota(jnp.int32, shape, 0)
    li = lax.broadcasted_iota(jnp.int32, shape, 2)
    tok = vi * LANES + li
    idx = tok
    n = nv * LANES
    span = 2
    while span <= n:
        dist = span // 2
        while dist >= 1:
            if dist >= LANES:
                pk = _swap_halves(key, dist // LANES)
                pi = _swap_halves(idx, dist // LANES)
            else:
                low = (li & dist) == 0
                pk = jnp.where(low, pltpu.roll(key, LANES - dist, axis=2),
                               pltpu.roll(key, dist, axis=2))
                pi = jnp.where(low, pltpu.roll(idx, LANES - dist, axis=2),
                               pltpu.roll(idx, dist, axis=2))
            better = (pk > key) | ((pk == key) & (pi < idx))
            is_low = (tok & dist) == 0
            if span < n:
                want_best = is_low == ((tok & span) == 0)
            else:
                want_best = is_low
            take = better == want_best
            key = jnp.where(take, pk, key)
            idx = jnp.where(take, pi, idx)
            dist //= 2
        span *= 2
    top = cap // LANES
    gate_ref[0, 0] = key[0:top]
    idx_ref[0, 0] = _row_base(idx[0:top])


def _route(logits_t, cap):
    b, nv, ne, _ = logits_t.shape
    grp = SUBLANES
    top = cap // LANES
    out_block = pl.BlockSpec((1, 1, top, grp, LANES), lambda i, j: (i, j, 0, 0, 0))
    return pl.pallas_call(
        functools.partial(_route_kernel, cap=cap),
        grid=(b, ne // grp),
        in_specs=[pl.BlockSpec((1, nv, ne, LANES), lambda i, j: (i, 0, 0, 0))],
        out_specs=[out_block, out_block],
        out_shape=[jax.ShapeDtypeStruct((b, ne // grp, top, grp, LANES), F32),
                   jax.ShapeDtypeStruct((b, ne // grp, top, grp, LANES), jnp.int32)],
        compiler_params=pltpu.CompilerParams(dimension_semantics=("arbitrary", "arbitrary")),
        name="route",
    )(logits_t)


BF16_ROWS = 2 * SUBLANES


def _dispatch_kernel(idx_ref, h_ref, xs_ref, stage_ref):
    cap = idx_ref.shape[2]

    def body(r, carry):
        for u in range(BF16_ROWS):
            src = idx_ref[0, 0, r * BF16_ROWS + u]
            dst = (u // SUBLANES) * GROUP + u % SUBLANES
            stage_ref[pl.ds(dst, ROW_TILE, stride=SUBLANES), :] = (
                h_ref[0, pl.ds(src, ROW_TILE, stride=SUBLANES), :])
        row0 = pl.multiple_of(r * BF16_ROWS, BF16_ROWS)
        for c in range(ROW_TILE):
            lo = stage_ref[c * SUBLANES:(c + 1) * SUBLANES, :]
            hi = stage_ref[GROUP + c * SUBLANES:GROUP + (c + 1) * SUBLANES, :]
            xs_ref[0, 0, pl.ds(row0, BF16_ROWS), c * LANES:(c + 1) * LANES] = (
                jnp.concatenate([lo, hi], axis=0).astype(BF16))
        return carry

    lax.fori_loop(0, cap // BF16_ROWS, body, 0)


def _dispatch(idx, h_rows, n_exp, cap):
    b = h_rows.shape[0]
    rows = h_rows.shape[1]
    d = ROW_TILE * LANES
    return pl.pallas_call(
        _dispatch_kernel,
        grid=(b, n_exp),
        in_specs=[
            pl.BlockSpec((1, 1, cap), lambda i, e: (i * n_exp + e, 0, 0),
                         memory_space=pltpu.SMEM),
            pl.BlockSpec((1, rows, LANES), lambda i, e: (i, 0, 0),
                         pipeline_mode=pl.Buffered(1)),
        ],
        out_specs=pl.BlockSpec((1, 1, cap, d), lambda i, e: (i, e, 0, 0)),
        out_shape=jax.ShapeDtypeStruct((b, n_exp, cap, d), BF16),
        scratch_shapes=[pltpu.VMEM((2 * GROUP, LANES), F32)],
        compiler_params=pltpu.CompilerParams(
            dimension_semantics=("arbitrary", "arbitrary"),
            vmem_limit_bytes=_vmem(rows * LANES * 4 + 4 * cap * d * 2)),
        name="dispatch",
    )(idx, h_rows)


def _ffn_kernel(xs_ref, wg_ref, wu_ref, wd_ref, gate_ref, y_ref):
    nb = xs_ref.shape[0]

    @pl.when(pl.program_id(1) == 0)
    def _():
        y_ref[...] = jnp.zeros(y_ref.shape, F32)

    wg = wg_ref[0].astype(BF16)
    wu = wu_ref[0].astype(BF16)
    wd = wd_ref[0].astype(BF16)
    for b in range(nb):
        xm = xs_ref[b, 0]
        a = jnp.dot(xm, wg, preferred_element_type=F32)
        u = jnp.dot(xm, wu, preferred_element_type=F32)
        hmid = (_silu(a) * u * gate_ref[b, 0]).astype(BF16)
        contrib = jnp.dot(hmid, wd, preferred_element_type=F32)
        rows = contrib.shape[0]
        for c in range(ROW_TILE):
            y_ref[b, 0, :, c * SUBLANES:(c + 1) * SUBLANES, :] += (
                contrib[:, c * LANES:(c + 1) * LANES].reshape(rows // SUBLANES, SUBLANES, LANES))


def _ffn(xs, w_gate, w_up, w_down, gate):
    b, n_exp, cap, d = xs.shape
    dff = w_gate.shape[2]
    tf = FFN_TILE
    groups = cap // SUBLANES
    return pl.pallas_call(
        _ffn_kernel,
        grid=(n_exp, dff // tf),
        in_specs=[
            pl.BlockSpec((b, 1, cap, d), lambda e, f: (0, e, 0, 0)),
            pl.BlockSpec((1, d, tf), lambda e, f: (e, 0, f)),
            pl.BlockSpec((1, d, tf), lambda e, f: (e, 0, f)),
            pl.BlockSpec((1, tf, d), lambda e, f: (e, f, 0)),
            pl.BlockSpec((b, 1, cap, 1), lambda e, f: (0, e, 0, 0)),
        ],
        out_specs=pl.BlockSpec((b, 1, groups, GROUP, LANES), lambda e, f: (0, e, 0, 0, 0)),
        out_shape=jax.ShapeDtypeStruct((b, n_exp, groups, GROUP, LANES), F32),
        compiler_params=pltpu.CompilerParams(
            dimension_semantics=("arbitrary", "arbitrary"),
            vmem_limit_bytes=_vmem(56 * 2 ** 20)),
        name="ffn",
    )(xs, w_gate, w_up, w_down, gate)


def _combine_kernel(idx_ref, y_ref, xmid_ref, gpost_ref, g2_ref, o_ref, acc_ref, *, n_exp):
    cap = idx_ref.shape[2]
    t = xmid_ref.shape[1]
    step = pl.program_id(1)

    @pl.when(step == 0)
    def _():
        zrows = GROUP

        def zero(r, carry):
            acc_ref[pl.ds(pl.multiple_of(r * zrows, zrows), zrows), :] = jnp.zeros(
                (zrows, LANES), F32)
            return carry

        lax.fori_loop(0, acc_ref.shape[0] // zrows, zero, 0)

    @pl.when(step < n_exp)
    def _():
        def body(r, carry):
            offs, vals = [], []
            for u in range(GATHER_UNROLL):
                dst = idx_ref[0, 0, r * GATHER_UNROLL + u]
                offs.append(dst)
                vals.append(acc_ref[pl.ds(dst, ROW_TILE, stride=SUBLANES), :]
                            + y_ref[0, 0, pl.ds(r * GROUP + u, ROW_TILE, stride=SUBLANES), :])
            for dst, val in zip(offs, vals):
                acc_ref[pl.ds(dst, ROW_TILE, stride=SUBLANES), :] = val
            return carry

        lax.fori_loop(0, cap // GATHER_UNROLL, body, 0)

    @pl.when(step >= n_exp)
    def _():
        rows = t * ROW_TILE
        start = pl.multiple_of((step - n_exp) * rows, rows)
        blk = acc_ref[pl.ds(start, rows), :].reshape(t // SUBLANES, GROUP, LANES)
        ffn = jnp.concatenate(
            [blk[:, c * SUBLANES:(c + 1) * SUBLANES, :].reshape(t, LANES)
             for c in range(ROW_TILE)], axis=-1)
        o_ref[0] = xmid_ref[0] + g2_ref[0] * _rms(ffn, gpost_ref[...])


def _combine(idx, y, x_mid, gpost, g2):
    b, n_exp, rows, _ = y.shape
    n, d = x_mid.shape[1:]
    cap = rows // ROW_TILE
    t = TOKEN_BLOCK
    last = n_exp - 1
    tok_block = pl.BlockSpec((1, t, d), lambda i, s: (i, jnp.maximum(s - n_exp, 0), 0))
    return pl.pallas_call(
        functools.partial(_combine_kernel, n_exp=n_exp),
        grid=(b, n_exp + n // t),
        in_specs=[
            pl.BlockSpec((1, 1, cap), lambda i, s: (i * n_exp + jnp.minimum(s, last), 0, 0),
                         memory_space=pltpu.SMEM),
            pl.BlockSpec((1, 1, rows, LANES), lambda i, s: (i, jnp.minimum(s, last), 0, 0)),
            tok_block,
            pl.BlockSpec((1, d), lambda i, s: (0, 0)),
            pl.BlockSpec((1, 1, d), lambda i, s: (i, 0, 0)),
        ],
        out_specs=tok_block,
        out_shape=jax.ShapeDtypeStruct((b, n, d), F32),
        scratch_shapes=[pltpu.VMEM((n * ROW_TILE, LANES), F32)],
        compiler_params=pltpu.CompilerParams(
            dimension_semantics=("arbitrary", "arbitrary"),
            vmem_limit_bytes=_vmem(n * d * 4 + 2 * rows * LANES * 4 + 8 * t * d * 4)),
        name="combine",
    )(idx, y, x_mid, gpost, g2)


def _rope_tables(n_pos):
    half = LANES // 2
    inv = 1.0 / (ROPE_BASE ** (jnp.arange(half, dtype=F32) / half))
    ang_a = (jnp.arange(n_pos // LANES, dtype=F32) * LANES)[:, None] * inv[None, :]
    ang_b = jnp.arange(LANES, dtype=F32)[:, None] * inv[None, :]
    ca, sa = jnp.cos(ang_a)[:, None, :], jnp.sin(ang_a)[:, None, :]
    cb, sb = jnp.cos(ang_b)[None, :, :], jnp.sin(ang_b)[None, :, :]
    cos = (ca * cb - sa * sb).reshape(n_pos, half)
    sin = (sa * cb + ca * sb).reshape(n_pos, half)
    return jnp.concatenate([cos, cos], axis=-1), jnp.concatenate([-sin, sin], axis=-1)


def kernel(x, c, ctx, c_ctx, w_ada, b_ada, pre_mix_g, post_mix_g, pre_ffn_g, post_ffn_g,
           w_in, conv_w, conv_b, ret_decay_logit, ret_norm_g, w_out,
           w_router, b_router, w_gate, w_up, w_down):
    bsz, n, d = x.shape
    ctx_len = ctx.shape[1]
    depth = w_ada.shape[0]
    n_exp = w_router.shape[2]
    cw = conv_w.shape[2]
    rw = RET_HEADS * LANES
    cap = max(1, EC_CAPACITY_FACTOR * n // n_exp)
    assert depth == 1, "only the single-layer block is implemented"
    assert d == ROW_TILE * LANES and n % TOKEN_BLOCK == 0 and TOKEN_BLOCK % RET_CHUNK == 0
    assert n % GRID_W == 0 and TOKEN_BLOCK % GRID_W == 0 and ctx_len % SUBLANES == 0
    assert n_exp % SUBLANES == 0 and cap % LANES == 0 and (n // LANES) & (n // LANES - 1) == 0
    assert w_in.shape[2] == 3 * cw + 4 * rw

    assert (ctx_len + n) % LANES == 0
    cos_all, sin_all = _rope_tables(ctx_len + n)
    cos_c, sin_c = cos_all[:ctx_len], sin_all[:ctx_len]
    cos_x, sin_x = cos_all[ctx_len:], sin_all[ctx_len:]
    i = 0

    pad = -(bsz + 1) % SUBLANES
    cond = jnp.concatenate([c, c_ctx[None, :], jnp.zeros((pad, d), F32)], axis=0)
    mod = _modulation(cond, w_ada[i], b_ada[i])
    sh1, sc1, g1, sh2, sc2, g2 = [mod[:bsz, j * d:(j + 1) * d].reshape(bsz, 1, d) for j in range(6)]
    csh1 = mod[bsz:bsz + 1, 0:d]
    csc1 = mod[bsz:bsz + 1, d:2 * d]

    lg = jax.nn.log_sigmoid(ret_decay_logit[i].astype(F32))
    w_in_b = w_in[i].astype(BF16)
    w_out_b = w_out[i].astype(BF16)
    k_off = 3 * cw + rw
    row = lambda a: a.reshape(1, -1)

    s_f, s_b = _context_states(lg, ctx, row(pre_mix_g[i]), csc1, csh1,
                               w_in_b[:, k_off:k_off + rw], w_in_b[:, k_off + rw:k_off + 2 * rw],
                               cos_c, sin_c)

    q, k, v, g, yb, conv = _mix_bwd(lg, x, row(pre_mix_g[i]), sc1, sh1, w_in_b,
                                    conv_w[i], row(conv_b[i]), cos_x, sin_x, s_b)

    wr = w_router[i].astype(F32)
    wr_hi = wr.astype(BF16)
    wr_lo = (wr - wr_hi.astype(F32)).astype(BF16)
    wr_cat = jnp.concatenate([wr_hi, wr_lo], axis=1)
    br = b_router[i].astype(F32).reshape(1, n_exp)
    x_mid, h_tiles, logits = _mix_fwd(lg, x, q, k, v, g, yb, conv, w_out_b, row(ret_norm_g[i]),
                                      row(post_mix_g[i]), row(pre_ffn_g[i]), g1, sc2, sh2,
                                      wr_cat, br, s_f)

    logits_t = logits.reshape(bsz, n // LANES, LANES, n_exp).transpose(0, 1, 3, 2)
    gate5, idx5 = _route(logits_t, cap)
    gate = gate5.transpose(0, 1, 3, 2, 4).reshape(bsz, n_exp, cap)
    idx = idx5.transpose(0, 1, 3, 2, 4).reshape(bsz * n_exp, 1, cap)

    xs = _dispatch(idx, h_tiles.reshape(bsz, n * ROW_TILE, LANES), n_exp, cap)
    y = _ffn(xs, w_gate[i], w_up[i], w_down[i], gate.reshape(bsz, n_exp, cap, 1))
    return _combine(idx, y.reshape(bsz, n_exp, cap * ROW_TILE, LANES), x_mid,
                    row(post_ffn_g[i]), g2)
```

```python
import functools

import jax
import jax.numpy as jnp
from jax import lax
from jax.experimental import pallas as pl
from jax.experimental.pallas import tpu as pltpu

F32 = jnp.float32
BF16 = jnp.bfloat16

GRID_W = 64
RET_HEADS = 4
ROPE_BASE = 10000.0
EC_CAPACITY_FACTOR = 2
RMS_EPS = 1e-6
GN_EPS = 1e-6

LANES = 128
SUBLANES = 8
ROW_TILE = SUBLANES
GROUP = SUBLANES * ROW_TILE
VMEM_CAP = 60000 * 1024

TOKEN_BLOCK = 512
FFN_TILE = 512
GATHER_UNROLL = SUBLANES


def _vmem(nbytes):
    return min(int(nbytes), VMEM_CAP)


def _rms(x, g):
    return x * lax.rsqrt(jnp.mean(x * x, axis=-1, keepdims=True) + RMS_EPS) * g


def _silu(x):
    return x * (1.0 / (1.0 + jnp.exp(-x)))


def _rot_half(t):
    return pltpu.roll(t, LANES // 2, axis=1)


def _store_tile_major(ref, lead, val):
    rows = val.shape[0]
    for c in range(ROW_TILE):
        ref[lead + (slice(None), slice(c * SUBLANES, (c + 1) * SUBLANES), slice(None))] = (
            val[:, c * LANES:(c + 1) * LANES].reshape(rows // SUBLANES, SUBLANES, LANES))


def _row_base(r):
    return (r >> 3) * GROUP + (r & (SUBLANES - 1))


def _modulation_kernel(cond_ref, w_ref, b_ref, o_ref):
    s = _silu(cond_ref[...])
    o_ref[...] = jnp.dot(s, w_ref[...], preferred_element_type=F32,
                         precision=lax.Precision.HIGHEST) + b_ref[...]


def _modulation(cond, w_ada, b_ada):
    rows, d = cond.shape
    cols = w_ada.shape[1]
    tile = cols // 4
    return pl.pallas_call(
        _modulation_kernel,
        grid=(cols // tile,),
        in_specs=[
            pl.BlockSpec((rows, d), lambda j: (0, 0)),
            pl.BlockSpec((d, tile), lambda j: (0, j)),
            pl.BlockSpec((1, tile), lambda j: (0, j)),
        ],
        out_specs=pl.BlockSpec((rows, tile), lambda j: (0, j)),
        out_shape=jax.ShapeDtypeStruct((rows, cols), F32),
        compiler_params=pltpu.CompilerParams(
            dimension_semantics=("arbitrary",),
            vmem_limit_bytes=_vmem(3 * d * tile * 4)),
        name="modulation",
    )(cond, w_ada, b_ada.reshape(1, cols))


def _context_kernel(lg_ref, ctx_ref, gpre_ref, sc_ref, sh_ref, wk_ref, wv_ref,
                    cos_ref, sin_ref, sf_ref, sb_ref):
    n = ctx_ref.shape[1]
    hd = LANES
    hc = _rms(ctx_ref[0], gpre_ref[...]) * (1.0 + sc_ref[...]) + sh_ref[...]
    hc = hc.astype(BF16)
    k = jnp.dot(hc, wk_ref[...], preferred_element_type=F32)
    v = jnp.dot(hc, wv_ref[...], preferred_element_type=F32)
    cos = cos_ref[...]
    sin = sin_ref[...]
    pos = lax.broadcasted_iota(jnp.int32, (n, 1), 0).astype(F32)
    for h in range(RET_HEADS):
        kh = k[:, h * hd:(h + 1) * hd]
        kr = (kh * cos + _rot_half(kh) * sin) * (hd ** -0.5)
        vh = v[:, h * hd:(h + 1) * hd].astype(BF16)
        wf = jnp.exp(lg_ref[0, h] * (n - 1.0 - pos))
        wb = jnp.exp(lg_ref[1, h] * pos)
        kf = (kr * wf).T.astype(BF16)
        kb = (kr * wb).T.astype(BF16)
        sf_ref[0, h] = jnp.dot(kf, vh, preferred_element_type=F32)
        sb_ref[0, h] = jnp.dot(kb, vh, preferred_element_type=F32)


def _context_states(lg, ctx, gpre, csc1, csh1, wk, wv, cos, sin):
    b, n, d = ctx.shape
    hd = LANES
    st = jax.ShapeDtypeStruct((b, RET_HEADS, hd, hd), F32)
    vec = pl.BlockSpec((1, d), lambda i: (0, 0))
    return pl.pallas_call(
        _context_kernel,
        grid=(b,),
        in_specs=[
            pl.BlockSpec(memory_space=pltpu.SMEM),
            pl.BlockSpec((1, n, d), lambda i: (i, 0, 0)),
            vec, vec, vec,
            pl.BlockSpec(wk.shape, lambda i: (0, 0)),
            pl.BlockSpec(wv.shape, lambda i: (0, 0)),
            pl.BlockSpec((n, hd), lambda i: (0, 0)),
            pl.BlockSpec((n, hd), lambda i: (0, 0)),
        ],
        out_specs=[pl.BlockSpec((1, RET_HEADS, hd, hd), lambda i: (i, 0, 0, 0))] * 2,
        out_shape=[st, st],
        compiler_params=pltpu.CompilerParams(dimension_semantics=("arbitrary",)),
        name="context",
    )(lg, ctx, gpre, csc1, csh1, wk, wv, cos, sin)


def _fill_tables(lg_ref, direction, decay_ref, xi_ref, zeta_ref):
    c = decay_ref.shape[1]
    i = lax.broadcasted_iota(jnp.int32, (c, c), 0)
    j = lax.broadcasted_iota(jnp.int32, (c, c), 1)
    col = lax.broadcasted_iota(jnp.int32, (c, 1), 0).astype(F32)
    for h in range(RET_HEADS):
        lg = lg_ref[direction, h]
        if direction == 0:
            rel = (i - j).astype(F32)
            mask = i >= j
            xi_ref[h] = jnp.exp(lg * (col + 1.0))
            zeta_ref[h] = jnp.exp(lg * (c - 1.0 - col))
        else:
            rel = (j - i).astype(F32)
            mask = j > i
            xi_ref[h] = jnp.exp(lg * (c - col))
            zeta_ref[h] = jnp.exp(lg * col)
        decay_ref[h] = jnp.where(mask, jnp.exp(lg * jnp.where(mask, rel, 0.0)), 0.0)


def _retention_block(qh, kh_f32, vh, lg, s_ref, decay_ref, xi_ref, zeta_ref, h):
    c = decay_ref.shape[1]
    kh = kh_f32.astype(BF16)
    scores = lax.dot_general(qh, kh, (((1,), (1,)), ((), ())), preferred_element_type=F32)
    s_prev = s_ref[h]
    lhs = jnp.concatenate([(scores * decay_ref[h]).astype(BF16),
                           (qh.astype(F32) * xi_ref[h]).astype(BF16)], axis=1)
    rhs = jnp.concatenate([vh, s_prev.astype(BF16)], axis=0)
    out = jnp.dot(lhs, rhs, preferred_element_type=F32)
    kz = (kh_f32 * zeta_ref[h]).T.astype(BF16)
    s_ref[h] = jnp.exp(lg * c) * s_prev + jnp.dot(kz, vh, preferred_element_type=F32)
    return out


def _mix_bwd_kernel(lg_ref, x_ref, gpre_ref, sc_ref, sh_ref, win_ref, cw_ref, cb_ref,
                    cos_ref, sin_ref, s0_ref,
                    q_ref, k_ref, v_ref, g_ref, yb_ref, conv_ref,
                    s_ref, qs_ref, ks_ref, vs_ref, decay_ref, xi_ref, zeta_ref, *, nb):
    t = x_ref.shape[1]
    hd = LANES
    cw = conv_ref.shape[2]
    rw = RET_HEADS * hd
    step = pl.program_id(0)
    cur = step % 2
    prev = 1 - cur

    @pl.when(step == 0)
    def _():
        s_ref[...] = jnp.zeros(s_ref.shape, F32)
        qs_ref[prev] = jnp.zeros(qs_ref.shape[1:], BF16)
        ks_ref[prev] = jnp.zeros(ks_ref.shape[1:], F32)
        vs_ref[prev] = jnp.zeros(vs_ref.shape[1:], BF16)
        _fill_tables(lg_ref, 1, decay_ref, xi_ref, zeta_ref)

    @pl.when(step % nb == 1)
    def _():
        s_ref[...] = s0_ref[0]

    for h in range(RET_HEADS):
        sl = slice(h * hd, (h + 1) * hd)
        yb_ref[0, :, sl] = _retention_block(
            qs_ref[prev, :, sl], ks_ref[prev, :, sl], vs_ref[prev, :, sl],
            lg_ref[1, h], s_ref, decay_ref, xi_ref, zeta_ref, h)

    hx = _rms(x_ref[0], gpre_ref[...]) * (1.0 + sc_ref[0]) + sh_ref[0]
    proj = jnp.dot(hx.astype(BF16), win_ref[...], preferred_element_type=F32)
    x_b = proj[:, 0:cw]
    x_c = proj[:, cw:2 * cw]
    x_x = proj[:, 2 * cw:3 * cw]
    off = 3 * cw
    q = proj[:, off:off + rw]
    k = proj[:, off + rw:off + 2 * rw]
    v = proj[:, off + 2 * rw:off + 3 * rw]
    g_ref[0] = proj[:, off + 3 * rw:off + 4 * rw]

    u = x_c * x_x
    col = lax.broadcasted_iota(jnp.int32, (t, cw), 0) & (GRID_W - 1)
    u_prev = jnp.where(col == 0, 0.0, pltpu.roll(u, 1, axis=0))
    u_next = jnp.where(col == GRID_W - 1, 0.0, pltpu.roll(u, t - 1, axis=0))
    y = u_prev * cw_ref[0:1, :] + u * cw_ref[1:2, :] + u_next * cw_ref[2:3, :] + cb_ref[...]
    conv_ref[0] = (x_b * y).astype(BF16)

    cos = cos_ref[...]
    sin = sin_ref[...]
    v_bf = v.astype(BF16)
    v_ref[0] = v_bf
    vs_ref[cur] = v_bf
    for h in range(RET_HEADS):
        sl = slice(h * hd, (h + 1) * hd)
        qh = q[:, sl]
        kh = k[:, sl]
        qr = (qh * cos + _rot_half(qh) * sin).astype(BF16)
        kr = (kh * cos + _rot_half(kh) * sin) * (hd ** -0.5)
        q_ref[0, :, sl] = qr
        qs_ref[cur, :, sl] = qr
        k_ref[0, :, sl] = kr.astype(BF16)
        ks_ref[cur, :, sl] = kr


def _mix_bwd(lg, x, gpre, sc1, sh1, w_in, conv_w, conv_b, cos, sin, s0_b):
    b, n, d = x.shape
    t = TOKEN_BLOCK
    nb = n // t
    steps = b * nb
    hd = LANES
    cw = conv_w.shape[1]
    rw = RET_HEADS * hd
    proj = lambda s: jnp.minimum(s, steps - 1)
    retn = lambda s: jnp.maximum(s - 1, 0)
    blk = lambda g: (g // nb, nb - 1 - g % nb, 0)
    vec = pl.BlockSpec((1, d), lambda s: (0, 0))
    bvec = pl.BlockSpec((1, 1, d), lambda s: (proj(s) // nb, 0, 0))
    tok = lambda w: pl.BlockSpec((1, t, w), lambda s: blk(proj(s)))
    rope = pl.BlockSpec((t, hd), lambda s: (nb - 1 - proj(s) % nb, 0))
    out = lambda w, dt: jax.ShapeDtypeStruct((b, n, w), dt)
    return pl.pallas_call(
        functools.partial(_mix_bwd_kernel, nb=nb),
        grid=(steps + 1,),
        in_specs=[
            pl.BlockSpec(memory_space=pltpu.SMEM),
            tok(d), vec, bvec, bvec,
            pl.BlockSpec(w_in.shape, lambda s: (0, 0)),
            pl.BlockSpec(conv_w.shape, lambda s: (0, 0)),
            pl.BlockSpec((1, cw), lambda s: (0, 0)),
            rope, rope,
            pl.BlockSpec((1, RET_HEADS, hd, hd), lambda s: (retn(s) // nb, 0, 0, 0)),
        ],
        out_specs=[tok(rw), tok(rw), tok(rw), tok(rw),
                   pl.BlockSpec((1, t, rw), lambda s: blk(retn(s))), tok(cw)],
        out_shape=[out(rw, BF16), out(rw, BF16), out(rw, BF16), out(rw, F32),
                   out(rw, F32), out(cw, BF16)],
        scratch_shapes=[pltpu.VMEM((RET_HEADS, hd, hd), F32),
                        pltpu.VMEM((2, t, rw), BF16), pltpu.VMEM((2, t, rw), F32),
                        pltpu.VMEM((2, t, rw), BF16),
                        pltpu.VMEM((RET_HEADS, t, t), F32),
                        pltpu.VMEM((RET_HEADS, t, 1), F32),
                        pltpu.VMEM((RET_HEADS, t, 1), F32)],
        compiler_params=pltpu.CompilerParams(
            dimension_semantics=("arbitrary",),
            vmem_limit_bytes=_vmem(52 * 2 ** 20)),
        name="mix_bwd",
    )(lg, x, gpre, sc1, sh1, w_in, conv_w, conv_b, cos, sin, s0_b)


def _mix_fwd_kernel(lg_ref, x_ref, q_ref, k_ref, v_ref, g_ref, yb_ref, conv_ref,
                    wout_ref, gn_ref, gpost_ref, gffn_ref, g1_ref, sc_ref, sh_ref,
                    wr_ref, br_ref, s0_ref,
                    xmid_ref, h2_ref, logit_ref, s_ref, mix_ref,
                    decay_ref, xi_ref, zeta_ref, *, nb):
    t = x_ref.shape[1]
    hd = LANES
    cw = conv_ref.shape[2]
    step = pl.program_id(0)
    cur = step % 2
    prev = 1 - cur

    @pl.when(step == 0)
    def _():
        mix_ref[prev] = jnp.zeros(mix_ref.shape[1:], BF16)
        _fill_tables(lg_ref, 0, decay_ref, xi_ref, zeta_ref)

    @pl.when(step % nb == 0)
    def _():
        s_ref[...] = s0_ref[0]

    mix_ref[cur, :, 0:cw] = conv_ref[0]
    for h in range(RET_HEADS):
        sl = slice(h * hd, (h + 1) * hd)
        y = _retention_block(q_ref[0, :, sl], k_ref[0, :, sl].astype(F32), v_ref[0, :, sl],
                             lg_ref[0, h], s_ref, decay_ref, xi_ref, zeta_ref, h)
        y = y + yb_ref[0, :, sl]
        mu = jnp.mean(y, axis=-1, keepdims=True)
        yc = y - mu
        var = jnp.mean(yc * yc, axis=-1, keepdims=True)
        yn = yc * lax.rsqrt(var + GN_EPS) * gn_ref[:, sl]
        mix_ref[cur, :, cw + h * hd:cw + (h + 1) * hd] = (
            _silu(g_ref[0, :, sl]) * yn).astype(BF16)

    mix = jnp.dot(mix_ref[prev], wout_ref[...], preferred_element_type=F32)
    x_mid = x_ref[0] + g1_ref[0] * _rms(mix, gpost_ref[...])
    xmid_ref[0] = x_mid
    h2 = _rms(x_mid, gffn_ref[...]) * (1.0 + sc_ref[0]) + sh_ref[0]
    ne = logit_ref.shape[2]
    h_hi = h2.astype(BF16)
    h_lo = (h2 - h_hi.astype(F32)).astype(BF16)
    p_hi = jnp.dot(h_hi, wr_ref[...], preferred_element_type=F32)
    p_lo = jnp.dot(h_lo, wr_ref[...], preferred_element_type=F32)
    p = p_hi + p_lo
    logit_ref[0] = p[:, 0:ne] + p[:, ne:2 * ne] + br_ref[...]
    _store_tile_major(h2_ref, (0,), h2)


def _mix_fwd(lg, x, q, k, v, g, yb, conv, w_out, gn, gpost, gffn, g1, sc2, sh2, wr, br, s0_f):
    b, n, d = x.shape
    t = TOKEN_BLOCK
    nb = n // t
    hd = LANES
    cw = conv.shape[2]
    rw = RET_HEADS * hd
    ne = br.shape[1]
    steps = b * nb
    ret = lambda s: jnp.minimum(s, steps - 1)
    outp = lambda s: jnp.maximum(s - 1, 0)
    vec = lambda w: pl.BlockSpec((1, w), lambda s: (0, 0))
    bvec = pl.BlockSpec((1, 1, d), lambda s: (outp(s) // nb, 0, 0))
    rtok = lambda w: pl.BlockSpec((1, t, w), lambda s: (ret(s) // nb, ret(s) % nb, 0))
    otok = lambda w: pl.BlockSpec((1, t, w), lambda s: (outp(s) // nb, outp(s) % nb, 0))
    return pl.pallas_call(
        functools.partial(_mix_fwd_kernel, nb=nb),
        grid=(steps + 1,),
        in_specs=[
            pl.BlockSpec(memory_space=pltpu.SMEM),
            otok(d), rtok(rw), rtok(rw), rtok(rw), rtok(rw), rtok(rw), rtok(cw),
            pl.BlockSpec(w_out.shape, lambda s: (0, 0)),
            vec(rw), vec(d), vec(d), bvec, bvec, bvec,
            pl.BlockSpec(wr.shape, lambda s: (0, 0)),
            vec(ne),
            pl.BlockSpec((1, RET_HEADS, hd, hd), lambda s: (ret(s) // nb, 0, 0, 0)),
        ],
        out_specs=[otok(d),
                   pl.BlockSpec((1, t // SUBLANES, GROUP, LANES),
                                lambda s: (outp(s) // nb, outp(s) % nb, 0, 0)),
                   otok(ne)],
        out_shape=[jax.ShapeDtypeStruct((b, n, d), F32),
                   jax.ShapeDtypeStruct((b, n // SUBLANES, GROUP, LANES), F32),
                   jax.ShapeDtypeStruct((b, n, ne), F32)],
        scratch_shapes=[pltpu.VMEM((RET_HEADS, hd, hd), F32),
                        pltpu.VMEM((2, t, cw + rw), BF16),
                        pltpu.VMEM((RET_HEADS, t, t), F32),
                        pltpu.VMEM((RET_HEADS, t, 1), F32),
                        pltpu.VMEM((RET_HEADS, t, 1), F32)],
        compiler_params=pltpu.CompilerParams(
            dimension_semantics=("arbitrary",),
            vmem_limit_bytes=_vmem(48 * 2 ** 20)),
        name="mix_fwd",
    )(lg, x, q, k, v, g, yb, conv, w_out, gn, gpost, gffn, g1, sc2, sh2, wr, br, s0_f)


def _swap_halves(a, dv):
    nv = a.shape[0]
    a5 = a.reshape(nv // (2 * dv), 2, dv, a.shape[1], a.shape[2])
    p = jnp.concatenate([a5[:, 1:2], a5[:, 0:1]], axis=1)
    return p.reshape(a.shape)


def _route_kernel(logit_ref, gate_ref, idx_ref, *, cap):
    nv, ne, _ = logit_ref.shape[1:]
    grp = SUBLANES
    x = logit_ref[0]
    m = jnp.max(x, axis=1, keepdims=True)
    s = jnp.sum(jnp.exp(x - m), axis=1, keepdims=True)
    e0 = pl.multiple_of(pl.program_id(1) * grp, grp)
    xg = logit_ref[0, :, pl.ds(e0, grp), :]
    key = jnp.exp(xg - m) / s

    shape = (nv, grp, LANES)
    vi = lax.broadcasted_iota(jnp.int32, shape, 0)
    li = lax.broadcasted_iota(jnp.int32, shape, 2)
    tok = vi * LANES + li
    idx = tok
    n = nv * LANES
    span = 2
    while span <= n:
        dist = span // 2
        while dist >= 1:
            if dist >= LANES:
                pk = _swap_halves(key, dist // LANES)
                pi = _swap_halves(idx, dist // LANES)
            else:
                low = (li & dist) == 0
                pk = jnp.where(low, pltpu.roll(key, LANES - dist, axis=2),
                               pltpu.roll(key, dist, axis=2))
                pi = jnp.where(low, pltpu.roll(idx, LANES - dist, axis=2),
                               pltpu.roll(idx, dist, axis=2))
            better = (pk > key) | ((pk == key) & (pi < idx))
            is_low = (tok & dist) == 0
            if span < n:
                want_best = is_low == ((tok & span) == 0)
            else:
                want_best = is_low
            take = better == want_best
            key = jnp.where(take, pk, key)
            idx = jnp.where(take, pi, idx)
            dist //= 2
        span *= 2
    top = cap // LANES
    gate_ref[0, 0] = key[0:top]
    idx_ref[0, 0] = _row_base(idx[0:top])


def _route(logits_t, cap):
    b, nv, ne, _ = logits_t.shape
    grp = SUBLANES
    top = cap // LANES
    out_block = pl.BlockSpec((1, 1, top, grp, LANES), lambda i, j: (i, j, 0, 0, 0))
    return pl.pallas_call(
        functools.partial(_route_kernel, cap=cap),
        grid=(b, ne // grp),
        in_specs=[pl.BlockSpec((1, nv, ne, LANES), lambda i, j: (i, 0, 0, 0))],
        out_specs=[out_block, out_block],
        out_shape=[jax.ShapeDtypeStruct((b, ne // grp, top, grp, LANES), F32),
                   jax.ShapeDtypeStruct((b, ne // grp, top, grp, LANES), jnp.int32)],
        compiler_params=pltpu.CompilerParams(dimension_semantics=("arbitrary", "arbitrary")),
        name="route",
    )(logits_t)


BF16_ROWS = 2 * SUBLANES


def _dispatch_kernel(idx_ref, h_ref, xs_ref, stage_ref):
    cap = idx_ref.shape[2]

    def body(r, carry):
        for u in range(BF16_ROWS):
            src = idx_ref[0, 0, r * BF16_ROWS + u]
            dst = (u // SUBLANES) * GROUP + u % SUBLANES
            stage_ref[pl.ds(dst, ROW_TILE, stride=SUBLANES), :] = (
                h_ref[0, pl.ds(src, ROW_TILE, stride=SUBLANES), :])
        row0 = pl.multiple_of(r * BF16_ROWS, BF16_ROWS)
        for c in range(ROW_TILE):
            lo = stage_ref[c * SUBLANES:(c + 1) * SUBLANES, :]
            hi = stage_ref[GROUP + c * SUBLANES:GROUP + (c + 1) * SUBLANES, :]
            xs_ref[0, 0, pl.ds(row0, BF16_ROWS), c * LANES:(c + 1) * LANES] = (
                jnp.concatenate([lo, hi], axis=0).astype(BF16))
        return carry

    lax.fori_loop(0, cap // BF16_ROWS, body, 0)


def _dispatch(idx, h_rows, n_exp, cap):
    b = h_rows.shape[0]
    rows = h_rows.shape[1]
    d = ROW_TILE * LANES
    return pl.pallas_call(
        _dispatch_kernel,
        grid=(b, n_exp),
        in_specs=[
            pl.BlockSpec((1, 1, cap), lambda i, e: (i * n_exp + e, 0, 0),
                         memory_space=pltpu.SMEM),
            pl.BlockSpec((1, rows, LANES), lambda i, e: (i, 0, 0),
                         pipeline_mode=pl.Buffered(1)),
        ],
        out_specs=pl.BlockSpec((1, 1, cap, d), lambda i, e: (i, e, 0, 0)),
        out_shape=jax.ShapeDtypeStruct((b, n_exp, cap, d), BF16),
        scratch_shapes=[pltpu.VMEM((2 * GROUP, LANES), F32)],
        compiler_params=pltpu.CompilerParams(
            dimension_semantics=("arbitrary", "arbitrary"),
            vmem_limit_bytes=_vmem(rows * LANES * 4 + 4 * cap * d * 2)),
        name="dispatch",
    )(idx, h_rows)


def _ffn_kernel(xs_ref, wg_ref, wu_ref, wd_ref, gate_ref, y_ref):
    nb = xs_ref.shape[0]

    @pl.when(pl.program_id(1) == 0)
    def _():
        y_ref[...] = jnp.zeros(y_ref.shape, F32)

    wg = wg_ref[0].astype(BF16)
    wu = wu_ref[0].astype(BF16)
    wd = wd_ref[0].astype(BF16)
    for b in range(nb):
        xm = xs_ref[b, 0]
        a = jnp.dot(xm, wg, preferred_element_type=F32)
        u = jnp.dot(xm, wu, preferred_element_type=F32)
        hmid = (_silu(a) * u * gate_ref[b, 0]).astype(BF16)
        contrib = jnp.dot(hmid, wd, preferred_element_type=F32)
        rows = contrib.shape[0]
        for c in range(ROW_TILE):
            y_ref[b, 0, :, c * SUBLANES:(c + 1) * SUBLANES, :] += (
                contrib[:, c * LANES:(c + 1) * LANES].reshape(rows // SUBLANES, SUBLANES, LANES))


def _ffn(xs, w_gate, w_up, w_down, gate):
    b, n_exp, cap, d = xs.shape
    dff = w_gate.shape[2]
    tf = FFN_TILE
    groups = cap // SUBLANES
    return pl.pallas_call(
        _ffn_kernel,
        grid=(n_exp, dff // tf),
        in_specs=[
            pl.BlockSpec((b, 1, cap, d), lambda e, f: (0, e, 0, 0)),
            pl.BlockSpec((1, d, tf), lambda e, f: (e, 0, f)),
            pl.BlockSpec((1, d, tf), lambda e, f: (e, 0, f)),
            pl.BlockSpec((1, tf, d), lambda e, f: (e, f, 0)),
            pl.BlockSpec((b, 1, cap, 1), lambda e, f: (0, e, 0, 0)),
        ],
        out_specs=pl.BlockSpec((b, 1, groups, GROUP, LANES), lambda e, f: (0, e, 0, 0, 0)),
        out_shape=jax.ShapeDtypeStruct((b, n_exp, groups, GROUP, LANES), F32),
        compiler_params=pltpu.CompilerParams(
            dimension_semantics=("arbitrary", "arbitrary"),
            vmem_limit_bytes=_vmem(56 * 2 ** 20)),
        name="ffn",
    )(xs, w_gate, w_up, w_down, gate)


def _combine_kernel(idx_ref, y_ref, xmid_ref, gpost_ref, g2_ref, o_ref, acc_ref, *, n_exp):
    cap = idx_ref.shape[2]
    t = xmid_ref.shape[1]
    step = pl.program_id(1)

    @pl.when(step == 0)
    def _():
        zrows = GROUP

        def zero(r, carry):
            acc_ref[pl.ds(pl.multiple_of(r * zrows, zrows), zrows), :] = jnp.zeros(
                (zrows, LANES), F32)
            return carry

        lax.fori_loop(0, acc_ref.shape[0] // zrows, zero, 0)

    @pl.when(step < n_exp)
    def _():
        def body(r, carry):
            offs, vals = [], []
            for u in range(GATHER_UNROLL):
                dst = idx_ref[0, 0, r * GATHER_UNROLL + u]
                offs.append(dst)
                vals.append(acc_ref[pl.ds(dst, ROW_TILE, stride=SUBLANES), :]
                            + y_ref[0, 0, pl.ds(r * GROUP + u, ROW_TILE, stride=SUBLANES), :])
            for dst, val in zip(offs, vals):
                acc_ref[pl.ds(dst, ROW_TILE, stride=SUBLANES), :] = val
            return carry

        lax.fori_loop(0, cap // GATHER_UNROLL, body, 0)

    @pl.when(step >= n_exp)
    def _():
        rows = t * ROW_TILE
        start = pl.multiple_of((step - n_exp) * rows, rows)
        blk = acc_ref[pl.ds(start, rows), :].reshape(t // SUBLANES, GROUP, LANES)
        ffn = jnp.concatenate(
            [blk[:, c * SUBLANES:(c + 1) * SUBLANES, :].reshape(t, LANES)
             for c in range(ROW_TILE)], axis=-1)
        o_ref[0] = xmid_ref[0] + g2_ref[0] * _rms(ffn, gpost_ref[...])


def _combine(idx, y, x_mid, gpost, g2):
    b, n_exp, rows, _ = y.shape
    n, d = x_mid.shape[1:]
    cap = rows // ROW_TILE
    t = TOKEN_BLOCK
    last = n_exp - 1
    tok_block = pl.BlockSpec((1, t, d), lambda i, s: (i, jnp.maximum(s - n_exp, 0), 0))
    return pl.pallas_call(
        functools.partial(_combine_kernel, n_exp=n_exp),
        grid=(b, n_exp + n // t),
        in_specs=[
            pl.BlockSpec((1, 1, cap), lambda i, s: (i * n_exp + jnp.minimum(s, last), 0, 0),
                         memory_space=pltpu.SMEM),
            pl.BlockSpec((1, 1, rows, LANES), lambda i, s: (i, jnp.minimum(s, last), 0, 0)),
            tok_block,
            pl.BlockSpec((1, d), lambda i, s: (0, 0)),
            pl.BlockSpec((1, 1, d), lambda i, s: (i, 0, 0)),
        ],
        out_specs=tok_block,
        out_shape=jax.ShapeDtypeStruct((b, n, d), F32),
        scratch_shapes=[pltpu.VMEM((n * ROW_TILE, LANES), F32)],
        compiler_params=pltpu.CompilerParams(
            dimension_semantics=("arbitrary", "arbitrary"),
            vmem_limit_bytes=_vmem(n * d * 4 + 2 * rows * LANES * 4 + 8 * t * d * 4)),
        name="combine",
    )(idx, y, x_mid, gpost, g2)


def _rope_tables(n_pos):
    half = LANES // 2
    inv = 1.0 / (ROPE_BASE ** (jnp.arange(half, dtype=F32) / half))
    ang_a = (jnp.arange(n_pos // LANES, dtype=F32) * LANES)[:, None] * inv[None, :]
    ang_b = jnp.arange(LANES, dtype=F32)[:, None] * inv[None, :]
    ca, sa = jnp.cos(ang_a)[:, None, :], jnp.sin(ang_a)[:, None, :]
    cb, sb = jnp.cos(ang_b)[None, :, :], jnp.sin(ang_b)[None, :, :]
    cos = (ca * cb - sa * sb).reshape(n_pos, half)
    sin = (sa * cb + ca * sb).reshape(n_pos, half)
    return jnp.concatenate([cos, cos], axis=-1), jnp.concatenate([-sin, sin], axis=-1)


def kernel(x, c, ctx, c_ctx, w_ada, b_ada, pre_mix_g, post_mix_g, pre_ffn_g, post_ffn_g,
           w_in, conv_w, conv_b, ret_decay_logit, ret_norm_g, w_out,
           w_router, b_router, w_gate, w_up, w_down):
    bsz, n, d = x.shape
    ctx_len = ctx.shape[1]
    depth = w_ada.shape[0]
    n_exp = w_router.shape[2]
    cw = conv_w.shape[2]
    rw = RET_HEADS * LANES
    cap = max(1, EC_CAPACITY_FACTOR * n // n_exp)
    assert depth == 1, "only the single-layer block is implemented"
    assert d == ROW_TILE * LANES and n % TOKEN_BLOCK == 0 and TOKEN_BLOCK % LANES == 0
    assert n % GRID_W == 0 and TOKEN_BLOCK % GRID_W == 0 and ctx_len % SUBLANES == 0
    assert n_exp % SUBLANES == 0 and cap % LANES == 0 and (n // LANES) & (n // LANES - 1) == 0
    assert w_in.shape[2] == 3 * cw + 4 * rw

    assert (ctx_len + n) % LANES == 0
    cos_all, sin_all = _rope_tables(ctx_len + n)
    cos_c, sin_c = cos_all[:ctx_len], sin_all[:ctx_len]
    cos_x, sin_x = cos_all[ctx_len:], sin_all[ctx_len:]
    i = 0

    pad = -(bsz + 1) % SUBLANES
    cond = jnp.concatenate([c, c_ctx[None, :], jnp.zeros((pad, d), F32)], axis=0)
    mod = _modulation(cond, w_ada[i], b_ada[i])
    sh1, sc1, g1, sh2, sc2, g2 = [mod[:bsz, j * d:(j + 1) * d].reshape(bsz, 1, d) for j in range(6)]
    csh1 = mod[bsz:bsz + 1, 0:d]
    csc1 = mod[bsz:bsz + 1, d:2 * d]

    lg = jax.nn.log_sigmoid(ret_decay_logit[i].astype(F32))
    w_in_b = w_in[i].astype(BF16)
    w_out_b = w_out[i].astype(BF16)
    k_off = 3 * cw + rw
    row = lambda a: a.reshape(1, -1)

    s_f, s_b = _context_states(lg, ctx, row(pre_mix_g[i]), csc1, csh1,
                               w_in_b[:, k_off:k_off + rw], w_in_b[:, k_off + rw:k_off + 2 * rw],
                               cos_c, sin_c)

    q, k, v, g, yb, conv = _mix_bwd(lg, x, row(pre_mix_g[i]), sc1, sh1, w_in_b,
                                    conv_w[i], row(conv_b[i]), cos_x, sin_x, s_b)

    wr = w_router[i].astype(F32)
    wr_hi = wr.astype(BF16)
    wr_lo = (wr - wr_hi.astype(F32)).astype(BF16)
    wr_cat = jnp.concatenate([wr_hi, wr_lo], axis=1)
    br = b_router[i].astype(F32).reshape(1, n_exp)
    x_mid, h_tiles, logits = _mix_fwd(lg, x, q, k, v, g, yb, conv, w_out_b, row(ret_norm_g[i]),
                                      row(post_mix_g[i]), row(pre_ffn_g[i]), g1, sc2, sh2,
                                      wr_cat, br, s_f)

    logits_t = logits.reshape(bsz, n // LANES, LANES, n_exp).transpose(0, 1, 3, 2)
    gate5, idx5 = _route(logits_t, cap)
    gate = gate5.transpose(0, 1, 3, 2, 4).reshape(bsz, n_exp, cap)
    idx = idx5.transpose(0, 1, 3, 2, 4).reshape(bsz * n_exp, 1, cap)

    xs = _dispatch(idx, h_tiles.reshape(bsz, n * ROW_TILE, LANES), n_exp, cap)
    y = _ffn(xs, w_gate[i], w_up[i], w_down[i], gate.reshape(bsz, n_exp, cap, 1))
    return _combine(idx, y.reshape(bsz, n_exp, cap * ROW_TILE, LANES), x_mid,
                    row(post_ffn_g[i]), g2)
```

```python
import functools

import jax
import jax.numpy as jnp
from jax import lax
from jax.experimental import pallas as pl
from jax.experimental.pallas import tpu as pltpu

F32 = jnp.float32
BF16 = jnp.bfloat16

GRID_W = 64
RET_HEADS = 4
ROPE_BASE = 10000.0
EC_CAPACITY_FACTOR = 2
RMS_EPS = 1e-6
GN_EPS = 1e-6

LANES = 128
SUBLANES = 8
ROW_TILE = SUBLANES
GROUP = SUBLANES * ROW_TILE
VMEM_CAP = 60000 * 1024

TOKEN_BLOCK = 512
FFN_TILE = 512
GATHER_UNROLL = SUBLANES


def _vmem(nbytes):
    return min(int(nbytes), VMEM_CAP)


def _rms(x, g):
    return x * lax.rsqrt(jnp.mean(x * x, axis=-1, keepdims=True) + RMS_EPS) * g


def _silu(x):
    return x * (1.0 / (1.0 + jnp.exp(-x)))


def _rot_half(t):
    return pltpu.roll(t, LANES // 2, axis=1)


def _store_tile_major(ref, lead, val):
    rows = val.shape[0]
    for c in range(ROW_TILE):
        ref[lead + (slice(None), slice(c * SUBLANES, (c + 1) * SUBLANES), slice(None))] = (
            val[:, c * LANES:(c + 1) * LANES].reshape(rows // SUBLANES, SUBLANES, LANES))


def _row_base(r):
    return (r >> 3) * GROUP + (r & (SUBLANES - 1))


def _modulation_kernel(cond_ref, w_ref, b_ref, o_ref):
    s = _silu(cond_ref[...])
    o_ref[...] = jnp.dot(s, w_ref[...], preferred_element_type=F32,
                         precision=lax.Precision.HIGHEST) + b_ref[...]


def _modulation(cond, w_ada, b_ada):
    rows, d = cond.shape
    cols = w_ada.shape[1]
    tile = cols // 4
    return pl.pallas_call(
        _modulation_kernel,
        grid=(cols // tile,),
        in_specs=[
            pl.BlockSpec((rows, d), lambda j: (0, 0)),
            pl.BlockSpec((d, tile), lambda j: (0, j)),
            pl.BlockSpec((1, tile), lambda j: (0, j)),
        ],
        out_specs=pl.BlockSpec((rows, tile), lambda j: (0, j)),
        out_shape=jax.ShapeDtypeStruct((rows, cols), F32),
        compiler_params=pltpu.CompilerParams(
            dimension_semantics=("arbitrary",),
            vmem_limit_bytes=_vmem(3 * d * tile * 4)),
        name="modulation",
    )(cond, w_ada, b_ada.reshape(1, cols))


def _context_kernel(lg_ref, ctx_ref, gpre_ref, sc_ref, sh_ref, wk_ref, wv_ref,
                    cos_ref, sin_ref, sf_ref, sb_ref):
    n = ctx_ref.shape[1]
    hd = LANES
    hc = _rms(ctx_ref[0], gpre_ref[...]) * (1.0 + sc_ref[...]) + sh_ref[...]
    hc = hc.astype(BF16)
    k = jnp.dot(hc, wk_ref[...], preferred_element_type=F32)
    v = jnp.dot(hc, wv_ref[...], preferred_element_type=F32)
    cos = cos_ref[...]
    sin = sin_ref[...]
    pos = lax.broadcasted_iota(jnp.int32, (n, 1), 0).astype(F32)
    for h in range(RET_HEADS):
        kh = k[:, h * hd:(h + 1) * hd]
        kr = (kh * cos + _rot_half(kh) * sin) * (hd ** -0.5)
        vh = v[:, h * hd:(h + 1) * hd].astype(BF16)
        wf = jnp.exp(lg_ref[0, h] * (n - 1.0 - pos))
        wb = jnp.exp(lg_ref[1, h] * pos)
        kf = (kr * wf).T.astype(BF16)
        kb = (kr * wb).T.astype(BF16)
        sf_ref[0, h] = jnp.dot(kf, vh, preferred_element_type=F32)
        sb_ref[0, h] = jnp.dot(kb, vh, preferred_element_type=F32)


def _context_states(lg, ctx, gpre, csc1, csh1, wk, wv, cos, sin):
    b, n, d = ctx.shape
    hd = LANES
    st = jax.ShapeDtypeStruct((b, RET_HEADS, hd, hd), F32)
    vec = pl.BlockSpec((1, d), lambda i: (0, 0))
    return pl.pallas_call(
        _context_kernel,
        grid=(b,),
        in_specs=[
            pl.BlockSpec(memory_space=pltpu.SMEM),
            pl.BlockSpec((1, n, d), lambda i: (i, 0, 0)),
            vec, vec, vec,
            pl.BlockSpec(wk.shape, lambda i: (0, 0)),
            pl.BlockSpec(wv.shape, lambda i: (0, 0)),
            pl.BlockSpec((n, hd), lambda i: (0, 0)),
            pl.BlockSpec((n, hd), lambda i: (0, 0)),
        ],
        out_specs=[pl.BlockSpec((1, RET_HEADS, hd, hd), lambda i: (i, 0, 0, 0))] * 2,
        out_shape=[st, st],
        compiler_params=pltpu.CompilerParams(dimension_semantics=("arbitrary",)),
        name="context",
    )(lg, ctx, gpre, csc1, csh1, wk, wv, cos, sin)


def _fill_tables(lg_ref, direction, decay_ref, xi_ref, zeta_ref):
    c = decay_ref.shape[1]
    i = lax.broadcasted_iota(jnp.int32, (c, c), 0)
    j = lax.broadcasted_iota(jnp.int32, (c, c), 1)
    col = lax.broadcasted_iota(jnp.int32, (c, 1), 0).astype(F32)
    for h in range(RET_HEADS):
        lg = lg_ref[direction, h]
        if direction == 0:
            rel = (i - j).astype(F32)
            mask = i >= j
            xi_ref[h] = jnp.exp(lg * (col + 1.0))
            zeta_ref[h] = jnp.exp(lg * (c - 1.0 - col))
        else:
            rel = (j - i).astype(F32)
            mask = j > i
            xi_ref[h] = jnp.exp(lg * (c - col))
            zeta_ref[h] = jnp.exp(lg * col)
        decay_ref[h] = jnp.where(mask, jnp.exp(lg * jnp.where(mask, rel, 0.0)), 0.0)


def _retention_block(qh, kh_f32, vh, lg, s_ref, decay_ref, xi_ref, zeta_ref, h):
    c = decay_ref.shape[1]
    kh = kh_f32.astype(BF16)
    scores = lax.dot_general(qh, kh, (((1,), (1,)), ((), ())), preferred_element_type=F32)
    s_prev = s_ref[h]
    lhs = jnp.concatenate([(scores * decay_ref[h]).astype(BF16),
                           (qh.astype(F32) * xi_ref[h]).astype(BF16)], axis=1)
    rhs = jnp.concatenate([vh, s_prev.astype(BF16)], axis=0)
    out = jnp.dot(lhs, rhs, preferred_element_type=F32)
    kz = (kh_f32 * zeta_ref[h]).T.astype(BF16)
    s_ref[h] = jnp.exp(lg * c) * s_prev + jnp.dot(kz, vh, preferred_element_type=F32)
    return out


def _mix_bwd_kernel(lg_ref, x_ref, gpre_ref, sc_ref, sh_ref, win_ref, cw_ref, cb_ref,
                    cos_ref, sin_ref, s0_ref,
                    q_ref, k_ref, v_ref, g_ref, yb_ref, conv_ref,
                    s_ref, qs_ref, ks_ref, vs_ref, decay_ref, xi_ref, zeta_ref, *, nb):
    t = x_ref.shape[1]
    hd = LANES
    cw = conv_ref.shape[2]
    rw = RET_HEADS * hd
    step = pl.program_id(0)
    cur = step % 2
    prev = 1 - cur

    @pl.when(step == 0)
    def _():
        s_ref[...] = jnp.zeros(s_ref.shape, F32)
        qs_ref[prev] = jnp.zeros(qs_ref.shape[1:], BF16)
        ks_ref[prev] = jnp.zeros(ks_ref.shape[1:], F32)
        vs_ref[prev] = jnp.zeros(vs_ref.shape[1:], BF16)
        _fill_tables(lg_ref, 1, decay_ref, xi_ref, zeta_ref)

    @pl.when(step % nb == 1)
    def _():
        s_ref[...] = s0_ref[0]

    for h in range(RET_HEADS):
        sl = slice(h * hd, (h + 1) * hd)
        yb_ref[0, :, sl] = _retention_block(
            qs_ref[prev, :, sl], ks_ref[prev, :, sl], vs_ref[prev, :, sl],
            lg_ref[1, h], s_ref, decay_ref, xi_ref, zeta_ref, h)

    hx = _rms(x_ref[0], gpre_ref[...]) * (1.0 + sc_ref[0]) + sh_ref[0]
    proj = jnp.dot(hx.astype(BF16), win_ref[...], preferred_element_type=F32)
    x_b = proj[:, 0:cw]
    x_c = proj[:, cw:2 * cw]
    x_x = proj[:, 2 * cw:3 * cw]
    off = 3 * cw
    q = proj[:, off:off + rw]
    k = proj[:, off + rw:off + 2 * rw]
    v = proj[:, off + 2 * rw:off + 3 * rw]
    g_ref[0] = proj[:, off + 3 * rw:off + 4 * rw]

    u = x_c * x_x
    col = lax.broadcasted_iota(jnp.int32, (t, cw), 0) & (GRID_W - 1)
    u_prev = jnp.where(col == 0, 0.0, pltpu.roll(u, 1, axis=0))
    u_next = jnp.where(col == GRID_W - 1, 0.0, pltpu.roll(u, t - 1, axis=0))
    y = u_prev * cw_ref[0:1, :] + u * cw_ref[1:2, :] + u_next * cw_ref[2:3, :] + cb_ref[...]
    conv_ref[0] = (x_b * y).astype(BF16)

    cos = cos_ref[...]
    sin = sin_ref[...]
    v_bf = v.astype(BF16)
    v_ref[0] = v_bf
    vs_ref[cur] = v_bf
    for h in range(RET_HEADS):
        sl = slice(h * hd, (h + 1) * hd)
        qh = q[:, sl]
        kh = k[:, sl]
        qr = (qh * cos + _rot_half(qh) * sin).astype(BF16)
        kr = (kh * cos + _rot_half(kh) * sin) * (hd ** -0.5)
        q_ref[0, :, sl] = qr
        qs_ref[cur, :, sl] = qr
        k_ref[0, :, sl] = kr.astype(BF16)
        ks_ref[cur, :, sl] = kr


def _mix_bwd(lg, x, gpre, sc1, sh1, w_in, conv_w, conv_b, cos, sin, s0_b):
    b, n, d = x.shape
    t = TOKEN_BLOCK
    nb = n // t
    steps = b * nb
    hd = LANES
    cw = conv_w.shape[1]
    rw = RET_HEADS * hd
    proj = lambda s: jnp.minimum(s, steps - 1)
    retn = lambda s: jnp.maximum(s - 1, 0)
    blk = lambda g: (g // nb, nb - 1 - g % nb, 0)
    vec = pl.BlockSpec((1, d), lambda s: (0, 0))
    bvec = pl.BlockSpec((1, 1, d), lambda s: (proj(s) // nb, 0, 0))
    tok = lambda w: pl.BlockSpec((1, t, w), lambda s: blk(proj(s)))
    rope = pl.BlockSpec((t, hd), lambda s: (nb - 1 - proj(s) % nb, 0))
    out = lambda w, dt: jax.ShapeDtypeStruct((b, n, w), dt)
    return pl.pallas_call(
        functools.partial(_mix_bwd_kernel, nb=nb),
        grid=(steps + 1,),
        in_specs=[
            pl.BlockSpec(memory_space=pltpu.SMEM),
            tok(d), vec, bvec, bvec,
            pl.BlockSpec(w_in.shape, lambda s: (0, 0)),
            pl.BlockSpec(conv_w.shape, lambda s: (0, 0)),
            pl.BlockSpec((1, cw), lambda s: (0, 0)),
            rope, rope,
            pl.BlockSpec((1, RET_HEADS, hd, hd), lambda s: (retn(s) // nb, 0, 0, 0)),
        ],
        out_specs=[tok(rw), tok(rw), tok(rw), tok(rw),
                   pl.BlockSpec((1, t, rw), lambda s: blk(retn(s))), tok(cw)],
        out_shape=[out(rw, BF16), out(rw, BF16), out(rw, BF16), out(rw, F32),
                   out(rw, F32), out(cw, BF16)],
        scratch_shapes=[pltpu.VMEM((RET_HEADS, hd, hd), F32),
                        pltpu.VMEM((2, t, rw), BF16), pltpu.VMEM((2, t, rw), F32),
                        pltpu.VMEM((2, t, rw), BF16),
                        pltpu.VMEM((RET_HEADS, t, t), F32),
                        pltpu.VMEM((RET_HEADS, t, 1), F32),
                        pltpu.VMEM((RET_HEADS, t, 1), F32)],
        compiler_params=pltpu.CompilerParams(
            dimension_semantics=("arbitrary",),
            vmem_limit_bytes=_vmem(52 * 2 ** 20)),
        name="mix_bwd",
    )(lg, x, gpre, sc1, sh1, w_in, conv_w, conv_b, cos, sin, s0_b)


def _mix_fwd_kernel(lg_ref, x_ref, q_ref, k_ref, v_ref, g_ref, yb_ref, conv_ref,
                    wout_ref, gn_ref, gpost_ref, gffn_ref, g1_ref, sc_ref, sh_ref,
                    wr_ref, br_ref, s0_ref,
                    xmid_ref, h2_ref, logit_ref, s_ref, mix_ref,
                    decay_ref, xi_ref, zeta_ref, *, nb):
    t = x_ref.shape[1]
    hd = LANES
    cw = conv_ref.shape[2]
    step = pl.program_id(0)
    cur = step % 2
    prev = 1 - cur

    @pl.when(step == 0)
    def _():
        mix_ref[prev] = jnp.zeros(mix_ref.shape[1:], BF16)
        _fill_tables(lg_ref, 0, decay_ref, xi_ref, zeta_ref)

    @pl.when(step % nb == 0)
    def _():
        s_ref[...] = s0_ref[0]

    mix_ref[cur, :, 0:cw] = conv_ref[0]
    for h in range(RET_HEADS):
        sl = slice(h * hd, (h + 1) * hd)
        y = _retention_block(q_ref[0, :, sl], k_ref[0, :, sl].astype(F32), v_ref[0, :, sl],
                             lg_ref[0, h], s_ref, decay_ref, xi_ref, zeta_ref, h)
        y = y + yb_ref[0, :, sl]
        mu = jnp.mean(y, axis=-1, keepdims=True)
        yc = y - mu
        var = jnp.mean(yc * yc, axis=-1, keepdims=True)
        yn = yc * lax.rsqrt(var + GN_EPS) * gn_ref[:, sl]
        mix_ref[cur, :, cw + h * hd:cw + (h + 1) * hd] = (
            _silu(g_ref[0, :, sl]) * yn).astype(BF16)

    mix = jnp.dot(mix_ref[prev], wout_ref[...], preferred_element_type=F32)
    x_mid = x_ref[0] + g1_ref[0] * _rms(mix, gpost_ref[...])
    xmid_ref[0] = x_mid
    h2 = _rms(x_mid, gffn_ref[...]) * (1.0 + sc_ref[0]) + sh_ref[0]
    ne = logit_ref.shape[2]
    h_hi = h2.astype(BF16)
    h_lo = (h2 - h_hi.astype(F32)).astype(BF16)
    p_hi = jnp.dot(h_hi, wr_ref[...], preferred_element_type=F32)
    p_lo = jnp.dot(h_lo, wr_ref[...], preferred_element_type=F32)
    p = p_hi + p_lo
    logit_ref[0] = p[:, 0:ne] + p[:, ne:2 * ne] + br_ref[...]
    _store_tile_major(h2_ref, (0,), h2)


def _mix_fwd(lg, x, q, k, v, g, yb, conv, w_out, gn, gpost, gffn, g1, sc2, sh2, wr, br, s0_f):
    b, n, d = x.shape
    t = TOKEN_BLOCK
    nb = n // t
    hd = LANES
    cw = conv.shape[2]
    rw = RET_HEADS * hd
    ne = br.shape[1]
    steps = b * nb
    ret = lambda s: jnp.minimum(s, steps - 1)
    outp = lambda s: jnp.maximum(s - 1, 0)
    vec = lambda w: pl.BlockSpec((1, w), lambda s: (0, 0))
    bvec = pl.BlockSpec((1, 1, d), lambda s: (outp(s) // nb, 0, 0))
    rtok = lambda w: pl.BlockSpec((1, t, w), lambda s: (ret(s) // nb, ret(s) % nb, 0))
    otok = lambda w: pl.BlockSpec((1, t, w), lambda s: (outp(s) // nb, outp(s) % nb, 0))
    return pl.pallas_call(
        functools.partial(_mix_fwd_kernel, nb=nb),
        grid=(steps + 1,),
        in_specs=[
            pl.BlockSpec(memory_space=pltpu.SMEM),
            otok(d), rtok(rw), rtok(rw), rtok(rw), rtok(rw), rtok(rw), rtok(cw),
            pl.BlockSpec(w_out.shape, lambda s: (0, 0)),
            vec(rw), vec(d), vec(d), bvec, bvec, bvec,
            pl.BlockSpec(wr.shape, lambda s: (0, 0)),
            vec(ne),
            pl.BlockSpec((1, RET_HEADS, hd, hd), lambda s: (ret(s) // nb, 0, 0, 0)),
        ],
        out_specs=[otok(d),
                   pl.BlockSpec((1, t // SUBLANES, GROUP, LANES),
                                lambda s: (outp(s) // nb, outp(s) % nb, 0, 0)),
                   otok(ne)],
        out_shape=[jax.ShapeDtypeStruct((b, n, d), F32),
                   jax.ShapeDtypeStruct((b, n // SUBLANES, GROUP, LANES), F32),
                   jax.ShapeDtypeStruct((b, n, ne), F32)],
        scratch_shapes=[pltpu.VMEM((RET_HEADS, hd, hd), F32),
                        pltpu.VMEM((2, t, cw + rw), BF16),
                        pltpu.VMEM((RET_HEADS, t, t), F32),
                        pltpu.VMEM((RET_HEADS, t, 1), F32),
                        pltpu.VMEM((RET_HEADS, t, 1), F32)],
        compiler_params=pltpu.CompilerParams(
            dimension_semantics=("arbitrary",),
            vmem_limit_bytes=_vmem(48 * 2 ** 20)),
        name="mix_fwd",
    )(lg, x, q, k, v, g, yb, conv, w_out, gn, gpost, gffn, g1, sc2, sh2, wr, br, s0_f)


def _route_kernel(logit_ref, gate_ref, idx_ref, *, cap):
    nv, ne, _ = logit_ref.shape[1:]
    grp = SUBLANES
    x = logit_ref[0]
    m = jnp.max(x, axis=1, keepdims=True)
    s = jnp.sum(jnp.exp(x - m), axis=1, keepdims=True)
    e0 = pl.multiple_of(pl.program_id(1) * grp, grp)
    xg = logit_ref[0, :, pl.ds(e0, grp), :]
    key = jnp.exp(xg - m) / s

    top = cap // LANES

    def coords(nvreg):
        shape = (nvreg, grp, LANES)
        vi = lax.broadcasted_iota(jnp.int32, shape, 0)
        li = lax.broadcasted_iota(jnp.int32, shape, 2)
        return vi, li, (vi // top) * cap + li * top + vi % top

    def better(pk, pi, k, i):
        return (pk > k) | ((pk == k) & (pi < i))

    def stage(k, i, li, pos, dist, span):
        if dist < top:
            halves = lambda a: a.reshape(a.shape[0] // (2 * dist), 2, dist, grp, LANES)
            k5, i5 = halves(k), halves(i)
            k_lo, k_hi, i_lo, i_hi = k5[:, 0], k5[:, 1], i5[:, 0], i5[:, 1]
            hi_wins = better(k_hi, i_hi, k_lo, i_lo)
            k_b, k_w = jnp.where(hi_wins, k_hi, k_lo), jnp.where(hi_wins, k_lo, k_hi)
            i_b, i_w = jnp.where(hi_wins, i_hi, i_lo), jnp.where(hi_wins, i_lo, i_hi)
            if span is not None:
                desc = (halves(pos)[:, 0] & span) == 0
                k_b, k_w = jnp.where(desc, k_b, k_w), jnp.where(desc, k_w, k_b)
                i_b, i_w = jnp.where(desc, i_b, i_w), jnp.where(desc, i_w, i_b)
            join = lambda lo, hi: jnp.concatenate([lo[:, None], hi[:, None]], axis=1).reshape(k.shape)
            return join(k_b, k_w), join(i_b, i_w)
        sh = dist // top
        low = (li & sh) == 0
        pk = jnp.where(low, pltpu.roll(k, LANES - sh, axis=2), pltpu.roll(k, sh, axis=2))
        pi = jnp.where(low, pltpu.roll(i, LANES - sh, axis=2), pltpu.roll(i, sh, axis=2))
        want_best = low if span is None else low == ((pos & span) == 0)
        take = better(pk, pi, k, i) == want_best
        return jnp.where(take, pk, k), jnp.where(take, pi, i)

    vi, li, pos = coords(nv)
    idx = vi * LANES + li
    span = 2
    while span <= cap:
        dist = span // 2
        while dist >= 1:
            key, idx = stage(key, idx, li, pos, dist, span)
            dist //= 2
        span *= 2
    nblk = nv // top
    while nblk > 1:
        k5 = key.reshape(nblk // 2, 2, top, grp, LANES)
        i5 = idx.reshape(nblk // 2, 2, top, grp, LANES)
        k0, k1, i0, i1 = k5[:, 0], k5[:, 1], i5[:, 0], i5[:, 1]
        take = better(k1, i1, k0, i0)
        nblk //= 2
        key = jnp.where(take, k1, k0).reshape(nblk * top, grp, LANES)
        idx = jnp.where(take, i1, i0).reshape(nblk * top, grp, LANES)
        vi, li, pos = coords(nblk * top)
        dist = cap // 2
        while dist >= 1:
            key, idx = stage(key, idx, li, pos, dist, None if nblk == 1 else cap)
            dist //= 2
    gate_ref[0, 0] = key[0:top]
    idx_ref[0, 0] = _row_base(idx[0:top])


def _route(logits_t, cap):
    b, nv, ne, _ = logits_t.shape
    grp = SUBLANES
    top = cap // LANES
    out_block = pl.BlockSpec((1, 1, top, grp, LANES), lambda i, j: (i, j, 0, 0, 0))
    return pl.pallas_call(
        functools.partial(_route_kernel, cap=cap),
        grid=(b, ne // grp),
        in_specs=[pl.BlockSpec((1, nv, ne, LANES), lambda i, j: (i, 0, 0, 0))],
        out_specs=[out_block, out_block],
        out_shape=[jax.ShapeDtypeStruct((b, ne // grp, top, grp, LANES), F32),
                   jax.ShapeDtypeStruct((b, ne // grp, top, grp, LANES), jnp.int32)],
        compiler_params=pltpu.CompilerParams(dimension_semantics=("arbitrary", "arbitrary")),
        name="route",
    )(logits_t)


BF16_ROWS = 2 * SUBLANES


def _dispatch_kernel(idx_ref, h_ref, xs_ref, stage_ref):
    cap = idx_ref.shape[2]

    def body(r, carry):
        for u in range(BF16_ROWS):
            src = idx_ref[0, 0, r * BF16_ROWS + u]
            dst = (u // SUBLANES) * GROUP + u % SUBLANES
            stage_ref[pl.ds(dst, ROW_TILE, stride=SUBLANES), :] = (
                h_ref[0, pl.ds(src, ROW_TILE, stride=SUBLANES), :])
        row0 = pl.multiple_of(r * BF16_ROWS, BF16_ROWS)
        for c in range(ROW_TILE):
            lo = stage_ref[c * SUBLANES:(c + 1) * SUBLANES, :]
            hi = stage_ref[GROUP + c * SUBLANES:GROUP + (c + 1) * SUBLANES, :]
            xs_ref[0, 0, pl.ds(row0, BF16_ROWS), c * LANES:(c + 1) * LANES] = (
                jnp.concatenate([lo, hi], axis=0).astype(BF16))
        return carry

    lax.fori_loop(0, cap // BF16_ROWS, body, 0)


def _dispatch(idx, h_rows, n_exp, cap):
    b = h_rows.shape[0]
    rows = h_rows.shape[1]
    d = ROW_TILE * LANES
    return pl.pallas_call(
        _dispatch_kernel,
        grid=(b, n_exp),
        in_specs=[
            pl.BlockSpec((1, 1, cap), lambda i, e: (i * n_exp + e, 0, 0),
                         memory_space=pltpu.SMEM),
            pl.BlockSpec((1, rows, LANES), lambda i, e: (i, 0, 0),
                         pipeline_mode=pl.Buffered(1)),
        ],
        out_specs=pl.BlockSpec((1, 1, cap, d), lambda i, e: (i, e, 0, 0)),
        out_shape=jax.ShapeDtypeStruct((b, n_exp, cap, d), BF16),
        scratch_shapes=[pltpu.VMEM((2 * GROUP, LANES), F32)],
        compiler_params=pltpu.CompilerParams(
            dimension_semantics=("arbitrary", "arbitrary"),
            vmem_limit_bytes=_vmem(rows * LANES * 4 + 4 * cap * d * 2)),
        name="dispatch",
    )(idx, h_rows)


def _ffn_kernel(xs_ref, wg_ref, wu_ref, wd_ref, gate_ref, y_ref):
    nb = xs_ref.shape[0]

    @pl.when(pl.program_id(1) == 0)
    def _():
        y_ref[...] = jnp.zeros(y_ref.shape, F32)

    wg = wg_ref[0].astype(BF16)
    wu = wu_ref[0].astype(BF16)
    wd = wd_ref[0].astype(BF16)
    for b in range(nb):
        xm = xs_ref[b, 0]
        a = jnp.dot(xm, wg, preferred_element_type=F32)
        u = jnp.dot(xm, wu, preferred_element_type=F32)
        hmid = (_silu(a) * u * gate_ref[b, 0]).astype(BF16)
        contrib = jnp.dot(hmid, wd, preferred_element_type=F32)
        rows = contrib.shape[0]
        for c in range(ROW_TILE):
            y_ref[b, 0, :, c * SUBLANES:(c + 1) * SUBLANES, :] += (
                contrib[:, c * LANES:(c + 1) * LANES].reshape(rows // SUBLANES, SUBLANES, LANES))


def _ffn(xs, w_gate, w_up, w_down, gate):
    b, n_exp, cap, d = xs.shape
    dff = w_gate.shape[2]
    tf = FFN_TILE
    groups = cap // SUBLANES
    return pl.pallas_call(
        _ffn_kernel,
        grid=(n_exp, dff // tf),
        in_specs=[
            pl.BlockSpec((b, 1, cap, d), lambda e, f: (0, e, 0, 0)),
            pl.BlockSpec((1, d, tf), lambda e, f: (e, 0, f)),
            pl.BlockSpec((1, d, tf), lambda e, f: (e, 0, f)),
            pl.BlockSpec((1, tf, d), lambda e, f: (e, f, 0)),
            pl.BlockSpec((b, 1, cap, 1), lambda e, f: (0, e, 0, 0)),
        ],
        out_specs=pl.BlockSpec((b, 1, groups, GROUP, LANES), lambda e, f: (0, e, 0, 0, 0)),
        out_shape=jax.ShapeDtypeStruct((b, n_exp, groups, GROUP, LANES), F32),
        compiler_params=pltpu.CompilerParams(
            dimension_semantics=("arbitrary", "arbitrary"),
            vmem_limit_bytes=_vmem(56 * 2 ** 20)),
        name="ffn",
    )(xs, w_gate, w_up, w_down, gate)


def _combine_kernel(idx_ref, y_ref, xmid_ref, gpost_ref, g2_ref, o_ref, acc_ref, *, n_exp):
    cap = idx_ref.shape[2]
    t = xmid_ref.shape[1]
    step = pl.program_id(1)

    @pl.when(step == 0)
    def _():
        zrows = GROUP

        def zero(r, carry):
            acc_ref[pl.ds(pl.multiple_of(r * zrows, zrows), zrows), :] = jnp.zeros(
                (zrows, LANES), F32)
            return carry

        lax.fori_loop(0, acc_ref.shape[0] // zrows, zero, 0)

    @pl.when(step < n_exp)
    def _():
        def body(r, carry):
            offs, vals = [], []
            for u in range(GATHER_UNROLL):
                dst = idx_ref[0, 0, r * GATHER_UNROLL + u]
                offs.append(dst)
                vals.append(acc_ref[pl.ds(dst, ROW_TILE, stride=SUBLANES), :]
                            + y_ref[0, 0, pl.ds(r * GROUP + u, ROW_TILE, stride=SUBLANES), :])
            for dst, val in zip(offs, vals):
                acc_ref[pl.ds(dst, ROW_TILE, stride=SUBLANES), :] = val
            return carry

        lax.fori_loop(0, cap // GATHER_UNROLL, body, 0)

    @pl.when(step >= n_exp)
    def _():
        rows = t * ROW_TILE
        start = pl.multiple_of((step - n_exp) * rows, rows)
        blk = acc_ref[pl.ds(start, rows), :].reshape(t // SUBLANES, GROUP, LANES)
        ffn = jnp.concatenate(
            [blk[:, c * SUBLANES:(c + 1) * SUBLANES, :].reshape(t, LANES)
             for c in range(ROW_TILE)], axis=-1)
        o_ref[0] = xmid_ref[0] + g2_ref[0] * _rms(ffn, gpost_ref[...])


def _combine(idx, y, x_mid, gpost, g2):
    b, n_exp, rows, _ = y.shape
    n, d = x_mid.shape[1:]
    cap = rows // ROW_TILE
    t = TOKEN_BLOCK
    last = n_exp - 1
    tok_block = pl.BlockSpec((1, t, d), lambda i, s: (i, jnp.maximum(s - n_exp, 0), 0))
    return pl.pallas_call(
        functools.partial(_combine_kernel, n_exp=n_exp),
        grid=(b, n_exp + n // t),
        in_specs=[
            pl.BlockSpec((1, 1, cap), lambda i, s: (i * n_exp + jnp.minimum(s, last), 0, 0),
                         memory_space=pltpu.SMEM),
            pl.BlockSpec((1, 1, rows, LANES), lambda i, s: (i, jnp.minimum(s, last), 0, 0)),
            tok_block,
            pl.BlockSpec((1, d), lambda i, s: (0, 0)),
            pl.BlockSpec((1, 1, d), lambda i, s: (i, 0, 0)),
        ],
        out_specs=tok_block,
        out_shape=jax.ShapeDtypeStruct((b, n, d), F32),
        scratch_shapes=[pltpu.VMEM((n * ROW_TILE, LANES), F32)],
        compiler_params=pltpu.CompilerParams(
            dimension_semantics=("arbitrary", "arbitrary"),
            vmem_limit_bytes=_vmem(n * d * 4 + 2 * rows * LANES * 4 + 8 * t * d * 4)),
        name="combine",
    )(idx, y, x_mid, gpost, g2)


def _rope_tables(n_pos):
    half = LANES // 2
    inv = 1.0 / (ROPE_BASE ** (jnp.arange(half, dtype=F32) / half))
    ang_a = (jnp.arange(n_pos // LANES, dtype=F32) * LANES)[:, None] * inv[None, :]
    ang_b = jnp.arange(LANES, dtype=F32)[:, None] * inv[None, :]
    ca, sa = jnp.cos(ang_a)[:, None, :], jnp.sin(ang_a)[:, None, :]
    cb, sb = jnp.cos(ang_b)[None, :, :], jnp.sin(ang_b)[None, :, :]
    cos = (ca * cb - sa * sb).reshape(n_pos, half)
    sin = (sa * cb + ca * sb).reshape(n_pos, half)
    return jnp.concatenate([cos, cos], axis=-1), jnp.concatenate([-sin, sin], axis=-1)


def kernel(x, c, ctx, c_ctx, w_ada, b_ada, pre_mix_g, post_mix_g, pre_ffn_g, post_ffn_g,
           w_in, conv_w, conv_b, ret_decay_logit, ret_norm_g, w_out,
           w_router, b_router, w_gate, w_up, w_down):
    bsz, n, d = x.shape
    ctx_len = ctx.shape[1]
    depth = w_ada.shape[0]
    n_exp = w_router.shape[2]
    cw = conv_w.shape[2]
    rw = RET_HEADS * LANES
    cap = max(1, EC_CAPACITY_FACTOR * n // n_exp)
    assert depth == 1, "only the single-layer block is implemented"
    assert d == ROW_TILE * LANES and n % TOKEN_BLOCK == 0 and TOKEN_BLOCK % LANES == 0
    assert n % GRID_W == 0 and TOKEN_BLOCK % GRID_W == 0 and ctx_len % SUBLANES == 0
    assert n_exp % SUBLANES == 0 and cap % LANES == 0 and (n // LANES) & (n // LANES - 1) == 0
    assert w_in.shape[2] == 3 * cw + 4 * rw

    assert (ctx_len + n) % LANES == 0
    cos_all, sin_all = _rope_tables(ctx_len + n)
    cos_c, sin_c = cos_all[:ctx_len], sin_all[:ctx_len]
    cos_x, sin_x = cos_all[ctx_len:], sin_all[ctx_len:]
    i = 0

    pad = -(bsz + 1) % SUBLANES
    cond = jnp.concatenate([c, c_ctx[None, :], jnp.zeros((pad, d), F32)], axis=0)
    mod = _modulation(cond, w_ada[i], b_ada[i])
    sh1, sc1, g1, sh2, sc2, g2 = [mod[:bsz, j * d:(j + 1) * d].reshape(bsz, 1, d) for j in range(6)]
    csh1 = mod[bsz:bsz + 1, 0:d]
    csc1 = mod[bsz:bsz + 1, d:2 * d]

    lg = jax.nn.log_sigmoid(ret_decay_logit[i].astype(F32))
    w_in_b = w_in[i].astype(BF16)
    w_out_b = w_out[i].astype(BF16)
    k_off = 3 * cw + rw
    row = lambda a: a.reshape(1, -1)

    s_f, s_b = _context_states(lg, ctx, row(pre_mix_g[i]), csc1, csh1,
                               w_in_b[:, k_off:k_off + rw], w_in_b[:, k_off + rw:k_off + 2 * rw],
                               cos_c, sin_c)

    q, k, v, g, yb, conv = _mix_bwd(lg, x, row(pre_mix_g[i]), sc1, sh1, w_in_b,
                                    conv_w[i], row(conv_b[i]), cos_x, sin_x, s_b)

    wr = w_router[i].astype(F32)
    wr_hi = wr.astype(BF16)
    wr_lo = (wr - wr_hi.astype(F32)).astype(BF16)
    wr_cat = jnp.concatenate([wr_hi, wr_lo], axis=1)
    br = b_router[i].astype(F32).reshape(1, n_exp)
    x_mid, h_tiles, logits = _mix_fwd(lg, x, q, k, v, g, yb, conv, w_out_b, row(ret_norm_g[i]),
                                      row(post_mix_g[i]), row(pre_ffn_g[i]), g1, sc2, sh2,
                                      wr_cat, br, s_f)

    logits_t = logits.reshape(bsz, n // LANES, LANES, n_exp).transpose(0, 1, 3, 2)
    gate5, idx5 = _route(logits_t, cap)
    gate = gate5.transpose(0, 1, 3, 2, 4).reshape(bsz, n_exp, cap)
    idx = idx5.transpose(0, 1, 3, 2, 4).reshape(bsz * n_exp, 1, cap)

    xs = _dispatch(idx, h_tiles.reshape(bsz, n * ROW_TILE, LANES), n_exp, cap)
    y = _ffn(xs, w_gate[i], w_up[i], w_down[i], gate.reshape(bsz, n_exp, cap, 1))
    return _combine(idx, y.reshape(bsz, n_exp, cap * ROW_TILE, LANES), x_mid,
                    row(post_ffn_g[i]), g2)
```

```python
import functools

import jax
import jax.numpy as jnp
from jax import lax
from jax.experimental import pallas as pl
from jax.experimental.pallas import tpu as pltpu

F32 = jnp.float32
BF16 = jnp.bfloat16

GRID_W = 64
RET_HEADS = 4
ROPE_BASE = 10000.0
EC_CAPACITY_FACTOR = 2
RMS_EPS = 1e-6
GN_EPS = 1e-6

LANES = 128
SUBLANES = 8
ROW_TILE = SUBLANES
GROUP = SUBLANES * ROW_TILE
VMEM_CAP = 60000 * 1024

TOKEN_BLOCK = 512
FFN_TILE = 512
SCATTER_UNROLL = 2 * SUBLANES


def _vmem(nbytes):
    return min(int(nbytes), VMEM_CAP)


def _rms(x, g):
    return x * lax.rsqrt(jnp.mean(x * x, axis=-1, keepdims=True) + RMS_EPS) * g


def _silu(x):
    return x * (1.0 / (1.0 + jnp.exp(-x)))


def _rot_half(t):
    return pltpu.roll(t, LANES // 2, axis=1)


def _store_tile_major(ref, lead, val):
    rows = val.shape[0]
    for c in range(ROW_TILE):
        ref[lead + (slice(None), slice(c * SUBLANES, (c + 1) * SUBLANES), slice(None))] = (
            val[:, c * LANES:(c + 1) * LANES].reshape(rows // SUBLANES, SUBLANES, LANES))


def _row_base(r):
    return (r >> 3) * GROUP + (r & (SUBLANES - 1))


def _modulation_kernel(cond_ref, w_ref, b_ref, o_ref):
    s = _silu(cond_ref[...])
    o_ref[...] = jnp.dot(s, w_ref[...], preferred_element_type=F32,
                         precision=lax.Precision.HIGHEST) + b_ref[...]


def _modulation(cond, w_ada, b_ada):
    rows, d = cond.shape
    cols = w_ada.shape[1]
    tile = cols // 4
    return pl.pallas_call(
        _modulation_kernel,
        grid=(cols // tile,),
        in_specs=[
            pl.BlockSpec((rows, d), lambda j: (0, 0)),
            pl.BlockSpec((d, tile), lambda j: (0, j)),
            pl.BlockSpec((1, tile), lambda j: (0, j)),
        ],
        out_specs=pl.BlockSpec((rows, tile), lambda j: (0, j)),
        out_shape=jax.ShapeDtypeStruct((rows, cols), F32),
        compiler_params=pltpu.CompilerParams(
            dimension_semantics=("arbitrary",),
            vmem_limit_bytes=_vmem(3 * d * tile * 4)),
        name="modulation",
    )(cond, w_ada, b_ada.reshape(1, cols))


def _context_kernel(lg_ref, ctx_ref, gpre_ref, sc_ref, sh_ref, wk_ref, wv_ref,
                    cos_ref, sin_ref, sf_ref, sb_ref):
    n = ctx_ref.shape[1]
    hd = LANES
    hc = _rms(ctx_ref[0], gpre_ref[...]) * (1.0 + sc_ref[...]) + sh_ref[...]
    hc = hc.astype(BF16)
    k = jnp.dot(hc, wk_ref[...], preferred_element_type=F32)
    v = jnp.dot(hc, wv_ref[...], preferred_element_type=F32)
    cos = cos_ref[...]
    sin = sin_ref[...]
    pos = lax.broadcasted_iota(jnp.int32, (n, 1), 0).astype(F32)
    for h in range(RET_HEADS):
        kh = k[:, h * hd:(h + 1) * hd]
        kr = (kh * cos + _rot_half(kh) * sin) * (hd ** -0.5)
        vh = v[:, h * hd:(h + 1) * hd].astype(BF16)
        wf = jnp.exp(lg_ref[0, h] * (n - 1.0 - pos))
        wb = jnp.exp(lg_ref[1, h] * pos)
        kf = (kr * wf).T.astype(BF16)
        kb = (kr * wb).T.astype(BF16)
        sf_ref[0, h] = jnp.dot(kf, vh, preferred_element_type=F32)
        sb_ref[0, h] = jnp.dot(kb, vh, preferred_element_type=F32)


def _context_states(lg, ctx, gpre, csc1, csh1, wk, wv, cos, sin):
    b, n, d = ctx.shape
    hd = LANES
    st = jax.ShapeDtypeStruct((b, RET_HEADS, hd, hd), F32)
    vec = pl.BlockSpec((1, d), lambda i: (0, 0))
    return pl.pallas_call(
        _context_kernel,
        grid=(b,),
        in_specs=[
            pl.BlockSpec(memory_space=pltpu.SMEM),
            pl.BlockSpec((1, n, d), lambda i: (i, 0, 0)),
            vec, vec, vec,
            pl.BlockSpec(wk.shape, lambda i: (0, 0)),
            pl.BlockSpec(wv.shape, lambda i: (0, 0)),
            pl.BlockSpec((n, hd), lambda i: (0, 0)),
            pl.BlockSpec((n, hd), lambda i: (0, 0)),
        ],
        out_specs=[pl.BlockSpec((1, RET_HEADS, hd, hd), lambda i: (i, 0, 0, 0))] * 2,
        out_shape=[st, st],
        compiler_params=pltpu.CompilerParams(dimension_semantics=("arbitrary",)),
        name="context",
    )(lg, ctx, gpre, csc1, csh1, wk, wv, cos, sin)


def _fill_tables(lg_ref, direction, decay_ref, xi_ref, zeta_ref):
    c = decay_ref.shape[1]
    i = lax.broadcasted_iota(jnp.int32, (c, c), 0)
    j = lax.broadcasted_iota(jnp.int32, (c, c), 1)
    col = lax.broadcasted_iota(jnp.int32, (c, 1), 0).astype(F32)
    for h in range(RET_HEADS):
        lg = lg_ref[direction, h]
        if direction == 0:
            rel = (i - j).astype(F32)
            mask = i >= j
            xi_ref[h] = jnp.exp(lg * (col + 1.0))
            zeta_ref[h] = jnp.exp(lg * (c - 1.0 - col))
        else:
            rel = (j - i).astype(F32)
            mask = j > i
            xi_ref[h] = jnp.exp(lg * (c - col))
            zeta_ref[h] = jnp.exp(lg * col)
        decay_ref[h] = jnp.where(mask, jnp.exp(lg * jnp.where(mask, rel, 0.0)), 0.0)


def _retention_block(qh, kh_f32, vh, lg, s_ref, decay_ref, xi_ref, zeta_ref, h):
    c = decay_ref.shape[1]
    kh = kh_f32.astype(BF16)
    scores = lax.dot_general(qh, kh, (((1,), (1,)), ((), ())), preferred_element_type=F32)
    s_prev = s_ref[h]
    lhs = jnp.concatenate([(scores * decay_ref[h]).astype(BF16),
                           (qh.astype(F32) * xi_ref[h]).astype(BF16)], axis=1)
    rhs = jnp.concatenate([vh, s_prev.astype(BF16)], axis=0)
    out = jnp.dot(lhs, rhs, preferred_element_type=F32)
    kz = (kh_f32 * zeta_ref[h]).T.astype(BF16)
    s_ref[h] = jnp.exp(lg * c) * s_prev + jnp.dot(kz, vh, preferred_element_type=F32)
    return out


def _mix_bwd_kernel(lg_ref, x_ref, gpre_ref, sc_ref, sh_ref, win_ref, cw_ref, cb_ref,
                    cos_ref, sin_ref, s0_ref,
                    q_ref, k_ref, v_ref, g_ref, yb_ref, conv_ref,
                    s_ref, qs_ref, ks_ref, vs_ref, decay_ref, xi_ref, zeta_ref, *, nb):
    t = x_ref.shape[1]
    hd = LANES
    cw = conv_ref.shape[2]
    rw = RET_HEADS * hd
    step = pl.program_id(0)
    cur = step % 2
    prev = 1 - cur

    @pl.when(step == 0)
    def _():
        s_ref[...] = jnp.zeros(s_ref.shape, F32)
        qs_ref[prev] = jnp.zeros(qs_ref.shape[1:], BF16)
        ks_ref[prev] = jnp.zeros(ks_ref.shape[1:], F32)
        vs_ref[prev] = jnp.zeros(vs_ref.shape[1:], BF16)
        _fill_tables(lg_ref, 1, decay_ref, xi_ref, zeta_ref)

    @pl.when(step % nb == 1)
    def _():
        s_ref[...] = s0_ref[0]

    for h in range(RET_HEADS):
        sl = slice(h * hd, (h + 1) * hd)
        yb_ref[0, :, sl] = _retention_block(
            qs_ref[prev, :, sl], ks_ref[prev, :, sl], vs_ref[prev, :, sl],
            lg_ref[1, h], s_ref, decay_ref, xi_ref, zeta_ref, h)

    hx = _rms(x_ref[0], gpre_ref[...]) * (1.0 + sc_ref[0]) + sh_ref[0]
    proj = jnp.dot(hx.astype(BF16), win_ref[...], preferred_element_type=F32)
    x_b = proj[:, 0:cw]
    x_c = proj[:, cw:2 * cw]
    x_x = proj[:, 2 * cw:3 * cw]
    off = 3 * cw
    q = proj[:, off:off + rw]
    k = proj[:, off + rw:off + 2 * rw]
    v = proj[:, off + 2 * rw:off + 3 * rw]
    g_ref[0] = proj[:, off + 3 * rw:off + 4 * rw]

    u = x_c * x_x
    col = lax.broadcasted_iota(jnp.int32, (t, cw), 0) & (GRID_W - 1)
    u_prev = jnp.where(col == 0, 0.0, pltpu.roll(u, 1, axis=0))
    u_next = jnp.where(col == GRID_W - 1, 0.0, pltpu.roll(u, t - 1, axis=0))
    y = u_prev * cw_ref[0:1, :] + u * cw_ref[1:2, :] + u_next * cw_ref[2:3, :] + cb_ref[...]
    conv_ref[0] = (x_b * y).astype(BF16)

    cos = cos_ref[...]
    sin = sin_ref[...]
    v_bf = v.astype(BF16)
    v_ref[0] = v_bf
    vs_ref[cur] = v_bf
    for h in range(RET_HEADS):
        sl = slice(h * hd, (h + 1) * hd)
        qh = q[:, sl]
        kh = k[:, sl]
        qr = (qh * cos + _rot_half(qh) * sin).astype(BF16)
        kr = (kh * cos + _rot_half(kh) * sin) * (hd ** -0.5)
        q_ref[0, :, sl] = qr
        qs_ref[cur, :, sl] = qr
        k_ref[0, :, sl] = kr.astype(BF16)
        ks_ref[cur, :, sl] = kr


def _mix_bwd(lg, x, gpre, sc1, sh1, w_in, conv_w, conv_b, cos, sin, s0_b):
    b, n, d = x.shape
    t = TOKEN_BLOCK
    nb = n // t
    steps = b * nb
    hd = LANES
    cw = conv_w.shape[1]
    rw = RET_HEADS * hd
    proj = lambda s: jnp.minimum(s, steps - 1)
    retn = lambda s: jnp.maximum(s - 1, 0)
    blk = lambda g: (g // nb, nb - 1 - g % nb, 0)
    vec = pl.BlockSpec((1, d), lambda s: (0, 0))
    bvec = pl.BlockSpec((1, 1, d), lambda s: (proj(s) // nb, 0, 0))
    tok = lambda w: pl.BlockSpec((1, t, w), lambda s: blk(proj(s)))
    rope = pl.BlockSpec((t, hd), lambda s: (nb - 1 - proj(s) % nb, 0))
    out = lambda w, dt: jax.ShapeDtypeStruct((b, n, w), dt)
    return pl.pallas_call(
        functools.partial(_mix_bwd_kernel, nb=nb),
        grid=(steps + 1,),
        in_specs=[
            pl.BlockSpec(memory_space=pltpu.SMEM),
            tok(d), vec, bvec, bvec,
            pl.BlockSpec(w_in.shape, lambda s: (0, 0)),
            pl.BlockSpec(conv_w.shape, lambda s: (0, 0)),
            pl.BlockSpec((1, cw), lambda s: (0, 0)),
            rope, rope,
            pl.BlockSpec((1, RET_HEADS, hd, hd), lambda s: (retn(s) // nb, 0, 0, 0)),
        ],
        out_specs=[tok(rw), tok(rw), tok(rw), tok(rw),
                   pl.BlockSpec((1, t, rw), lambda s: blk(retn(s))), tok(cw)],
        out_shape=[out(rw, BF16), out(rw, BF16), out(rw, BF16), out(rw, F32),
                   out(rw, F32), out(cw, BF16)],
        scratch_shapes=[pltpu.VMEM((RET_HEADS, hd, hd), F32),
                        pltpu.VMEM((2, t, rw), BF16), pltpu.VMEM((2, t, rw), F32),
                        pltpu.VMEM((2, t, rw), BF16),
                        pltpu.VMEM((RET_HEADS, t, t), F32),
                        pltpu.VMEM((RET_HEADS, t, 1), F32),
                        pltpu.VMEM((RET_HEADS, t, 1), F32)],
        compiler_params=pltpu.CompilerParams(
            dimension_semantics=("arbitrary",),
            vmem_limit_bytes=_vmem(52 * 2 ** 20)),
        name="mix_bwd",
    )(lg, x, gpre, sc1, sh1, w_in, conv_w, conv_b, cos, sin, s0_b)


def _mix_fwd_kernel(lg_ref, x_ref, q_ref, k_ref, v_ref, g_ref, yb_ref, conv_ref,
                    wout_ref, gn_ref, gpost_ref, gffn_ref, g1_ref, sc_ref, sh_ref,
                    wr_ref, br_ref, s0_ref,
                    xmid_ref, h2_ref, logit_ref, s_ref, mix_ref, prev_ref,
                    decay_ref, xi_ref, zeta_ref, *, nb):
    hd = LANES
    cw = conv_ref.shape[2]
    step = pl.program_id(0)

    @pl.when(step == 0)
    def _():
        prev_ref[...] = jnp.zeros(prev_ref.shape, BF16)
        _fill_tables(lg_ref, 0, decay_ref, xi_ref, zeta_ref)

    @pl.when(step % nb == 0)
    def _():
        s_ref[...] = s0_ref[0]

    t = x_ref.shape[1]
    ne = logit_ref.shape[2]
    mix = jnp.dot(prev_ref[...], wout_ref[...], preferred_element_type=F32)

    def finish_rows(r0, r1):
        x_mid = x_ref[0, r0:r1] + g1_ref[0] * _rms(mix[r0:r1], gpost_ref[...])
        xmid_ref[0, r0:r1] = x_mid
        h2 = _rms(x_mid, gffn_ref[...]) * (1.0 + sc_ref[0]) + sh_ref[0]
        h_hi = h2.astype(BF16)
        h_lo = (h2 - h_hi.astype(F32)).astype(BF16)
        p = (jnp.dot(h_hi, wr_ref[...], preferred_element_type=F32)
             + jnp.dot(h_lo, wr_ref[...], preferred_element_type=F32))
        logit_ref[0, r0:r1] = p[:, 0:ne] + p[:, ne:2 * ne] + br_ref[...]
        for c in range(ROW_TILE):
            h2_ref[0, r0 // SUBLANES:r1 // SUBLANES, c * SUBLANES:(c + 1) * SUBLANES, :] = (
                h2[:, c * LANES:(c + 1) * LANES].reshape((r1 - r0) // SUBLANES, SUBLANES, LANES))

    mix_ref[:, 0:cw] = conv_ref[0]
    rows = t // RET_HEADS
    for h in range(RET_HEADS):
        sl = slice(h * hd, (h + 1) * hd)
        y = _retention_block(q_ref[0, :, sl], k_ref[0, :, sl].astype(F32), v_ref[0, :, sl],
                             lg_ref[0, h], s_ref, decay_ref, xi_ref, zeta_ref, h)
        finish_rows(h * rows, (h + 1) * rows)
        y = y + yb_ref[0, :, sl]
        mu = jnp.mean(y, axis=-1, keepdims=True)
        yc = y - mu
        var = jnp.mean(yc * yc, axis=-1, keepdims=True)
        yn = yc * lax.rsqrt(var + GN_EPS) * gn_ref[:, sl]
        mix_ref[:, cw + h * hd:cw + (h + 1) * hd] = (
            _silu(g_ref[0, :, sl]) * yn).astype(BF16)
    prev_ref[...] = mix_ref[...]


def _mix_fwd(lg, x, q, k, v, g, yb, conv, w_out, gn, gpost, gffn, g1, sc2, sh2, wr, br, s0_f):
    b, n, d = x.shape
    t = TOKEN_BLOCK
    nb = n // t
    hd = LANES
    cw = conv.shape[2]
    rw = RET_HEADS * hd
    ne = br.shape[1]
    steps = b * nb
    ret = lambda s: jnp.minimum(s, steps - 1)
    outp = lambda s: jnp.maximum(s - 1, 0)
    vec = lambda w: pl.BlockSpec((1, w), lambda s: (0, 0))
    bvec = pl.BlockSpec((1, 1, d), lambda s: (outp(s) // nb, 0, 0))
    rtok = lambda w: pl.BlockSpec((1, t, w), lambda s: (ret(s) // nb, ret(s) % nb, 0))
    otok = lambda w: pl.BlockSpec((1, t, w), lambda s: (outp(s) // nb, outp(s) % nb, 0))
    return pl.pallas_call(
        functools.partial(_mix_fwd_kernel, nb=nb),
        grid=(steps + 1,),
        in_specs=[
            pl.BlockSpec(memory_space=pltpu.SMEM),
            otok(d), rtok(rw), rtok(rw), rtok(rw), rtok(rw), rtok(rw), rtok(cw),
            pl.BlockSpec(w_out.shape, lambda s: (0, 0)),
            vec(rw), vec(d), vec(d), bvec, bvec, bvec,
            pl.BlockSpec(wr.shape, lambda s: (0, 0)),
            vec(ne),
            pl.BlockSpec((1, RET_HEADS, hd, hd), lambda s: (ret(s) // nb, 0, 0, 0)),
        ],
        out_specs=[otok(d),
                   pl.BlockSpec((1, t // SUBLANES, GROUP, LANES),
                                lambda s: (outp(s) // nb, outp(s) % nb, 0, 0)),
                   otok(ne)],
        out_shape=[jax.ShapeDtypeStruct((b, n, d), F32),
                   jax.ShapeDtypeStruct((b, n // SUBLANES, GROUP, LANES), F32),
                   jax.ShapeDtypeStruct((b, n, ne), F32)],
        scratch_shapes=[pltpu.VMEM((RET_HEADS, hd, hd), F32),
                        pltpu.VMEM((t, cw + rw), BF16),
                        pltpu.VMEM((t, cw + rw), BF16),
                        pltpu.VMEM((RET_HEADS, t, t), F32),
                        pltpu.VMEM((RET_HEADS, t, 1), F32),
                        pltpu.VMEM((RET_HEADS, t, 1), F32)],
        compiler_params=pltpu.CompilerParams(
            dimension_semantics=("arbitrary",),
            vmem_limit_bytes=_vmem(48 * 2 ** 20)),
        name="mix_fwd",
    )(lg, x, q, k, v, g, yb, conv, w_out, gn, gpost, gffn, g1, sc2, sh2, wr, br, s0_f)


def _route_kernel(logit_ref, gate_ref, idx_ref, *, cap):
    nv, ne, _ = logit_ref.shape[1:]
    grp = SUBLANES
    x = logit_ref[0]
    m = jnp.max(x, axis=1, keepdims=True)
    s = jnp.sum(jnp.exp(x - m), axis=1, keepdims=True)
    e0 = pl.multiple_of(pl.program_id(1) * grp, grp)
    xg = logit_ref[0, :, pl.ds(e0, grp), :]
    key = jnp.exp(xg - m) / s

    top = cap // LANES

    def coords(nvreg):
        shape = (nvreg, grp, LANES)
        vi = lax.broadcasted_iota(jnp.int32, shape, 0)
        li = lax.broadcasted_iota(jnp.int32, shape, 2)
        return vi, li, (vi // top) * cap + li * top + vi % top

    def better(pk, pi, k, i):
        return (pk > k) | ((pk == k) & (pi < i))

    def stage(k, i, li, pos, dist, span):
        if dist < top:
            halves = lambda a: a.reshape(a.shape[0] // (2 * dist), 2, dist, grp, LANES)
            k5, i5 = halves(k), halves(i)
            k_lo, k_hi, i_lo, i_hi = k5[:, 0], k5[:, 1], i5[:, 0], i5[:, 1]
            hi_wins = better(k_hi, i_hi, k_lo, i_lo)
            k_b, k_w = jnp.where(hi_wins, k_hi, k_lo), jnp.where(hi_wins, k_lo, k_hi)
            i_b, i_w = jnp.where(hi_wins, i_hi, i_lo), jnp.where(hi_wins, i_lo, i_hi)
            if span is not None:
                desc = (halves(pos)[:, 0] & span) == 0
                k_b, k_w = jnp.where(desc, k_b, k_w), jnp.where(desc, k_w, k_b)
                i_b, i_w = jnp.where(desc, i_b, i_w), jnp.where(desc, i_w, i_b)
            join = lambda lo, hi: jnp.concatenate([lo[:, None], hi[:, None]], axis=1).reshape(k.shape)
            return join(k_b, k_w), join(i_b, i_w)
        sh = dist // top
        low = (li & sh) == 0
        pk = jnp.where(low, pltpu.roll(k, LANES - sh, axis=2), pltpu.roll(k, sh, axis=2))
        pi = jnp.where(low, pltpu.roll(i, LANES - sh, axis=2), pltpu.roll(i, sh, axis=2))
        want_best = low if span is None else low == ((pos & span) == 0)
        take = better(pk, pi, k, i) == want_best
        return jnp.where(take, pk, k), jnp.where(take, pi, i)

    vi, li, pos = coords(nv)
    idx = vi * LANES + li
    span = 2
    while span <= cap:
        dist = span // 2
        while dist >= 1:
            key, idx = stage(key, idx, li, pos, dist, span)
            dist //= 2
        span *= 2
    nblk = nv // top
    while nblk > 1:
        k5 = key.reshape(nblk // 2, 2, top, grp, LANES)
        i5 = idx.reshape(nblk // 2, 2, top, grp, LANES)
        k0, k1, i0, i1 = k5[:, 0], k5[:, 1], i5[:, 0], i5[:, 1]
        take = better(k1, i1, k0, i0)
        nblk //= 2
        key = jnp.where(take, k1, k0).reshape(nblk * top, grp, LANES)
        idx = jnp.where(take, i1, i0).reshape(nblk * top, grp, LANES)
        vi, li, pos = coords(nblk * top)
        dist = cap // 2
        while dist >= 1:
            key, idx = stage(key, idx, li, pos, dist, None if nblk == 1 else cap)
            dist //= 2
    gate_ref[0, 0] = key[0:top]
    idx_ref[0, 0] = _row_base(idx[0:top])


def _route(logits_t, cap):
    b, nv, ne, _ = logits_t.shape
    grp = SUBLANES
    top = cap // LANES
    out_block = pl.BlockSpec((1, 1, top, grp, LANES), lambda i, j: (i, j, 0, 0, 0))
    return pl.pallas_call(
        functools.partial(_route_kernel, cap=cap),
        grid=(b, ne // grp),
        in_specs=[pl.BlockSpec((1, nv, ne, LANES), lambda i, j: (i, 0, 0, 0))],
        out_specs=[out_block, out_block],
        out_shape=[jax.ShapeDtypeStruct((b, ne // grp, top, grp, LANES), F32),
                   jax.ShapeDtypeStruct((b, ne // grp, top, grp, LANES), jnp.int32)],
        compiler_params=pltpu.CompilerParams(dimension_semantics=("arbitrary", "arbitrary")),
        name="route",
    )(logits_t)


BF16_ROWS = 2 * SUBLANES


def _dispatch_kernel(idx_ref, h_ref, xs_ref, stage_ref):
    cap = idx_ref.shape[2]

    def body(r, carry):
        for u in range(BF16_ROWS):
            src = idx_ref[0, 0, r * BF16_ROWS + u]
            dst = (u // SUBLANES) * GROUP + u % SUBLANES
            stage_ref[pl.ds(dst, ROW_TILE, stride=SUBLANES), :] = (
                h_ref[0, pl.ds(src, ROW_TILE, stride=SUBLANES), :])
        row0 = pl.multiple_of(r * BF16_ROWS, BF16_ROWS)
        for c in range(ROW_TILE):
            lo = stage_ref[c * SUBLANES:(c + 1) * SUBLANES, :]
            hi = stage_ref[GROUP + c * SUBLANES:GROUP + (c + 1) * SUBLANES, :]
            xs_ref[0, 0, pl.ds(row0, BF16_ROWS), c * LANES:(c + 1) * LANES] = (
                jnp.concatenate([lo, hi], axis=0).astype(BF16))
        return carry

    lax.fori_loop(0, cap // BF16_ROWS, body, 0)


def _dispatch(idx, h_rows, n_exp, cap):
    b = h_rows.shape[0]
    rows = h_rows.shape[1]
    d = ROW_TILE * LANES
    return pl.pallas_call(
        _dispatch_kernel,
        grid=(b, n_exp),
        in_specs=[
            pl.BlockSpec((1, 1, cap), lambda i, e: (i * n_exp + e, 0, 0),
                         memory_space=pltpu.SMEM),
            pl.BlockSpec((1, rows, LANES), lambda i, e: (i, 0, 0),
                         pipeline_mode=pl.Buffered(1)),
        ],
        out_specs=pl.BlockSpec((1, 1, cap, d), lambda i, e: (i, e, 0, 0)),
        out_shape=jax.ShapeDtypeStruct((b, n_exp, cap, d), BF16),
        scratch_shapes=[pltpu.VMEM((2 * GROUP, LANES), F32)],
        compiler_params=pltpu.CompilerParams(
            dimension_semantics=("arbitrary", "arbitrary"),
            vmem_limit_bytes=_vmem(rows * LANES * 4 + 4 * cap * d * 2)),
        name="dispatch",
    )(idx, h_rows)


def _ffn_kernel(xs_ref, wg_ref, wu_ref, wd_ref, gate_ref, y_ref):
    nb = xs_ref.shape[0]

    @pl.when(pl.program_id(1) == 0)
    def _():
        y_ref[...] = jnp.zeros(y_ref.shape, F32)

    wg = wg_ref[0].astype(BF16)
    wu = wu_ref[0].astype(BF16)
    wd = wd_ref[0].astype(BF16)
    for b in range(nb):
        xm = xs_ref[b, 0]
        a = jnp.dot(xm, wg, preferred_element_type=F32)
        u = jnp.dot(xm, wu, preferred_element_type=F32)
        hmid = (_silu(a) * u * gate_ref[b, 0]).astype(BF16)
        contrib = jnp.dot(hmid, wd, preferred_element_type=F32)
        rows = contrib.shape[0]
        for c in range(ROW_TILE):
            y_ref[b, 0, :, c * SUBLANES:(c + 1) * SUBLANES, :] += (
                contrib[:, c * LANES:(c + 1) * LANES].reshape(rows // SUBLANES, SUBLANES, LANES))


def _ffn(xs, w_gate, w_up, w_down, gate):
    b, n_exp, cap, d = xs.shape
    dff = w_gate.shape[2]
    tf = FFN_TILE
    groups = cap // SUBLANES
    return pl.pallas_call(
        _ffn_kernel,
        grid=(n_exp, dff // tf),
        in_specs=[
            pl.BlockSpec((b, 1, cap, d), lambda e, f: (0, e, 0, 0)),
            pl.BlockSpec((1, d, tf), lambda e, f: (e, 0, f)),
            pl.BlockSpec((1, d, tf), lambda e, f: (e, 0, f)),
            pl.BlockSpec((1, tf, d), lambda e, f: (e, f, 0)),
            pl.BlockSpec((b, 1, cap, 1), lambda e, f: (0, e, 0, 0)),
        ],
        out_specs=pl.BlockSpec((b, 1, groups, GROUP, LANES), lambda e, f: (0, e, 0, 0, 0)),
        out_shape=jax.ShapeDtypeStruct((b, n_exp, groups, GROUP, LANES), F32),
        compiler_params=pltpu.CompilerParams(
            dimension_semantics=("arbitrary", "arbitrary"),
            vmem_limit_bytes=_vmem(56 * 2 ** 20)),
        name="ffn",
    )(xs, w_gate, w_up, w_down, gate)


def _combine_kernel(idx_ref, y_ref, xmid_ref, gpost_ref, g2_ref, o_ref, acc_ref, *, n_exp):
    cap = idx_ref.shape[2]
    t = xmid_ref.shape[1]
    step = pl.program_id(1)

    @pl.when(step == 0)
    def _():
        zrows = GROUP

        def zero(r, carry):
            acc_ref[pl.ds(pl.multiple_of(r * zrows, zrows), zrows), :] = jnp.zeros(
                (zrows, LANES), F32)
            return carry

        lax.fori_loop(0, acc_ref.shape[0] // zrows, zero, 0)

    @pl.when(step < n_exp)
    def _():
        def body(r, carry):
            offs, vals = [], []
            for u in range(SCATTER_UNROLL):
                dst = idx_ref[0, 0, r * SCATTER_UNROLL + u]
                src = (r * (SCATTER_UNROLL // SUBLANES) + u // SUBLANES) * GROUP + u % SUBLANES
                offs.append(dst)
                vals.append(acc_ref[pl.ds(dst, ROW_TILE, stride=SUBLANES), :]
                            + y_ref[0, 0, pl.ds(src, ROW_TILE, stride=SUBLANES), :])
            for dst, val in zip(offs, vals):
                acc_ref[pl.ds(dst, ROW_TILE, stride=SUBLANES), :] = val
            return carry

        lax.fori_loop(0, cap // SCATTER_UNROLL, body, 0)

    @pl.when(step >= n_exp)
    def _():
        rows = t * ROW_TILE
        start = pl.multiple_of((step - n_exp) * rows, rows)
        blk = acc_ref[pl.ds(start, rows), :].reshape(t // SUBLANES, GROUP, LANES)
        ffn = jnp.concatenate(
            [blk[:, c * SUBLANES:(c + 1) * SUBLANES, :].reshape(t, LANES)
             for c in range(ROW_TILE)], axis=-1)
        o_ref[0] = xmid_ref[0] + g2_ref[0] * _rms(ffn, gpost_ref[...])


def _combine(idx, y, x_mid, gpost, g2):
    b, n_exp, rows, _ = y.shape
    n, d = x_mid.shape[1:]
    cap = rows // ROW_TILE
    t = TOKEN_BLOCK
    last = n_exp - 1
    tok_block = pl.BlockSpec((1, t, d), lambda i, s: (i, jnp.maximum(s - n_exp, 0), 0))
    return pl.pallas_call(
        functools.partial(_combine_kernel, n_exp=n_exp),
        grid=(b, n_exp + n // t),
        in_specs=[
            pl.BlockSpec((1, 1, cap), lambda i, s: (i * n_exp + jnp.minimum(s, last), 0, 0),
                         memory_space=pltpu.SMEM),
            pl.BlockSpec((1, 1, rows, LANES), lambda i, s: (i, jnp.minimum(s, last), 0, 0)),
            tok_block,
            pl.BlockSpec((1, d), lambda i, s: (0, 0)),
            pl.BlockSpec((1, 1, d), lambda i, s: (i, 0, 0)),
        ],
        out_specs=tok_block,
        out_shape=jax.ShapeDtypeStruct((b, n, d), F32),
        scratch_shapes=[pltpu.VMEM((n * ROW_TILE, LANES), F32)],
        compiler_params=pltpu.CompilerParams(
            dimension_semantics=("arbitrary", "arbitrary"),
            vmem_limit_bytes=_vmem(n * d * 4 + 2 * rows * LANES * 4 + 8 * t * d * 4)),
        name="combine",
    )(idx, y, x_mid, gpost, g2)


def _rope_tables(n_pos):
    half = LANES // 2
    inv = 1.0 / (ROPE_BASE ** (jnp.arange(half, dtype=F32) / half))
    ang_a = (jnp.arange(n_pos // LANES, dtype=F32) * LANES)[:, None] * inv[None, :]
    ang_b = jnp.arange(LANES, dtype=F32)[:, None] * inv[None, :]
    ca, sa = jnp.cos(ang_a)[:, None, :], jnp.sin(ang_a)[:, None, :]
    cb, sb = jnp.cos(ang_b)[None, :, :], jnp.sin(ang_b)[None, :, :]
    cos = (ca * cb - sa * sb).reshape(n_pos, half)
    sin = (sa * cb + ca * sb).reshape(n_pos, half)
    return jnp.concatenate([cos, cos], axis=-1), jnp.concatenate([-sin, sin], axis=-1)


def kernel(x, c, ctx, c_ctx, w_ada, b_ada, pre_mix_g, post_mix_g, pre_ffn_g, post_ffn_g,
           w_in, conv_w, conv_b, ret_decay_logit, ret_norm_g, w_out,
           w_router, b_router, w_gate, w_up, w_down):
    bsz, n, d = x.shape
    ctx_len = ctx.shape[1]
    depth = w_ada.shape[0]
    n_exp = w_router.shape[2]
    cw = conv_w.shape[2]
    rw = RET_HEADS * LANES
    cap = max(1, EC_CAPACITY_FACTOR * n // n_exp)
    assert depth == 1, "only the single-layer block is implemented"
    assert d == ROW_TILE * LANES and n % TOKEN_BLOCK == 0 and TOKEN_BLOCK % LANES == 0
    assert n % GRID_W == 0 and TOKEN_BLOCK % GRID_W == 0 and ctx_len % SUBLANES == 0
    assert n_exp % SUBLANES == 0 and cap % LANES == 0 and (n // LANES) & (n // LANES - 1) == 0
    assert w_in.shape[2] == 3 * cw + 4 * rw

    assert (ctx_len + n) % LANES == 0
    cos_all, sin_all = _rope_tables(ctx_len + n)
    cos_c, sin_c = cos_all[:ctx_len], sin_all[:ctx_len]
    cos_x, sin_x = cos_all[ctx_len:], sin_all[ctx_len:]
    i = 0

    pad = -(bsz + 1) % SUBLANES
    cond = jnp.concatenate([c, c_ctx[None, :], jnp.zeros((pad, d), F32)], axis=0)
    mod = _modulation(cond, w_ada[i], b_ada[i])
    sh1, sc1, g1, sh2, sc2, g2 = [mod[:bsz, j * d:(j + 1) * d].reshape(bsz, 1, d) for j in range(6)]
    csh1 = mod[bsz:bsz + 1, 0:d]
    csc1 = mod[bsz:bsz + 1, d:2 * d]

    lg = jax.nn.log_sigmoid(ret_decay_logit[i].astype(F32))
    w_in_b = w_in[i].astype(BF16)
    w_out_b = w_out[i].astype(BF16)
    k_off = 3 * cw + rw
    row = lambda a: a.reshape(1, -1)

    s_f, s_b = _context_states(lg, ctx, row(pre_mix_g[i]), csc1, csh1,
                               w_in_b[:, k_off:k_off + rw], w_in_b[:, k_off + rw:k_off + 2 * rw],
                               cos_c, sin_c)

    q, k, v, g, yb, conv = _mix_bwd(lg, x, row(pre_mix_g[i]), sc1, sh1, w_in_b,
                                    conv_w[i], row(conv_b[i]), cos_x, sin_x, s_b)

    wr = w_router[i].astype(F32)
    wr_hi = wr.astype(BF16)
    wr_lo = (wr - wr_hi.astype(F32)).astype(BF16)
    wr_cat = jnp.concatenate([wr_hi, wr_lo], axis=1)
    br = b_router[i].astype(F32).reshape(1, n_exp)
    x_mid, h_tiles, logits = _mix_fwd(lg, x, q, k, v, g, yb, conv, w_out_b, row(ret_norm_g[i]),
                                      row(post_mix_g[i]), row(pre_ffn_g[i]), g1, sc2, sh2,
                                      wr_cat, br, s_f)

    logits_t = logits.reshape(bsz, n // LANES, LANES, n_exp).transpose(0, 1, 3, 2)
    gate5, idx5 = _route(logits_t, cap)
    gate = gate5.transpose(0, 1, 3, 2, 4).reshape(bsz, n_exp, cap)
    idx = idx5.transpose(0, 1, 3, 2, 4).reshape(bsz * n_exp, 1, cap)

    xs = _dispatch(idx, h_tiles.reshape(bsz, n * ROW_TILE, LANES), n_exp, cap)
    y = _ffn(xs, w_gate[i], w_up[i], w_down[i], gate.reshape(bsz, n_exp, cap, 1))
    return _combine(idx, y.reshape(bsz, n_exp, cap * ROW_TILE, LANES), x_mid,
                    row(post_ffn_g[i]), g2)
```

```python
import functools

import jax
import jax.numpy as jnp
from jax import lax
from jax.experimental import pallas as pl
from jax.experimental.pallas import tpu as pltpu

F32 = jnp.float32
BF16 = jnp.bfloat16

GRID_W = 64
RET_HEADS = 4
ROPE_BASE = 10000.0
EC_CAPACITY_FACTOR = 2
RMS_EPS = 1e-6
GN_EPS = 1e-6

LANES = 128
SUBLANES = 8
ROW_TILE = SUBLANES
GROUP = SUBLANES * ROW_TILE
VMEM_CAP = 60000 * 1024

TOKEN_BLOCK = 512
FFN_TILE = 512
SCATTER_UNROLL = 2 * SUBLANES


def _vmem(nbytes):
    return min(int(nbytes), VMEM_CAP)


def _rms(x, g):
    return x * lax.rsqrt(jnp.mean(x * x, axis=-1, keepdims=True) + RMS_EPS) * g


def _silu(x):
    return x * (1.0 / (1.0 + jnp.exp(-x)))


def _rot_half(t):
    return pltpu.roll(t, LANES // 2, axis=1)


def _store_tile_major(ref, lead, val):
    rows = val.shape[0]
    for c in range(ROW_TILE):
        ref[lead + (slice(None), slice(c * SUBLANES, (c + 1) * SUBLANES), slice(None))] = (
            val[:, c * LANES:(c + 1) * LANES].reshape(rows // SUBLANES, SUBLANES, LANES))


def _row_base(r):
    return (r >> 3) * GROUP + (r & (SUBLANES - 1))


def _modulation_kernel(cond_ref, w_ref, b_ref, o_ref):
    s = _silu(cond_ref[...])
    o_ref[...] = jnp.dot(s, w_ref[...], preferred_element_type=F32,
                         precision=lax.Precision.HIGHEST) + b_ref[...]


def _modulation(cond, w_ada, b_ada):
    rows, d = cond.shape
    cols = w_ada.shape[1]
    tile = cols // 4
    return pl.pallas_call(
        _modulation_kernel,
        grid=(cols // tile,),
        in_specs=[
            pl.BlockSpec((rows, d), lambda j: (0, 0)),
            pl.BlockSpec((d, tile), lambda j: (0, j)),
            pl.BlockSpec((1, tile), lambda j: (0, j)),
        ],
        out_specs=pl.BlockSpec((rows, tile), lambda j: (0, j)),
        out_shape=jax.ShapeDtypeStruct((rows, cols), F32),
        compiler_params=pltpu.CompilerParams(
            dimension_semantics=("arbitrary",),
            vmem_limit_bytes=_vmem(3 * d * tile * 4)),
        name="modulation",
    )(cond, w_ada, b_ada.reshape(1, cols))


def _context_kernel(lg_ref, ctx_ref, gpre_ref, sc_ref, sh_ref, wk_ref, wv_ref,
                    cos_ref, sin_ref, sf_ref, sb_ref):
    n = ctx_ref.shape[1]
    hd = LANES
    hc = _rms(ctx_ref[0], gpre_ref[...]) * (1.0 + sc_ref[...]) + sh_ref[...]
    hc = hc.astype(BF16)
    k = jnp.dot(hc, wk_ref[...], preferred_element_type=F32)
    v = jnp.dot(hc, wv_ref[...], preferred_element_type=F32)
    cos = cos_ref[...]
    sin = sin_ref[...]
    pos = lax.broadcasted_iota(jnp.int32, (n, 1), 0).astype(F32)
    for h in range(RET_HEADS):
        kh = k[:, h * hd:(h + 1) * hd]
        kr = (kh * cos + _rot_half(kh) * sin) * (hd ** -0.5)
        vh = v[:, h * hd:(h + 1) * hd].astype(BF16)
        wf = jnp.exp(lg_ref[0, h] * (n - 1.0 - pos))
        wb = jnp.exp(lg_ref[1, h] * pos)
        kf = (kr * wf).T.astype(BF16)
        kb = (kr * wb).T.astype(BF16)
        sf_ref[0, h] = jnp.dot(kf, vh, preferred_element_type=F32)
        sb_ref[0, h] = jnp.dot(kb, vh, preferred_element_type=F32)


def _context_states(lg, ctx, gpre, csc1, csh1, w_in, k_col, cos, sin):
    b, n, d = ctx.shape
    hd = LANES
    rw = RET_HEADS * hd
    st = jax.ShapeDtypeStruct((b, RET_HEADS, hd, hd), F32)
    vec = pl.BlockSpec((1, d), lambda i: (0, 0))
    return pl.pallas_call(
        _context_kernel,
        grid=(b,),
        in_specs=[
            pl.BlockSpec(memory_space=pltpu.SMEM),
            pl.BlockSpec((1, n, d), lambda i: (i, 0, 0)),
            vec, vec, vec,
            pl.BlockSpec((d, rw), lambda i: (0, k_col)),
            pl.BlockSpec((d, rw), lambda i: (0, k_col + 1)),
            pl.BlockSpec((n, hd), lambda i: (0, 0)),
            pl.BlockSpec((n, hd), lambda i: (0, 0)),
        ],
        out_specs=[pl.BlockSpec((1, RET_HEADS, hd, hd), lambda i: (i, 0, 0, 0))] * 2,
        out_shape=[st, st],
        compiler_params=pltpu.CompilerParams(dimension_semantics=("arbitrary",)),
        name="context",
    )(lg, ctx, gpre, csc1, csh1, w_in, w_in, cos, sin)


def _fill_tables(lg_ref, direction, decay_ref, xi_ref, zeta_ref):
    c = decay_ref.shape[1]
    i = lax.broadcasted_iota(jnp.int32, (c, c), 0)
    j = lax.broadcasted_iota(jnp.int32, (c, c), 1)
    col = lax.broadcasted_iota(jnp.int32, (c, 1), 0).astype(F32)
    for h in range(RET_HEADS):
        lg = lg_ref[direction, h]
        if direction == 0:
            rel = (i - j).astype(F32)
            mask = i >= j
            xi_ref[h] = jnp.exp(lg * (col + 1.0))
            zeta_ref[h] = jnp.exp(lg * (c - 1.0 - col))
        else:
            rel = (j - i).astype(F32)
            mask = j > i
            xi_ref[h] = jnp.exp(lg * (c - col))
            zeta_ref[h] = jnp.exp(lg * col)
        decay_ref[h] = jnp.where(mask, jnp.exp(lg * jnp.where(mask, rel, 0.0)), 0.0)


def _retention_block(qh, kh_f32, vh, lg, s_ref, decay_ref, xi_ref, zeta_ref, h):
    c = decay_ref.shape[1]
    kh = kh_f32.astype(BF16)
    scores = lax.dot_general(qh, kh, (((1,), (1,)), ((), ())), preferred_element_type=F32)
    s_prev = s_ref[h]
    lhs = jnp.concatenate([(scores * decay_ref[h]).astype(BF16),
                           (qh.astype(F32) * xi_ref[h]).astype(BF16)], axis=1)
    rhs = jnp.concatenate([vh, s_prev.astype(BF16)], axis=0)
    out = jnp.dot(lhs, rhs, preferred_element_type=F32)
    kz = (kh_f32 * zeta_ref[h]).T.astype(BF16)
    s_ref[h] = jnp.exp(lg * c) * s_prev + jnp.dot(kz, vh, preferred_element_type=F32)
    return out


def _mix_bwd_kernel(lg_ref, x_ref, gpre_ref, sc_ref, sh_ref, win_ref, cw_ref, cb_ref,
                    cos_ref, sin_ref, s0_ref,
                    q_ref, k_ref, v_ref, g_ref, yb_ref, conv_ref,
                    s_ref, qs_ref, ks_ref, vs_ref, decay_ref, xi_ref, zeta_ref, *, nb):
    t = x_ref.shape[1]
    hd = LANES
    cw = conv_ref.shape[2]
    rw = RET_HEADS * hd
    step = pl.program_id(0)
    cur = step % 2
    prev = 1 - cur

    @pl.when(step == 0)
    def _():
        s_ref[...] = jnp.zeros(s_ref.shape, F32)
        qs_ref[prev] = jnp.zeros(qs_ref.shape[1:], BF16)
        ks_ref[prev] = jnp.zeros(ks_ref.shape[1:], F32)
        vs_ref[prev] = jnp.zeros(vs_ref.shape[1:], BF16)
        _fill_tables(lg_ref, 1, decay_ref, xi_ref, zeta_ref)

    @pl.when(step % nb == 1)
    def _():
        s_ref[...] = s0_ref[0]

    for h in range(RET_HEADS):
        sl = slice(h * hd, (h + 1) * hd)
        yb_ref[0, :, sl] = _retention_block(
            qs_ref[prev, :, sl], ks_ref[prev, :, sl], vs_ref[prev, :, sl],
            lg_ref[1, h], s_ref, decay_ref, xi_ref, zeta_ref, h)

    hx = _rms(x_ref[0], gpre_ref[...]) * (1.0 + sc_ref[0]) + sh_ref[0]
    proj = jnp.dot(hx.astype(BF16), win_ref[...], preferred_element_type=F32)
    x_b = proj[:, 0:cw]
    x_c = proj[:, cw:2 * cw]
    x_x = proj[:, 2 * cw:3 * cw]
    off = 3 * cw
    q = proj[:, off:off + rw]
    k = proj[:, off + rw:off + 2 * rw]
    v = proj[:, off + 2 * rw:off + 3 * rw]
    g_ref[0] = proj[:, off + 3 * rw:off + 4 * rw]

    u = x_c * x_x
    col = lax.broadcasted_iota(jnp.int32, (t, cw), 0) & (GRID_W - 1)
    u_prev = jnp.where(col == 0, 0.0, pltpu.roll(u, 1, axis=0))
    u_next = jnp.where(col == GRID_W - 1, 0.0, pltpu.roll(u, t - 1, axis=0))
    y = u_prev * cw_ref[0:1, :] + u * cw_ref[1:2, :] + u_next * cw_ref[2:3, :] + cb_ref[...]
    conv_ref[0] = (x_b * y).astype(BF16)

    cos = cos_ref[...]
    sin = sin_ref[...]
    v_bf = v.astype(BF16)
    v_ref[0] = v_bf
    vs_ref[cur] = v_bf
    for h in range(RET_HEADS):
        sl = slice(h * hd, (h + 1) * hd)
        qh = q[:, sl]
        kh = k[:, sl]
        qr = (qh * cos + _rot_half(qh) * sin).astype(BF16)
        kr = (kh * cos + _rot_half(kh) * sin) * (hd ** -0.5)
        q_ref[0, :, sl] = qr
        qs_ref[cur, :, sl] = qr
        k_ref[0, :, sl] = kr.astype(BF16)
        ks_ref[cur, :, sl] = kr


def _mix_bwd(lg, x, gpre, sc1, sh1, w_in, conv_w, conv_b, cos, sin, s0_b):
    b, n, d = x.shape
    t = TOKEN_BLOCK
    nb = n // t
    steps = b * nb
    hd = LANES
    cw = conv_w.shape[1]
    rw = RET_HEADS * hd
    proj = lambda s: jnp.minimum(s, steps - 1)
    retn = lambda s: jnp.maximum(s - 1, 0)
    blk = lambda g: (g // nb, nb - 1 - g % nb, 0)
    vec = pl.BlockSpec((1, d), lambda s: (0, 0))
    bvec = pl.BlockSpec((1, 1, d), lambda s: (proj(s) // nb, 0, 0))
    tok = lambda w: pl.BlockSpec((1, t, w), lambda s: blk(proj(s)))
    rope = pl.BlockSpec((t, hd), lambda s: (nb - 1 - proj(s) % nb, 0))
    out = lambda w, dt: jax.ShapeDtypeStruct((b, n, w), dt)
    return pl.pallas_call(
        functools.partial(_mix_bwd_kernel, nb=nb),
        grid=(steps + 1,),
        in_specs=[
            pl.BlockSpec(memory_space=pltpu.SMEM),
            tok(d), vec, bvec, bvec,
            pl.BlockSpec(w_in.shape, lambda s: (0, 0)),
            pl.BlockSpec(conv_w.shape, lambda s: (0, 0)),
            pl.BlockSpec((1, cw), lambda s: (0, 0)),
            rope, rope,
            pl.BlockSpec((1, RET_HEADS, hd, hd), lambda s: (retn(s) // nb, 0, 0, 0)),
        ],
        out_specs=[tok(rw), tok(rw), tok(rw), tok(rw),
                   pl.BlockSpec((1, t, rw), lambda s: blk(retn(s))), tok(cw)],
        out_shape=[out(rw, BF16), out(rw, BF16), out(rw, BF16), out(rw, F32),
                   out(rw, F32), out(cw, BF16)],
        scratch_shapes=[pltpu.VMEM((RET_HEADS, hd, hd), F32),
                        pltpu.VMEM((2, t, rw), BF16), pltpu.VMEM((2, t, rw), F32),
                        pltpu.VMEM((2, t, rw), BF16),
                        pltpu.VMEM((RET_HEADS, t, t), F32),
                        pltpu.VMEM((RET_HEADS, t, 1), F32),
                        pltpu.VMEM((RET_HEADS, t, 1), F32)],
        compiler_params=pltpu.CompilerParams(
            dimension_semantics=("arbitrary",),
            vmem_limit_bytes=_vmem(52 * 2 ** 20)),
        name="mix_bwd",
    )(lg, x, gpre, sc1, sh1, w_in, conv_w, conv_b, cos, sin, s0_b)


def _mix_fwd_kernel(lg_ref, x_ref, q_ref, k_ref, v_ref, g_ref, yb_ref, conv_ref,
                    wout_ref, gn_ref, gpost_ref, gffn_ref, g1_ref, sc_ref, sh_ref,
                    wr_ref, br_ref, s0_ref,
                    xmid_ref, h2_ref, logit_ref, s_ref, mix_ref, prev_ref,
                    decay_ref, xi_ref, zeta_ref, *, nb):
    hd = LANES
    cw = conv_ref.shape[2]
    step = pl.program_id(0)

    @pl.when(step == 0)
    def _():
        prev_ref[...] = jnp.zeros(prev_ref.shape, BF16)
        _fill_tables(lg_ref, 0, decay_ref, xi_ref, zeta_ref)

    @pl.when(step % nb == 0)
    def _():
        s_ref[...] = s0_ref[0]

    t = x_ref.shape[1]
    ne = logit_ref.shape[2]
    mix = jnp.dot(prev_ref[...], wout_ref[...], preferred_element_type=F32)

    def finish_rows(r0, r1):
        x_mid = x_ref[0, r0:r1] + g1_ref[0] * _rms(mix[r0:r1], gpost_ref[...])
        xmid_ref[0, r0:r1] = x_mid
        h2 = _rms(x_mid, gffn_ref[...]) * (1.0 + sc_ref[0]) + sh_ref[0]
        h_hi = h2.astype(BF16)
        h_lo = (h2 - h_hi.astype(F32)).astype(BF16)
        nt = (((1,), (1,)), ((), ()))
        p = (lax.dot_general(wr_ref[...], h_hi, nt, preferred_element_type=F32)
             + lax.dot_general(wr_ref[...], h_lo, nt, preferred_element_type=F32))
        logit_ref[0, r0 // LANES] = p[0:ne] + p[ne:2 * ne] + br_ref[...]
        for c in range(ROW_TILE):
            h2_ref[0, r0 // SUBLANES:r1 // SUBLANES, c * SUBLANES:(c + 1) * SUBLANES, :] = (
                h2[:, c * LANES:(c + 1) * LANES].reshape((r1 - r0) // SUBLANES, SUBLANES, LANES))

    mix_ref[:, 0:cw] = conv_ref[0]
    rows = t // RET_HEADS
    for h in range(RET_HEADS):
        sl = slice(h * hd, (h + 1) * hd)
        y = _retention_block(q_ref[0, :, sl], k_ref[0, :, sl].astype(F32), v_ref[0, :, sl],
                             lg_ref[0, h], s_ref, decay_ref, xi_ref, zeta_ref, h)
        finish_rows(h * rows, (h + 1) * rows)
        y = y + yb_ref[0, :, sl]
        mu = jnp.mean(y, axis=-1, keepdims=True)
        yc = y - mu
        var = jnp.mean(yc * yc, axis=-1, keepdims=True)
        yn = yc * lax.rsqrt(var + GN_EPS) * gn_ref[:, sl]
        mix_ref[:, cw + h * hd:cw + (h + 1) * hd] = (
            _silu(g_ref[0, :, sl]) * yn).astype(BF16)
    prev_ref[...] = mix_ref[...]


def _mix_fwd(lg, x, q, k, v, g, yb, conv, w_out, gn, gpost, gffn, g1, sc2, sh2, wr, br, s0_f):
    b, n, d = x.shape
    t = TOKEN_BLOCK
    nb = n // t
    hd = LANES
    cw = conv.shape[2]
    rw = RET_HEADS * hd
    ne = br.shape[0]
    assert t // RET_HEADS == LANES
    steps = b * nb
    ret = lambda s: jnp.minimum(s, steps - 1)
    outp = lambda s: jnp.maximum(s - 1, 0)
    vec = lambda w: pl.BlockSpec((1, w), lambda s: (0, 0))
    bvec = pl.BlockSpec((1, 1, d), lambda s: (outp(s) // nb, 0, 0))
    rtok = lambda w: pl.BlockSpec((1, t, w), lambda s: (ret(s) // nb, ret(s) % nb, 0))
    otok = lambda w: pl.BlockSpec((1, t, w), lambda s: (outp(s) // nb, outp(s) % nb, 0))
    return pl.pallas_call(
        functools.partial(_mix_fwd_kernel, nb=nb),
        grid=(steps + 1,),
        in_specs=[
            pl.BlockSpec(memory_space=pltpu.SMEM),
            otok(d), rtok(rw), rtok(rw), rtok(rw), rtok(rw), rtok(rw), rtok(cw),
            pl.BlockSpec(w_out.shape, lambda s: (0, 0)),
            vec(rw), vec(d), vec(d), bvec, bvec, bvec,
            pl.BlockSpec(wr.shape, lambda s: (0, 0)),
            pl.BlockSpec((ne, 1), lambda s: (0, 0)),
            pl.BlockSpec((1, RET_HEADS, hd, hd), lambda s: (ret(s) // nb, 0, 0, 0)),
        ],
        out_specs=[otok(d),
                   pl.BlockSpec((1, t // SUBLANES, GROUP, LANES),
                                lambda s: (outp(s) // nb, outp(s) % nb, 0, 0)),
                   pl.BlockSpec((1, t // LANES, ne, LANES),
                                lambda s: (outp(s) // nb, outp(s) % nb, 0, 0))],
        out_shape=[jax.ShapeDtypeStruct((b, n, d), F32),
                   jax.ShapeDtypeStruct((b, n // SUBLANES, GROUP, LANES), F32),
                   jax.ShapeDtypeStruct((b, n // LANES, ne, LANES), F32)],
        scratch_shapes=[pltpu.VMEM((RET_HEADS, hd, hd), F32),
                        pltpu.VMEM((t, cw + rw), BF16),
                        pltpu.VMEM((t, cw + rw), BF16),
                        pltpu.VMEM((RET_HEADS, t, t), F32),
                        pltpu.VMEM((RET_HEADS, t, 1), F32),
                        pltpu.VMEM((RET_HEADS, t, 1), F32)],
        compiler_params=pltpu.CompilerParams(
            dimension_semantics=("arbitrary",),
            vmem_limit_bytes=_vmem(48 * 2 ** 20)),
        name="mix_fwd",
    )(lg, x, q, k, v, g, yb, conv, w_out, gn, gpost, gffn, g1, sc2, sh2, wr, br, s0_f)


def _route_kernel(logit_ref, gate_ref, idx_ref, *, cap):
    nv, ne, _ = logit_ref.shape[1:]
    grp = SUBLANES
    x = logit_ref[0]
    m = jnp.max(x, axis=1, keepdims=True)
    s = jnp.sum(jnp.exp(x - m), axis=1, keepdims=True)
    e0 = pl.multiple_of(pl.program_id(1) * grp, grp)
    xg = logit_ref[0, :, pl.ds(e0, grp), :]
    key = jnp.exp(xg - m) / s

    top = cap // LANES

    def coords(nvreg):
        shape = (nvreg, grp, LANES)
        vi = lax.broadcasted_iota(jnp.int32, shape, 0)
        li = lax.broadcasted_iota(jnp.int32, shape, 2)
        return vi, li, (vi // top) * cap + li * top + vi % top

    def better(pk, pi, k, i):
        return (pk > k) | ((pk == k) & (pi < i))

    def stage(k, i, li, pos, dist, span):
        if dist < top:
            halves = lambda a: a.reshape(a.shape[0] // (2 * dist), 2, dist, grp, LANES)
            k5, i5 = halves(k), halves(i)
            k_lo, k_hi, i_lo, i_hi = k5[:, 0], k5[:, 1], i5[:, 0], i5[:, 1]
            hi_wins = better(k_hi, i_hi, k_lo, i_lo)
            k_b, k_w = jnp.where(hi_wins, k_hi, k_lo), jnp.where(hi_wins, k_lo, k_hi)
            i_b, i_w = jnp.where(hi_wins, i_hi, i_lo), jnp.where(hi_wins, i_lo, i_hi)
            if span is not None:
                desc = (halves(pos)[:, 0] & span) == 0
                k_b, k_w = jnp.where(desc, k_b, k_w), jnp.where(desc, k_w, k_b)
                i_b, i_w = jnp.where(desc, i_b, i_w), jnp.where(desc, i_w, i_b)
            join = lambda lo, hi: jnp.concatenate([lo[:, None], hi[:, None]], axis=1).reshape(k.shape)
            return join(k_b, k_w), join(i_b, i_w)
        sh = dist // top
        low = (li & sh) == 0
        pk = jnp.where(low, pltpu.roll(k, LANES - sh, axis=2), pltpu.roll(k, sh, axis=2))
        pi = jnp.where(low, pltpu.roll(i, LANES - sh, axis=2), pltpu.roll(i, sh, axis=2))
        want_best = low if span is None else low == ((pos & span) == 0)
        take = better(pk, pi, k, i) == want_best
        return jnp.where(take, pk, k), jnp.where(take, pi, i)

    vi, li, pos = coords(nv)
    idx = vi * LANES + li
    span = 2
    while span <= cap:
        dist = span // 2
        while dist >= 1:
            key, idx = stage(key, idx, li, pos, dist, span)
            dist //= 2
        span *= 2
    nblk = nv // top
    while nblk > 1:
        k5 = key.reshape(nblk // 2, 2, top, grp, LANES)
        i5 = idx.reshape(nblk // 2, 2, top, grp, LANES)
        k0, k1, i0, i1 = k5[:, 0], k5[:, 1], i5[:, 0], i5[:, 1]
        take = better(k1, i1, k0, i0)
        nblk //= 2
        key = jnp.where(take, k1, k0).reshape(nblk * top, grp, LANES)
        idx = jnp.where(take, i1, i0).reshape(nblk * top, grp, LANES)
        vi, li, pos = coords(nblk * top)
        dist = cap // 2
        while dist >= 1:
            key, idx = stage(key, idx, li, pos, dist, None if nblk == 1 else cap)
            dist //= 2
    gate_ref[0, 0] = key[0:top]
    idx_ref[0, 0] = _row_base(idx[0:top])


def _route(logits_t, cap):
    b, nv, ne, _ = logits_t.shape
    grp = SUBLANES
    top = cap // LANES
    out_block = pl.BlockSpec((1, 1, top, grp, LANES), lambda i, j: (i, j, 0, 0, 0))
    return pl.pallas_call(
        functools.partial(_route_kernel, cap=cap),
        grid=(b, ne // grp),
        in_specs=[pl.BlockSpec((1, nv, ne, LANES), lambda i, j: (i, 0, 0, 0))],
        out_specs=[out_block, out_block],
        out_shape=[jax.ShapeDtypeStruct((b, ne // grp, top, grp, LANES), F32),
                   jax.ShapeDtypeStruct((b, ne // grp, top, grp, LANES), jnp.int32)],
        compiler_params=pltpu.CompilerParams(dimension_semantics=("arbitrary", "arbitrary")),
        name="route",
    )(logits_t)


BF16_ROWS = 2 * SUBLANES


def _dispatch_kernel(idx_ref, h_ref, xs_ref, stage_ref):
    cap = idx_ref.shape[2]

    def body(r, carry):
        for u in range(BF16_ROWS):
            src = idx_ref[0, 0, r * BF16_ROWS + u]
            dst = (u // SUBLANES) * GROUP + u % SUBLANES
            stage_ref[pl.ds(dst, ROW_TILE, stride=SUBLANES), :] = (
                h_ref[0, pl.ds(src, ROW_TILE, stride=SUBLANES), :])
        row0 = pl.multiple_of(r * BF16_ROWS, BF16_ROWS)
        for c in range(ROW_TILE):
            lo = stage_ref[c * SUBLANES:(c + 1) * SUBLANES, :]
            hi = stage_ref[GROUP + c * SUBLANES:GROUP + (c + 1) * SUBLANES, :]
            xs_ref[0, 0, pl.ds(row0, BF16_ROWS), c * LANES:(c + 1) * LANES] = (
                jnp.concatenate([lo, hi], axis=0).astype(BF16))
        return carry

    lax.fori_loop(0, cap // BF16_ROWS, body, 0)


def _dispatch(idx, h_rows, n_exp, cap):
    b = h_rows.shape[0]
    rows = h_rows.shape[1]
    d = ROW_TILE * LANES
    return pl.pallas_call(
        _dispatch_kernel,
        grid=(b, n_exp),
        in_specs=[
            pl.BlockSpec((1, 1, cap), lambda i, e: (i * n_exp + e, 0, 0),
                         memory_space=pltpu.SMEM),
            pl.BlockSpec((1, rows, LANES), lambda i, e: (i, 0, 0),
                         pipeline_mode=pl.Buffered(1)),
        ],
        out_specs=pl.BlockSpec((1, 1, cap, d), lambda i, e: (i, e, 0, 0)),
        out_shape=jax.ShapeDtypeStruct((b, n_exp, cap, d), BF16),
        scratch_shapes=[pltpu.VMEM((2 * GROUP, LANES), F32)],
        compiler_params=pltpu.CompilerParams(
            dimension_semantics=("arbitrary", "arbitrary"),
            vmem_limit_bytes=_vmem(rows * LANES * 4 + 4 * cap * d * 2)),
        name="dispatch",
    )(idx, h_rows)


def _ffn_kernel(xs_ref, wg_ref, wu_ref, wd_ref, y_ref):
    nb = xs_ref.shape[0]
    tf = wg_ref.shape[2]

    @pl.when(pl.program_id(1) == 0)
    def _():
        y_ref[...] = jnp.zeros(y_ref.shape, F32)

    wgu = jnp.concatenate([wg_ref[0].astype(BF16), wu_ref[0].astype(BF16)], axis=1)
    wd = wd_ref[0].astype(BF16)
    for b in range(nb):
        au = jnp.dot(xs_ref[b, 0], wgu, preferred_element_type=F32)
        hmid = (_silu(au[:, 0:tf]) * au[:, tf:2 * tf]).astype(BF16)
        contrib = jnp.dot(hmid, wd, preferred_element_type=F32)
        rows = contrib.shape[0]
        for c in range(ROW_TILE):
            y_ref[b, 0, :, c * SUBLANES:(c + 1) * SUBLANES, :] += (
                contrib[:, c * LANES:(c + 1) * LANES].reshape(rows // SUBLANES, SUBLANES, LANES))


def _ffn(xs, w_gate, w_up, w_down):
    b, n_exp, cap, d = xs.shape
    dff = w_gate.shape[2]
    tf = FFN_TILE
    groups = cap // SUBLANES
    return pl.pallas_call(
        _ffn_kernel,
        grid=(n_exp, dff // tf),
        in_specs=[
            pl.BlockSpec((b, 1, cap, d), lambda e, f: (0, e, 0, 0)),
            pl.BlockSpec((1, d, tf), lambda e, f: (e, 0, f)),
            pl.BlockSpec((1, d, tf), lambda e, f: (e, 0, f)),
            pl.BlockSpec((1, tf, d), lambda e, f: (e, f, 0)),
        ],
        out_specs=pl.BlockSpec((b, 1, groups, GROUP, LANES), lambda e, f: (0, e, 0, 0, 0)),
        out_shape=jax.ShapeDtypeStruct((b, n_exp, groups, GROUP, LANES), F32),
        compiler_params=pltpu.CompilerParams(
            dimension_semantics=("arbitrary", "arbitrary"),
            vmem_limit_bytes=_vmem(56 * 2 ** 20)),
        name="ffn",
    )(xs, w_gate, w_up, w_down)


def _combine_kernel(idx_ref, gate_ref, y_ref, xmid_ref, gpost_ref, g2_ref, o_ref, acc_ref,
                    *, n_exp):
    cap = idx_ref.shape[2]
    t = xmid_ref.shape[1]
    step = pl.program_id(1)

    @pl.when(step == 0)
    def _():
        zrows = GROUP

        def zero(r, carry):
            acc_ref[pl.ds(pl.multiple_of(r * zrows, zrows), zrows), :] = jnp.zeros(
                (zrows, LANES), F32)
            return carry

        lax.fori_loop(0, acc_ref.shape[0] // zrows, zero, 0)

    @pl.when(step < n_exp)
    def _():
        def body(r, carry):
            offs, vals = [], []
            for u in range(SCATTER_UNROLL):
                dst = idx_ref[0, 0, r * SCATTER_UNROLL + u]
                gate = gate_ref[0, 0, r * SCATTER_UNROLL + u]
                src = (r * (SCATTER_UNROLL // SUBLANES) + u // SUBLANES) * GROUP + u % SUBLANES
                offs.append(dst)
                vals.append(acc_ref[pl.ds(dst, ROW_TILE, stride=SUBLANES), :]
                            + gate * y_ref[0, 0, pl.ds(src, ROW_TILE, stride=SUBLANES), :])
            for dst, val in zip(offs, vals):
                acc_ref[pl.ds(dst, ROW_TILE, stride=SUBLANES), :] = val
            return carry

        lax.fori_loop(0, cap // SCATTER_UNROLL, body, 0)

    @pl.when(step >= n_exp)
    def _():
        rows = t * ROW_TILE
        start = pl.multiple_of((step - n_exp) * rows, rows)
        blk = acc_ref[pl.ds(start, rows), :].reshape(t // SUBLANES, GROUP, LANES)
        ffn = jnp.concatenate(
            [blk[:, c * SUBLANES:(c + 1) * SUBLANES, :].reshape(t, LANES)
             for c in range(ROW_TILE)], axis=-1)
        o_ref[0] = xmid_ref[0] + g2_ref[0] * _rms(ffn, gpost_ref[...])


def _combine(idx, gate, y, x_mid, gpost, g2):
    b, n_exp, rows, _ = y.shape
    n, d = x_mid.shape[1:]
    cap = rows // ROW_TILE
    t = TOKEN_BLOCK
    last = n_exp - 1
    tok_block = pl.BlockSpec((1, t, d), lambda i, s: (i, jnp.maximum(s - n_exp, 0), 0))
    per_expert = pl.BlockSpec((1, 1, cap), lambda i, s: (i * n_exp + jnp.minimum(s, last), 0, 0),
                              memory_space=pltpu.SMEM)
    return pl.pallas_call(
        functools.partial(_combine_kernel, n_exp=n_exp),
        grid=(b, n_exp + n // t),
        in_specs=[
            per_expert, per_expert,
            pl.BlockSpec((1, 1, rows, LANES), lambda i, s: (i, jnp.minimum(s, last), 0, 0)),
            tok_block,
            pl.BlockSpec((1, d), lambda i, s: (0, 0)),
            pl.BlockSpec((1, 1, d), lambda i, s: (i, 0, 0)),
        ],
        out_specs=tok_block,
        out_shape=jax.ShapeDtypeStruct((b, n, d), F32),
        scratch_shapes=[pltpu.VMEM((n * ROW_TILE, LANES), F32)],
        compiler_params=pltpu.CompilerParams(
            dimension_semantics=("arbitrary", "arbitrary"),
            vmem_limit_bytes=_vmem(n * d * 4 + 2 * rows * LANES * 4 + 8 * t * d * 4)),
        name="combine",
    )(idx, gate, y, x_mid, gpost, g2)


def _rope_tables(n_pos):
    half = LANES // 2
    inv = 1.0 / (ROPE_BASE ** (jnp.arange(half, dtype=F32) / half))
    ang_a = (jnp.arange(n_pos // LANES, dtype=F32) * LANES)[:, None] * inv[None, :]
    ang_b = jnp.arange(LANES, dtype=F32)[:, None] * inv[None, :]
    ca, sa = jnp.cos(ang_a)[:, None, :], jnp.sin(ang_a)[:, None, :]
    cb, sb = jnp.cos(ang_b)[None, :, :], jnp.sin(ang_b)[None, :, :]
    cos = (ca * cb - sa * sb).reshape(n_pos, half)
    sin = (sa * cb + ca * sb).reshape(n_pos, half)
    return jnp.concatenate([cos, cos], axis=-1), jnp.concatenate([-sin, sin], axis=-1)


def kernel(x, c, ctx, c_ctx, w_ada, b_ada, pre_mix_g, post_mix_g, pre_ffn_g, post_ffn_g,
           w_in, conv_w, conv_b, ret_decay_logit, ret_norm_g, w_out,
           w_router, b_router, w_gate, w_up, w_down):
    bsz, n, d = x.shape
    ctx_len = ctx.shape[1]
    depth = w_ada.shape[0]
    n_exp = w_router.shape[2]
    cw = conv_w.shape[2]
    rw = RET_HEADS * LANES
    cap = max(1, EC_CAPACITY_FACTOR * n // n_exp)
    assert depth == 1, "only the single-layer block is implemented"
    assert d == ROW_TILE * LANES and n % TOKEN_BLOCK == 0 and TOKEN_BLOCK % LANES == 0
    assert n % GRID_W == 0 and TOKEN_BLOCK % GRID_W == 0 and ctx_len % SUBLANES == 0
    assert n_exp % SUBLANES == 0 and cap % LANES == 0 and (n // LANES) & (n // LANES - 1) == 0
    assert w_in.shape[2] == 3 * cw + 4 * rw

    assert (ctx_len + n) % LANES == 0
    cos_all, sin_all = _rope_tables(ctx_len + n)
    cos_c, sin_c = cos_all[:ctx_len], sin_all[:ctx_len]
    cos_x, sin_x = cos_all[ctx_len:], sin_all[ctx_len:]
    i = 0

    pad = -(bsz + 1) % SUBLANES
    cond = jnp.concatenate([c, c_ctx[None, :], jnp.zeros((pad, d), F32)], axis=0)
    mod = _modulation(cond, pltpu.with_memory_space_constraint(w_ada[i], pltpu.HBM), b_ada[i])
    sh1, sc1, g1, sh2, sc2, g2 = [mod[:bsz, j * d:(j + 1) * d].reshape(bsz, 1, d) for j in range(6)]
    csh1 = mod[bsz:bsz + 1, 0:d]
    csc1 = mod[bsz:bsz + 1, d:2 * d]

    lg = jax.nn.log_sigmoid(ret_decay_logit[i].astype(F32))
    w_in_b = w_in[i].astype(BF16)
    w_out_b = w_out[i].astype(BF16)
    k_off = 3 * cw + rw
    row = lambda a: a.reshape(1, -1)

    assert k_off % rw == 0
    s_f, s_b = _context_states(lg, ctx, row(pre_mix_g[i]), csc1, csh1, w_in_b, k_off // rw,
                               cos_c, sin_c)

    q, k, v, g, yb, conv = _mix_bwd(lg, x, row(pre_mix_g[i]), sc1, sh1, w_in_b,
                                    conv_w[i], row(conv_b[i]), cos_x, sin_x, s_b)

    wr = w_router[i].astype(F32).T
    wr_hi = wr.astype(BF16)
    wr_lo = (wr - wr_hi.astype(F32)).astype(BF16)
    wr_cat = jnp.concatenate([wr_hi, wr_lo], axis=0)
    br = b_router[i].astype(F32).reshape(n_exp, 1)
    x_mid, h_tiles, logits_t = _mix_fwd(lg, x, q, k, v, g, yb, conv, w_out_b, row(ret_norm_g[i]),
                                        row(post_mix_g[i]), row(pre_ffn_g[i]), g1, sc2, sh2,
                                        wr_cat, br, s_f)
    gate5, idx5 = _route(logits_t, cap)
    gate = gate5.transpose(0, 1, 3, 2, 4).reshape(bsz * n_exp, 1, cap)
    idx = idx5.transpose(0, 1, 3, 2, 4).reshape(bsz * n_exp, 1, cap)

    xs = _dispatch(idx, h_tiles.reshape(bsz, n * ROW_TILE, LANES), n_exp, cap)
    y = _ffn(xs, w_gate[i], w_up[i], w_down[i])
    return _combine(idx, gate, y.reshape(bsz, n_exp, cap * ROW_TILE, LANES), x_mid,
                    row(post_ffn_g[i]), g2)
```

```python
import functools

import jax
import jax.numpy as jnp
from jax import lax
from jax.experimental import pallas as pl
from jax.experimental.pallas import tpu as pltpu

F32 = jnp.float32
BF16 = jnp.bfloat16

GRID_W = 64
RET_HEADS = 4
ROPE_BASE = 10000.0
EC_CAPACITY_FACTOR = 2
RMS_EPS = 1e-6
GN_EPS = 1e-6

LANES = 128
SUBLANES = 8
ROW_TILE = SUBLANES
GROUP = SUBLANES * ROW_TILE
VMEM_CAP = 60000 * 1024

TOKEN_BLOCK = 512
RET_CHUNK = 256
FFN_TILE = 512
SCATTER_UNROLL = 2 * SUBLANES


def _vmem(nbytes):
    return min(int(nbytes), VMEM_CAP)


def _rms(x, g):
    return x * lax.rsqrt(jnp.mean(x * x, axis=-1, keepdims=True) + RMS_EPS) * g


def _silu(x):
    return x * (1.0 / (1.0 + jnp.exp(-x)))


def _rot_half(t):
    return pltpu.roll(t, LANES // 2, axis=1)


def _store_tile_major(ref, lead, val):
    rows = val.shape[0]
    for c in range(ROW_TILE):
        ref[lead + (slice(None), slice(c * SUBLANES, (c + 1) * SUBLANES), slice(None))] = (
            val[:, c * LANES:(c + 1) * LANES].reshape(rows // SUBLANES, SUBLANES, LANES))


def _row_base(r):
    return (r >> 3) * GROUP + (r & (SUBLANES - 1))


def _modulation_kernel(cond_ref, w_ref, b_ref, o_ref):
    s = _silu(cond_ref[...])
    o_ref[...] = jnp.dot(s, w_ref[...], preferred_element_type=F32,
                         precision=lax.Precision.HIGHEST) + b_ref[...]


def _modulation(cond, w_ada, b_ada):
    rows, d = cond.shape
    cols = w_ada.shape[1]
    tile = cols // 4
    return pl.pallas_call(
        _modulation_kernel,
        grid=(cols // tile,),
        in_specs=[
            pl.BlockSpec((rows, d), lambda j: (0, 0)),
            pl.BlockSpec((d, tile), lambda j: (0, j)),
            pl.BlockSpec((1, tile), lambda j: (0, j)),
        ],
        out_specs=pl.BlockSpec((rows, tile), lambda j: (0, j)),
        out_shape=jax.ShapeDtypeStruct((rows, cols), F32),
        compiler_params=pltpu.CompilerParams(
            dimension_semantics=("arbitrary",),
            vmem_limit_bytes=_vmem(3 * d * tile * 4)),
        name="modulation",
    )(cond, w_ada, b_ada.reshape(1, cols))


def _context_kernel(lg_ref, ctx_ref, gpre_ref, sc_ref, sh_ref, wk_ref, wv_ref,
                    cos_ref, sin_ref, sf_ref, sb_ref):
    n = ctx_ref.shape[1]
    hd = LANES
    hc = _rms(ctx_ref[0], gpre_ref[...]) * (1.0 + sc_ref[...]) + sh_ref[...]
    hc = hc.astype(BF16)
    k = jnp.dot(hc, wk_ref[...], preferred_element_type=F32)
    v = jnp.dot(hc, wv_ref[...], preferred_element_type=F32)
    cos = cos_ref[...]
    sin = sin_ref[...]
    pos = lax.broadcasted_iota(jnp.int32, (n, 1), 0).astype(F32)
    for h in range(RET_HEADS):
        kh = k[:, h * hd:(h + 1) * hd]
        kr = (kh * cos + _rot_half(kh) * sin) * (hd ** -0.5)
        vh = v[:, h * hd:(h + 1) * hd].astype(BF16)
        wf = jnp.exp(lg_ref[0, h] * (n - 1.0 - pos))
        wb = jnp.exp(lg_ref[1, h] * pos)
        kf = (kr * wf).T.astype(BF16)
        kb = (kr * wb).T.astype(BF16)
        sf_ref[0, h] = jnp.dot(kf, vh, preferred_element_type=F32)
        sb_ref[0, h] = jnp.dot(kb, vh, preferred_element_type=F32)


def _context_states(lg, ctx, gpre, csc1, csh1, w_in, k_col, cos, sin):
    b, n, d = ctx.shape
    hd = LANES
    rw = RET_HEADS * hd
    st = jax.ShapeDtypeStruct((b, RET_HEADS, hd, hd), F32)
    vec = pl.BlockSpec((1, d), lambda i: (0, 0))
    return pl.pallas_call(
        _context_kernel,
        grid=(b,),
        in_specs=[
            pl.BlockSpec(memory_space=pltpu.SMEM),
            pl.BlockSpec((1, n, d), lambda i: (i, 0, 0)),
            vec, vec, vec,
            pl.BlockSpec((d, rw), lambda i: (0, k_col)),
            pl.BlockSpec((d, rw), lambda i: (0, k_col + 1)),
            pl.BlockSpec((n, hd), lambda i: (0, 0)),
            pl.BlockSpec((n, hd), lambda i: (0, 0)),
        ],
        out_specs=[pl.BlockSpec((1, RET_HEADS, hd, hd), lambda i: (i, 0, 0, 0))] * 2,
        out_shape=[st, st],
        compiler_params=pltpu.CompilerParams(dimension_semantics=("arbitrary",)),
        name="context",
    )(lg, ctx, gpre, csc1, csh1, w_in, w_in, cos, sin)


def _fill_tables(lg_ref, direction, decay_ref, xi_ref, zeta_ref):
    c = decay_ref.shape[1]
    i = lax.broadcasted_iota(jnp.int32, (c, c), 0)
    j = lax.broadcasted_iota(jnp.int32, (c, c), 1)
    col = lax.broadcasted_iota(jnp.int32, (c, 1), 0).astype(F32)
    for h in range(RET_HEADS):
        lg = lg_ref[direction, h]
        if direction == 0:
            rel = (i - j).astype(F32)
            mask = i >= j
            xi_ref[h] = jnp.exp(lg * (col + 1.0))
            zeta_ref[h] = jnp.exp(lg * (c - 1.0 - col))
        else:
            rel = (j - i).astype(F32)
            mask = j > i
            xi_ref[h] = jnp.exp(lg * (c - col))
            zeta_ref[h] = jnp.exp(lg * col)
        decay_ref[h] = jnp.where(mask, jnp.exp(lg * jnp.where(mask, rel, 0.0)), 0.0)


def _retention_block(q_blk, k_blk_f32, v_blk, lg, s_ref, decay_ref, xi_ref, zeta_ref, h,
                     reverse):
    c = decay_ref.shape[1]
    n_chunks = q_blk.shape[0] // c
    outs = [None] * n_chunks
    for ci in (reversed(range(n_chunks)) if reverse else range(n_chunks)):
        rows = slice(ci * c, (ci + 1) * c)
        qh, kh_f32, vh = q_blk[rows], k_blk_f32[rows], v_blk[rows]
        kh = kh_f32.astype(BF16)
        scores = lax.dot_general(qh, kh, (((1,), (1,)), ((), ())), preferred_element_type=F32)
        s_prev = s_ref[h]
        lhs = jnp.concatenate([(scores * decay_ref[h]).astype(BF16),
                               (qh.astype(F32) * xi_ref[h]).astype(BF16)], axis=1)
        rhs = jnp.concatenate([vh, s_prev.astype(BF16)], axis=0)
        outs[ci] = jnp.dot(lhs, rhs, preferred_element_type=F32)
        kz = (kh_f32 * zeta_ref[h]).T.astype(BF16)
        s_ref[h] = jnp.exp(lg * c) * s_prev + jnp.dot(kz, vh, preferred_element_type=F32)
    return outs[0] if n_chunks == 1 else jnp.concatenate(outs, axis=0)


def _mix_bwd_kernel(lg_ref, x_ref, gpre_ref, sc_ref, sh_ref, win_ref, cw_ref, cb_ref,
                    cos_ref, sin_ref, s0_ref,
                    q_ref, k_ref, v_ref, g_ref, yb_ref, conv_ref,
                    s_ref, qs_ref, ks_ref, vs_ref, decay_ref, xi_ref, zeta_ref, *, nb):
    t = x_ref.shape[1]
    hd = LANES
    cw = conv_ref.shape[2]
    rw = RET_HEADS * hd
    step = pl.program_id(0)
    cur = step % 2
    prev = 1 - cur

    @pl.when(step == 0)
    def _():
        s_ref[...] = jnp.zeros(s_ref.shape, F32)
        qs_ref[prev] = jnp.zeros(qs_ref.shape[1:], BF16)
        ks_ref[prev] = jnp.zeros(ks_ref.shape[1:], F32)
        vs_ref[prev] = jnp.zeros(vs_ref.shape[1:], BF16)
        _fill_tables(lg_ref, 1, decay_ref, xi_ref, zeta_ref)

    @pl.when(step % nb == 1)
    def _():
        s_ref[...] = s0_ref[0]

    for h in range(RET_HEADS):
        sl = slice(h * hd, (h + 1) * hd)
        yb_ref[0, :, sl] = _retention_block(
            qs_ref[prev, :, sl], ks_ref[prev, :, sl], vs_ref[prev, :, sl],
            lg_ref[1, h], s_ref, decay_ref, xi_ref, zeta_ref, h, reverse=True)

    hx = _rms(x_ref[0], gpre_ref[...]) * (1.0 + sc_ref[0]) + sh_ref[0]
    proj = jnp.dot(hx.astype(BF16), win_ref[...], preferred_element_type=F32)
    x_b = proj[:, 0:cw]
    x_c = proj[:, cw:2 * cw]
    x_x = proj[:, 2 * cw:3 * cw]
    off = 3 * cw
    q = proj[:, off:off + rw]
    k = proj[:, off + rw:off + 2 * rw]
    v = proj[:, off + 2 * rw:off + 3 * rw]
    g_ref[0] = proj[:, off + 3 * rw:off + 4 * rw]

    u = x_c * x_x
    col = lax.broadcasted_iota(jnp.int32, (t, cw), 0) & (GRID_W - 1)
    u_prev = jnp.where(col == 0, 0.0, pltpu.roll(u, 1, axis=0))
    u_next = jnp.where(col == GRID_W - 1, 0.0, pltpu.roll(u, t - 1, axis=0))
    y = u_prev * cw_ref[0:1, :] + u * cw_ref[1:2, :] + u_next * cw_ref[2:3, :] + cb_ref[...]
    conv_ref[0] = (x_b * y).astype(BF16)

    cos = cos_ref[...]
    sin = sin_ref[...]
    v_bf = v.astype(BF16)
    v_ref[0] = v_bf
    vs_ref[cur] = v_bf
    for h in range(RET_HEADS):
        sl = slice(h * hd, (h + 1) * hd)
        qh = q[:, sl]
        kh = k[:, sl]
        qr = (qh * cos + _rot_half(qh) * sin).astype(BF16)
        kr = (kh * cos + _rot_half(kh) * sin) * (hd ** -0.5)
        q_ref[0, :, sl] = qr
        qs_ref[cur, :, sl] = qr
        k_ref[0, :, sl] = kr.astype(BF16)
        ks_ref[cur, :, sl] = kr


def _mix_bwd(lg, x, gpre, sc1, sh1, w_in, conv_w, conv_b, cos, sin, s0_b):
    b, n, d = x.shape
    t = TOKEN_BLOCK
    nb = n // t
    steps = b * nb
    hd = LANES
    cw = conv_w.shape[1]
    rw = RET_HEADS * hd
    proj = lambda s: jnp.minimum(s, steps - 1)
    retn = lambda s: jnp.maximum(s - 1, 0)
    blk = lambda g: (g // nb, nb - 1 - g % nb, 0)
    vec = pl.BlockSpec((1, d), lambda s: (0, 0))
    bvec = pl.BlockSpec((1, 1, d), lambda s: (proj(s) // nb, 0, 0))
    tok = lambda w: pl.BlockSpec((1, t, w), lambda s: blk(proj(s)))
    rope = pl.BlockSpec((t, hd), lambda s: (nb - 1 - proj(s) % nb, 0))
    out = lambda w, dt: jax.ShapeDtypeStruct((b, n, w), dt)
    return pl.pallas_call(
        functools.partial(_mix_bwd_kernel, nb=nb),
        grid=(steps + 1,),
        in_specs=[
            pl.BlockSpec(memory_space=pltpu.SMEM),
            tok(d), vec, bvec, bvec,
            pl.BlockSpec(w_in.shape, lambda s: (0, 0)),
            pl.BlockSpec(conv_w.shape, lambda s: (0, 0)),
            pl.BlockSpec((1, cw), lambda s: (0, 0)),
            rope, rope,
            pl.BlockSpec((1, RET_HEADS, hd, hd), lambda s: (retn(s) // nb, 0, 0, 0)),
        ],
        out_specs=[tok(rw), tok(rw), tok(rw), tok(rw),
                   pl.BlockSpec((1, t, rw), lambda s: blk(retn(s))), tok(cw)],
        out_shape=[out(rw, BF16), out(rw, BF16), out(rw, BF16), out(rw, F32),
                   out(rw, F32), out(cw, BF16)],
        scratch_shapes=[pltpu.VMEM((RET_HEADS, hd, hd), F32),
                        pltpu.VMEM((2, t, rw), BF16), pltpu.VMEM((2, t, rw), F32),
                        pltpu.VMEM((2, t, rw), BF16),
                        pltpu.VMEM((RET_HEADS, RET_CHUNK, RET_CHUNK), F32),
                        pltpu.VMEM((RET_HEADS, RET_CHUNK, 1), F32),
                        pltpu.VMEM((RET_HEADS, RET_CHUNK, 1), F32)],
        compiler_params=pltpu.CompilerParams(
            dimension_semantics=("arbitrary",),
            vmem_limit_bytes=_vmem(52 * 2 ** 20)),
        name="mix_bwd",
    )(lg, x, gpre, sc1, sh1, w_in, conv_w, conv_b, cos, sin, s0_b)


def _mix_fwd_kernel(lg_ref, x_ref, q_ref, k_ref, v_ref, g_ref, yb_ref, conv_ref,
                    wout_ref, gn_ref, gpost_ref, gffn_ref, g1_ref, sc_ref, sh_ref,
                    wr_ref, br_ref, s0_ref,
                    xmid_ref, h2_ref, logit_ref, s_ref, mix_ref, prev_ref,
                    decay_ref, xi_ref, zeta_ref, *, nb):
    hd = LANES
    cw = conv_ref.shape[2]
    step = pl.program_id(0)

    @pl.when(step == 0)
    def _():
        prev_ref[...] = jnp.zeros(prev_ref.shape, BF16)
        _fill_tables(lg_ref, 0, decay_ref, xi_ref, zeta_ref)

    @pl.when(step % nb == 0)
    def _():
        s_ref[...] = s0_ref[0]

    t = x_ref.shape[1]
    ne = logit_ref.shape[2]
    mix = jnp.dot(prev_ref[...], wout_ref[...], preferred_element_type=F32)

    def finish_rows(r0, r1):
        x_mid = x_ref[0, r0:r1] + g1_ref[0] * _rms(mix[r0:r1], gpost_ref[...])
        xmid_ref[0, r0:r1] = x_mid
        h2 = _rms(x_mid, gffn_ref[...]) * (1.0 + sc_ref[0]) + sh_ref[0]
        h_hi = h2.astype(BF16)
        h_lo = (h2 - h_hi.astype(F32)).astype(BF16)
        nt = (((1,), (1,)), ((), ()))
        p = (lax.dot_general(wr_ref[...], h_hi, nt, preferred_element_type=F32)
             + lax.dot_general(wr_ref[...], h_lo, nt, preferred_element_type=F32))
        logit_ref[0, r0 // LANES] = p[0:ne] + p[ne:2 * ne] + br_ref[...]
        for c in range(ROW_TILE):
            h2_ref[0, r0 // SUBLANES:r1 // SUBLANES, c * SUBLANES:(c + 1) * SUBLANES, :] = (
                h2[:, c * LANES:(c + 1) * LANES].reshape((r1 - r0) // SUBLANES, SUBLANES, LANES))

    mix_ref[:, 0:cw] = conv_ref[0]
    rows = t // RET_HEADS
    for h in range(RET_HEADS):
        sl = slice(h * hd, (h + 1) * hd)
        y = _retention_block(q_ref[0, :, sl], k_ref[0, :, sl].astype(F32), v_ref[0, :, sl],
                             lg_ref[0, h], s_ref, decay_ref, xi_ref, zeta_ref, h, reverse=False)
        finish_rows(h * rows, (h + 1) * rows)
        y = y + yb_ref[0, :, sl]
        mu = jnp.mean(y, axis=-1, keepdims=True)
        yc = y - mu
        var = jnp.mean(yc * yc, axis=-1, keepdims=True)
        yn = yc * lax.rsqrt(var + GN_EPS) * gn_ref[:, sl]
        mix_ref[:, cw + h * hd:cw + (h + 1) * hd] = (
            _silu(g_ref[0, :, sl]) * yn).astype(BF16)
    prev_ref[...] = mix_ref[...]


def _mix_fwd(lg, x, q, k, v, g, yb, conv, w_out, gn, gpost, gffn, g1, sc2, sh2, wr, br, s0_f):
    b, n, d = x.shape
    t = TOKEN_BLOCK
    nb = n // t
    hd = LANES
    cw = conv.shape[2]
    rw = RET_HEADS * hd
    ne = br.shape[0]
    assert t // RET_HEADS == LANES
    steps = b * nb
    ret = lambda s: jnp.minimum(s, steps - 1)
    outp = lambda s: jnp.maximum(s - 1, 0)
    vec = lambda w: pl.BlockSpec((1, w), lambda s: (0, 0))
    bvec = pl.BlockSpec((1, 1, d), lambda s: (outp(s) // nb, 0, 0))
    rtok = lambda w: pl.BlockSpec((1, t, w), lambda s: (ret(s) // nb, ret(s) % nb, 0))
    otok = lambda w: pl.BlockSpec((1, t, w), lambda s: (outp(s) // nb, outp(s) % nb, 0))
    return pl.pallas_call(
        functools.partial(_mix_fwd_kernel, nb=nb),
        grid=(steps + 1,),
        in_specs=[
            pl.BlockSpec(memory_space=pltpu.SMEM),
            otok(d), rtok(rw), rtok(rw), rtok(rw), rtok(rw), rtok(rw), rtok(cw),
            pl.BlockSpec(w_out.shape, lambda s: (0, 0)),
            vec(rw), vec(d), vec(d), bvec, bvec, bvec,
            pl.BlockSpec(wr.shape, lambda s: (0, 0)),
            pl.BlockSpec((ne, 1), lambda s: (0, 0)),
            pl.BlockSpec((1, RET_HEADS, hd, hd), lambda s: (ret(s) // nb, 0, 0, 0)),
        ],
        out_specs=[otok(d),
                   pl.BlockSpec((1, t // SUBLANES, GROUP, LANES),
                                lambda s: (outp(s) // nb, outp(s) % nb, 0, 0)),
                   pl.BlockSpec((1, t // LANES, ne, LANES),
                                lambda s: (outp(s) // nb, outp(s) % nb, 0, 0))],
        out_shape=[jax.ShapeDtypeStruct((b, n, d), F32),
                   jax.ShapeDtypeStruct((b, n // SUBLANES, GROUP, LANES), F32),
                   jax.ShapeDtypeStruct((b, n // LANES, ne, LANES), F32)],
        scratch_shapes=[pltpu.VMEM((RET_HEADS, hd, hd), F32),
                        pltpu.VMEM((t, cw + rw), BF16),
                        pltpu.VMEM((t, cw + rw), BF16),
                        pltpu.VMEM((RET_HEADS, RET_CHUNK, RET_CHUNK), F32),
                        pltpu.VMEM((RET_HEADS, RET_CHUNK, 1), F32),
                        pltpu.VMEM((RET_HEADS, RET_CHUNK, 1), F32)],
        compiler_params=pltpu.CompilerParams(
            dimension_semantics=("arbitrary",),
            vmem_limit_bytes=_vmem(48 * 2 ** 20)),
        name="mix_fwd",
    )(lg, x, q, k, v, g, yb, conv, w_out, gn, gpost, gffn, g1, sc2, sh2, wr, br, s0_f)


def _route_kernel(logit_ref, gate_ref, idx_ref, *, cap):
    nv, ne, _ = logit_ref.shape[1:]
    grp = SUBLANES
    x = logit_ref[0]
    m = jnp.max(x, axis=1, keepdims=True)
    s = jnp.sum(jnp.exp(x - m), axis=1, keepdims=True)
    e0 = pl.multiple_of(pl.program_id(1) * grp, grp)
    xg = logit_ref[0, :, pl.ds(e0, grp), :]
    key = jnp.exp(xg - m) / s

    top = cap // LANES

    def coords(nvreg):
        shape = (nvreg, grp, LANES)
        vi = lax.broadcasted_iota(jnp.int32, shape, 0)
        li = lax.broadcasted_iota(jnp.int32, shape, 2)
        return vi, li, (vi // top) * cap + li * top + vi % top

    def better(pk, pi, k, i):
        return (pk > k) | ((pk == k) & (pi < i))

    def stage(k, i, li, pos, dist, span):
        if dist < top:
            halves = lambda a: a.reshape(a.shape[0] // (2 * dist), 2, dist, grp, LANES)
            k5, i5 = halves(k), halves(i)
            k_lo, k_hi, i_lo, i_hi = k5[:, 0], k5[:, 1], i5[:, 0], i5[:, 1]
            hi_wins = better(k_hi, i_hi, k_lo, i_lo)
            k_b, k_w = jnp.where(hi_wins, k_hi, k_lo), jnp.where(hi_wins, k_lo, k_hi)
            i_b, i_w = jnp.where(hi_wins, i_hi, i_lo), jnp.where(hi_wins, i_lo, i_hi)
            if span is not None:
                desc = (halves(pos)[:, 0] & span) == 0
                k_b, k_w = jnp.where(desc, k_b, k_w), jnp.where(desc, k_w, k_b)
                i_b, i_w = jnp.where(desc, i_b, i_w), jnp.where(desc, i_w, i_b)
            join = lambda lo, hi: jnp.concatenate([lo[:, None], hi[:, None]], axis=1).reshape(k.shape)
            return join(k_b, k_w), join(i_b, i_w)
        sh = dist // top
        low = (li & sh) == 0
        pk = jnp.where(low, pltpu.roll(k, LANES - sh, axis=2), pltpu.roll(k, sh, axis=2))
        pi = jnp.where(low, pltpu.roll(i, LANES - sh, axis=2), pltpu.roll(i, sh, axis=2))
        want_best = low if span is None else low == ((pos & span) == 0)
        take = better(pk, pi, k, i) == want_best
        return jnp.where(take, pk, k), jnp.where(take, pi, i)

    vi, li, pos = coords(nv)
    idx = vi * LANES + li
    span = 2
    while span <= cap:
        dist = span // 2
        while dist >= 1:
            key, idx = stage(key, idx, li, pos, dist, span)
            dist //= 2
        span *= 2
    nblk = nv // top
    while nblk > 1:
        k5 = key.reshape(nblk // 2, 2, top, grp, LANES)
        i5 = idx.reshape(nblk // 2, 2, top, grp, LANES)
        k0, k1, i0, i1 = k5[:, 0], k5[:, 1], i5[:, 0], i5[:, 1]
        take = better(k1, i1, k0, i0)
        nblk //= 2
        key = jnp.where(take, k1, k0).reshape(nblk * top, grp, LANES)
        idx = jnp.where(take, i1, i0).reshape(nblk * top, grp, LANES)
        vi, li, pos = coords(nblk * top)
        dist = cap // 2
        while dist >= 1:
            key, idx = stage(key, idx, li, pos, dist, None if nblk == 1 else cap)
            dist //= 2
    gate_ref[0, 0] = key[0:top]
    idx_ref[0, 0] = _row_base(idx[0:top])


def _route(logits_t, cap):
    b, nv, ne, _ = logits_t.shape
    grp = SUBLANES
    top = cap // LANES
    out_block = pl.BlockSpec((1, 1, top, grp, LANES), lambda i, j: (i, j, 0, 0, 0))
    return pl.pallas_call(
        functools.partial(_route_kernel, cap=cap),
        grid=(b, ne // grp),
        in_specs=[pl.BlockSpec((1, nv, ne, LANES), lambda i, j: (i, 0, 0, 0))],
        out_specs=[out_block, out_block],
        out_shape=[jax.ShapeDtypeStruct((b, ne // grp, top, grp, LANES), F32),
                   jax.ShapeDtypeStruct((b, ne // grp, top, grp, LANES), jnp.int32)],
        compiler_params=pltpu.CompilerParams(dimension_semantics=("arbitrary", "arbitrary")),
        name="route",
    )(logits_t)


BF16_ROWS = 2 * SUBLANES
GATHER_STAGES = 4


def _dispatch_kernel(idx_ref, h_ref, xs_ref, *stage_refs):
    cap = idx_ref.shape[2]
    per_iter = BF16_ROWS * len(stage_refs)

    def body(r, carry):
        for s, stage_ref in enumerate(stage_refs):
            for u in range(BF16_ROWS):
                src = idx_ref[0, 0, r * per_iter + s * BF16_ROWS + u]
                dst = (u // SUBLANES) * GROUP + u % SUBLANES
                stage_ref[pl.ds(dst, ROW_TILE, stride=SUBLANES), :] = (
                    h_ref[0, pl.ds(src, ROW_TILE, stride=SUBLANES), :])
        for s, stage_ref in enumerate(stage_refs):
            row0 = pl.multiple_of(r * per_iter + s * BF16_ROWS, BF16_ROWS)
            for c in range(ROW_TILE):
                lo = stage_ref[c * SUBLANES:(c + 1) * SUBLANES, :]
                hi = stage_ref[GROUP + c * SUBLANES:GROUP + (c + 1) * SUBLANES, :]
                xs_ref[0, 0, pl.ds(row0, BF16_ROWS), c * LANES:(c + 1) * LANES] = (
                    jnp.concatenate([lo, hi], axis=0).astype(BF16))
        return carry

    lax.fori_loop(0, cap // per_iter, body, 0)


def _dispatch(idx, h_rows, n_exp, cap):
    b = h_rows.shape[0]
    rows = h_rows.shape[1]
    d = ROW_TILE * LANES
    return pl.pallas_call(
        _dispatch_kernel,
        grid=(b, n_exp),
        in_specs=[
            pl.BlockSpec((1, 1, cap), lambda i, e: (i * n_exp + e, 0, 0),
                         memory_space=pltpu.SMEM),
            pl.BlockSpec((1, rows, LANES), lambda i, e: (i, 0, 0),
                         pipeline_mode=pl.Buffered(1)),
        ],
        out_specs=pl.BlockSpec((1, 1, cap, d), lambda i, e: (i, e, 0, 0)),
        out_shape=jax.ShapeDtypeStruct((b, n_exp, cap, d), BF16),
        scratch_shapes=[pltpu.VMEM((2 * GROUP, LANES), F32)] * GATHER_STAGES,
        compiler_params=pltpu.CompilerParams(
            dimension_semantics=("arbitrary", "arbitrary"),
            vmem_limit_bytes=_vmem(rows * LANES * 4 + 4 * cap * d * 2)),
        name="dispatch",
    )(idx, h_rows)


def _ffn_kernel(xs_ref, wg_ref, wu_ref, wd_ref, y_ref):
    nb = xs_ref.shape[0]
    tf = wg_ref.shape[2]

    @pl.when(pl.program_id(1) == 0)
    def _():
        y_ref[...] = jnp.zeros(y_ref.shape, F32)

    wgu = jnp.concatenate([wg_ref[0].astype(BF16), wu_ref[0].astype(BF16)], axis=1)
    wd = wd_ref[0].astype(BF16)
    for b in range(nb):
        au = jnp.dot(xs_ref[b, 0], wgu, preferred_element_type=F32)
        hmid = (_silu(au[:, 0:tf]) * au[:, tf:2 * tf]).astype(BF16)
        contrib = jnp.dot(hmid, wd, preferred_element_type=F32)
        rows = contrib.shape[0]
        for c in range(ROW_TILE):
            y_ref[b, 0, :, c * SUBLANES:(c + 1) * SUBLANES, :] += (
                contrib[:, c * LANES:(c + 1) * LANES].reshape(rows // SUBLANES, SUBLANES, LANES))


def _ffn(xs, w_gate, w_up, w_down):
    b, n_exp, cap, d = xs.shape
    dff = w_gate.shape[2]
    tf = FFN_TILE
    groups = cap // SUBLANES
    return pl.pallas_call(
        _ffn_kernel,
        grid=(n_exp, dff // tf),
        in_specs=[
            pl.BlockSpec((b, 1, cap, d), lambda e, f: (0, e, 0, 0)),
            pl.BlockSpec((1, d, tf), lambda e, f: (e, 0, f)),
            pl.BlockSpec((1, d, tf), lambda e, f: (e, 0, f)),
            pl.BlockSpec((1, tf, d), lambda e, f: (e, f, 0)),
        ],
        out_specs=pl.BlockSpec((b, 1, groups, GROUP, LANES), lambda e, f: (0, e, 0, 0, 0)),
        out_shape=jax.ShapeDtypeStruct((b, n_exp, groups, GROUP, LANES), F32),
        compiler_params=pltpu.CompilerParams(
            dimension_semantics=("arbitrary", "arbitrary"),
            vmem_limit_bytes=_vmem(56 * 2 ** 20)),
        name="ffn",
    )(xs, w_gate, w_up, w_down)


def _combine_kernel(idx_ref, gate_ref, y_ref, xmid_ref, gpost_ref, g2_ref, o_ref, acc_ref,
                    *, n_exp):
    cap = idx_ref.shape[2]
    t = xmid_ref.shape[1]
    step = pl.program_id(1)

    @pl.when(step == 0)
    def _():
        zrows = GROUP

        def zero(r, carry):
            acc_ref[pl.ds(pl.multiple_of(r * zrows, zrows), zrows), :] = jnp.zeros(
                (zrows, LANES), F32)
            return carry

        lax.fori_loop(0, acc_ref.shape[0] // zrows, zero, 0)

    @pl.when(step < n_exp)
    def _():
        def body(r, carry):
            offs, vals = [], []
            for u in range(SCATTER_UNROLL):
                dst = idx_ref[0, 0, r * SCATTER_UNROLL + u]
                gate = gate_ref[0, 0, r * SCATTER_UNROLL + u]
                src = (r * (SCATTER_UNROLL // SUBLANES) + u // SUBLANES) * GROUP + u % SUBLANES
                offs.append(dst)
                vals.append(acc_ref[pl.ds(dst, ROW_TILE, stride=SUBLANES), :]
                            + gate * y_ref[0, 0, pl.ds(src, ROW_TILE, stride=SUBLANES), :])
            for dst, val in zip(offs, vals):
                acc_ref[pl.ds(dst, ROW_TILE, stride=SUBLANES), :] = val
            return carry

        lax.fori_loop(0, cap // SCATTER_UNROLL, body, 0)

    @pl.when(step >= n_exp)
    def _():
        rows = t * ROW_TILE
        start = pl.multiple_of((step - n_exp) * rows, rows)
        blk = acc_ref[pl.ds(start, rows), :].reshape(t // SUBLANES, GROUP, LANES)
        ffn = jnp.concatenate(
            [blk[:, c * SUBLANES:(c + 1) * SUBLANES, :].reshape(t, LANES)
             for c in range(ROW_TILE)], axis=-1)
        o_ref[0] = xmid_ref[0] + g2_ref[0] * _rms(ffn, gpost_ref[...])


def _combine(idx, gate, y, x_mid, gpost, g2):
    b, n_exp, rows, _ = y.shape
    n, d = x_mid.shape[1:]
    cap = rows // ROW_TILE
    t = TOKEN_BLOCK
    last = n_exp - 1
    tok_block = pl.BlockSpec((1, t, d), lambda i, s: (i, jnp.maximum(s - n_exp, 0), 0))
    per_expert = pl.BlockSpec((1, 1, cap), lambda i, s: (i * n_exp + jnp.minimum(s, last), 0, 0),
                              memory_space=pltpu.SMEM)
    return pl.pallas_call(
        functools.partial(_combine_kernel, n_exp=n_exp),
        grid=(b, n_exp + n // t),
        in_specs=[
            per_expert, per_expert,
            pl.BlockSpec((1, 1, rows, LANES), lambda i, s: (i, jnp.minimum(s, last), 0, 0)),
            tok_block,
            pl.BlockSpec((1, d), lambda i, s: (0, 0)),
            pl.BlockSpec((1, 1, d), lambda i, s: (i, 0, 0)),
        ],
        out_specs=tok_block,
        out_shape=jax.ShapeDtypeStruct((b, n, d), F32),
        scratch_shapes=[pltpu.VMEM((n * ROW_TILE, LANES), F32)],
        compiler_params=pltpu.CompilerParams(
            dimension_semantics=("arbitrary", "arbitrary"),
            vmem_limit_bytes=_vmem(n * d * 4 + 2 * rows * LANES * 4 + 8 * t * d * 4)),
        name="combine",
    )(idx, gate, y, x_mid, gpost, g2)


def _rope_tables(n_pos):
    half = LANES // 2
    inv = 1.0 / (ROPE_BASE ** (jnp.arange(half, dtype=F32) / half))
    ang_a = (jnp.arange(n_pos // LANES, dtype=F32) * LANES)[:, None] * inv[None, :]
    ang_b = jnp.arange(LANES, dtype=F32)[:, None] * inv[None, :]
    ca, sa = jnp.cos(ang_a)[:, None, :], jnp.sin(ang_a)[:, None, :]
    cb, sb = jnp.cos(ang_b)[None, :, :], jnp.sin(ang_b)[None, :, :]
    cos = (ca * cb - sa * sb).reshape(n_pos, half)
    sin = (sa * cb + ca * sb).reshape(n_pos, half)
    return jnp.concatenate([cos, cos], axis=-1), jnp.concatenate([-sin, sin], axis=-1)


def kernel(x, c, ctx, c_ctx, w_ada, b_ada, pre_mix_g, post_mix_g, pre_ffn_g, post_ffn_g,
           w_in, conv_w, conv_b, ret_decay_logit, ret_norm_g, w_out,
           w_router, b_router, w_gate, w_up, w_down):
    bsz, n, d = x.shape
    ctx_len = ctx.shape[1]
    depth = w_ada.shape[0]
    n_exp = w_router.shape[2]
    cw = conv_w.shape[2]
    rw = RET_HEADS * LANES
    cap = max(1, EC_CAPACITY_FACTOR * n // n_exp)
    assert depth == 1, "only the single-layer block is implemented"
    assert d == ROW_TILE * LANES and n % TOKEN_BLOCK == 0 and TOKEN_BLOCK % LANES == 0
    assert n % GRID_W == 0 and TOKEN_BLOCK % GRID_W == 0 and ctx_len % SUBLANES == 0
    assert n_exp % SUBLANES == 0 and cap % LANES == 0 and (n // LANES) & (n // LANES - 1) == 0
    assert w_in.shape[2] == 3 * cw + 4 * rw

    assert (ctx_len + n) % LANES == 0
    cos_all, sin_all = _rope_tables(ctx_len + n)
    cos_c, sin_c = cos_all[:ctx_len], sin_all[:ctx_len]
    cos_x, sin_x = cos_all[ctx_len:], sin_all[ctx_len:]
    i = 0

    pad = -(bsz + 1) % SUBLANES
    cond = jnp.concatenate([c, c_ctx[None, :], jnp.zeros((pad, d), F32)], axis=0)
    mod = _modulation(cond, pltpu.with_memory_space_constraint(w_ada[i], pltpu.HBM), b_ada[i])
    sh1, sc1, g1, sh2, sc2, g2 = [mod[:bsz, j * d:(j + 1) * d].reshape(bsz, 1, d) for j in range(6)]
    csh1 = mod[bsz:bsz + 1, 0:d]
    csc1 = mod[bsz:bsz + 1, d:2 * d]

    lg = jax.nn.log_sigmoid(ret_decay_logit[i].astype(F32))
    w_in_b = w_in[i].astype(BF16)
    w_out_b = w_out[i].astype(BF16)
    k_off = 3 * cw + rw
    row = lambda a: a.reshape(1, -1)

    assert k_off % rw == 0
    s_f, s_b = _context_states(lg, ctx, row(pre_mix_g[i]), csc1, csh1, w_in_b, k_off // rw,
                               cos_c, sin_c)

    q, k, v, g, yb, conv = _mix_bwd(lg, x, row(pre_mix_g[i]), sc1, sh1, w_in_b,
                                    conv_w[i], row(conv_b[i]), cos_x, sin_x, s_b)

    wr = w_router[i].astype(F32).T
    wr_hi = wr.astype(BF16)
    wr_lo = (wr - wr_hi.astype(F32)).astype(BF16)
    wr_cat = jnp.concatenate([wr_hi, wr_lo], axis=0)
    br = b_router[i].astype(F32).reshape(n_exp, 1)
    x_mid, h_tiles, logits_t = _mix_fwd(lg, x, q, k, v, g, yb, conv, w_out_b, row(ret_norm_g[i]),
                                        row(post_mix_g[i]), row(pre_ffn_g[i]), g1, sc2, sh2,
                                        wr_cat, br, s_f)
    gate5, idx5 = _route(logits_t, cap)
    gate = gate5.transpose(0, 1, 3, 2, 4).reshape(bsz * n_exp, 1, cap)
    idx = idx5.transpose(0, 1, 3, 2, 4).reshape(bsz * n_exp, 1, cap)

    xs = _dispatch(idx, h_tiles.reshape(bsz, n * ROW_TILE, LANES), n_exp, cap)
    y = _ffn(xs, w_gate[i], w_up[i], w_down[i])
    return _combine(idx, gate, y.reshape(bsz, n_exp, cap * ROW_TILE, LANES), x_mid,
                    row(post_ffn_g[i]), g2)
```

```python
import functools

import jax
import jax.numpy as jnp
from jax import lax
from jax.experimental import pallas as pl
from jax.experimental.pallas import tpu as pltpu

F32 = jnp.float32
BF16 = jnp.bfloat16

GRID_W = 64
RET_HEADS = 4
ROPE_BASE = 10000.0
EC_CAPACITY_FACTOR = 2
RMS_EPS = 1e-6
GN_EPS = 1e-6

LANES = 128
SUBLANES = 8
ROW_TILE = SUBLANES
GROUP = SUBLANES * ROW_TILE
VMEM_CAP = 60000 * 1024

TOKEN_BLOCK = 512
PROJ_BLOCK = 1024
RET_CHUNK = 256
FFN_TILE = 512
SCATTER_UNROLL = 2 * SUBLANES


def _vmem(nbytes):
    return min(int(nbytes), VMEM_CAP)


def _rms(x, g):
    return x * lax.rsqrt(jnp.mean(x * x, axis=-1, keepdims=True) + RMS_EPS) * g


def _silu(x):
    return x * (1.0 / (1.0 + jnp.exp(-x)))


def _rot_half(t):
    return pltpu.roll(t, LANES // 2, axis=1)


def _store_tile_major(ref, lead, val):
    rows = val.shape[0]
    for c in range(ROW_TILE):
        ref[lead + (slice(None), slice(c * SUBLANES, (c + 1) * SUBLANES), slice(None))] = (
            val[:, c * LANES:(c + 1) * LANES].reshape(rows // SUBLANES, SUBLANES, LANES))


def _row_base(r):
    return (r >> 3) * GROUP + (r & (SUBLANES - 1))


def _modulation_kernel(cond_ref, w_ref, b_ref, o_ref):
    s = _silu(cond_ref[...])
    o_ref[...] = jnp.dot(s, w_ref[...], preferred_element_type=F32,
                         precision=lax.Precision.HIGHEST) + b_ref[...]


def _modulation(cond, w_ada, b_ada):
    rows, d = cond.shape
    cols = w_ada.shape[1]
    tile = cols // 4
    return pl.pallas_call(
        _modulation_kernel,
        grid=(cols // tile,),
        in_specs=[
            pl.BlockSpec((rows, d), lambda j: (0, 0)),
            pl.BlockSpec((d, tile), lambda j: (0, j)),
            pl.BlockSpec((1, tile), lambda j: (0, j)),
        ],
        out_specs=pl.BlockSpec((rows, tile), lambda j: (0, j)),
        out_shape=jax.ShapeDtypeStruct((rows, cols), F32),
        compiler_params=pltpu.CompilerParams(
            dimension_semantics=("arbitrary",),
            vmem_limit_bytes=_vmem(3 * d * tile * 4)),
        name="modulation",
    )(cond, w_ada, b_ada.reshape(1, cols))


def _context_kernel(lg_ref, ctx_ref, gpre_ref, sc_ref, sh_ref, wk_ref, wv_ref,
                    cos_ref, sin_ref, sf_ref, sb_ref):
    n = ctx_ref.shape[1]
    hd = LANES
    hc = _rms(ctx_ref[0], gpre_ref[...]) * (1.0 + sc_ref[...]) + sh_ref[...]
    hc = hc.astype(BF16)
    k = jnp.dot(hc, wk_ref[...], preferred_element_type=F32)
    v = jnp.dot(hc, wv_ref[...], preferred_element_type=F32)
    cos = cos_ref[...]
    sin = sin_ref[...]
    pos = lax.broadcasted_iota(jnp.int32, (n, 1), 0).astype(F32)
    for h in range(RET_HEADS):
        kh = k[:, h * hd:(h + 1) * hd]
        kr = (kh * cos + _rot_half(kh) * sin) * (hd ** -0.5)
        vh = v[:, h * hd:(h + 1) * hd].astype(BF16)
        wf = jnp.exp(lg_ref[0, h] * (n - 1.0 - pos))
        wb = jnp.exp(lg_ref[1, h] * pos)
        kf = (kr * wf).T.astype(BF16)
        kb = (kr * wb).T.astype(BF16)
        sf_ref[0, h] = jnp.dot(kf, vh, preferred_element_type=F32)
        sb_ref[0, h] = jnp.dot(kb, vh, preferred_element_type=F32)


def _context_states(lg, ctx, gpre, csc1, csh1, w_in, k_col, cos, sin):
    b, n, d = ctx.shape
    hd = LANES
    rw = RET_HEADS * hd
    st = jax.ShapeDtypeStruct((b, RET_HEADS, hd, hd), F32)
    vec = pl.BlockSpec((1, d), lambda i: (0, 0))
    return pl.pallas_call(
        _context_kernel,
        grid=(b,),
        in_specs=[
            pl.BlockSpec(memory_space=pltpu.SMEM),
            pl.BlockSpec((1, n, d), lambda i: (i, 0, 0)),
            vec, vec, vec,
            pl.BlockSpec((d, rw), lambda i: (0, k_col)),
            pl.BlockSpec((d, rw), lambda i: (0, k_col + 1)),
            pl.BlockSpec((n, hd), lambda i: (0, 0)),
            pl.BlockSpec((n, hd), lambda i: (0, 0)),
        ],
        out_specs=[pl.BlockSpec((1, RET_HEADS, hd, hd), lambda i: (i, 0, 0, 0))] * 2,
        out_shape=[st, st],
        compiler_params=pltpu.CompilerParams(dimension_semantics=("arbitrary",)),
        name="context",
    )(lg, ctx, gpre, csc1, csh1, w_in, w_in, cos, sin)


def _fill_tables(lg_ref, direction, decay_ref, xi_ref, zeta_ref):
    c = decay_ref.shape[1]
    i = lax.broadcasted_iota(jnp.int32, (c, c), 0)
    j = lax.broadcasted_iota(jnp.int32, (c, c), 1)
    col = lax.broadcasted_iota(jnp.int32, (c, 1), 0).astype(F32)
    for h in range(RET_HEADS):
        lg = lg_ref[direction, h]
        if direction == 0:
            rel = (i - j).astype(F32)
            mask = i >= j
            xi_ref[h] = jnp.exp(lg * (col + 1.0))
            zeta_ref[h] = jnp.exp(lg * (c - 1.0 - col))
        else:
            rel = (j - i).astype(F32)
            mask = j > i
            xi_ref[h] = jnp.exp(lg * (c - col))
            zeta_ref[h] = jnp.exp(lg * col)
        decay_ref[h] = jnp.where(mask, jnp.exp(lg * jnp.where(mask, rel, 0.0)), 0.0)


def _retention_block(q_blk, k_blk_f32, v_blk, lg, s_ref, decay_ref, xi_ref, zeta_ref, h,
                     reverse):
    c = decay_ref.shape[1]
    n_chunks = q_blk.shape[0] // c
    outs = [None] * n_chunks
    for ci in (reversed(range(n_chunks)) if reverse else range(n_chunks)):
        rows = slice(ci * c, (ci + 1) * c)
        qh, kh_f32, vh = q_blk[rows], k_blk_f32[rows], v_blk[rows]
        kh = kh_f32.astype(BF16)
        scores = lax.dot_general(qh, kh, (((1,), (1,)), ((), ())), preferred_element_type=F32)
        s_prev = s_ref[h]
        lhs = jnp.concatenate([(scores * decay_ref[h]).astype(BF16),
                               (qh.astype(F32) * xi_ref[h]).astype(BF16)], axis=1)
        rhs = jnp.concatenate([vh, s_prev.astype(BF16)], axis=0)
        outs[ci] = jnp.dot(lhs, rhs, preferred_element_type=F32)
        kz = (kh_f32 * zeta_ref[h]).T.astype(BF16)
        s_ref[h] = jnp.exp(lg * c) * s_prev + jnp.dot(kz, vh, preferred_element_type=F32)
    return outs[0] if n_chunks == 1 else jnp.concatenate(outs, axis=0)


def _mix_bwd_kernel(lg_ref, x_ref, gpre_ref, sc_ref, sh_ref, win_ref, cw_ref, cb_ref,
                    cos_ref, sin_ref, s0_ref,
                    q_ref, k_ref, v_ref, g_ref, yb_ref, conv_ref,
                    s_ref, decay_ref, xi_ref, zeta_ref):
    t = x_ref.shape[1]
    hd = LANES
    cw = conv_ref.shape[2]
    rw = RET_HEADS * hd

    @pl.when((pl.program_id(0) == 0) & (pl.program_id(1) == 0))
    def _():
        _fill_tables(lg_ref, 1, decay_ref, xi_ref, zeta_ref)

    @pl.when(pl.program_id(1) == 0)
    def _():
        s_ref[...] = s0_ref[0]

    hx = _rms(x_ref[0], gpre_ref[...]) * (1.0 + sc_ref[0]) + sh_ref[0]
    proj = jnp.dot(hx.astype(BF16), win_ref[...], preferred_element_type=F32)
    x_b = proj[:, 0:cw]
    x_c = proj[:, cw:2 * cw]
    x_x = proj[:, 2 * cw:3 * cw]
    off = 3 * cw
    q = proj[:, off:off + rw]
    k = proj[:, off + rw:off + 2 * rw]
    v = proj[:, off + 2 * rw:off + 3 * rw]
    g_ref[0] = proj[:, off + 3 * rw:off + 4 * rw]

    u = x_c * x_x
    col = lax.broadcasted_iota(jnp.int32, (t, cw), 0) & (GRID_W - 1)
    u_prev = jnp.where(col == 0, 0.0, pltpu.roll(u, 1, axis=0))
    u_next = jnp.where(col == GRID_W - 1, 0.0, pltpu.roll(u, t - 1, axis=0))
    y = u_prev * cw_ref[0:1, :] + u * cw_ref[1:2, :] + u_next * cw_ref[2:3, :] + cb_ref[...]
    conv_ref[0] = (x_b * y).astype(BF16)

    cos = cos_ref[...]
    sin = sin_ref[...]
    v_bf = v.astype(BF16)
    v_ref[0] = v_bf
    for h in range(RET_HEADS):
        sl = slice(h * hd, (h + 1) * hd)
        qh = q[:, sl]
        kh = k[:, sl]
        qr = (qh * cos + _rot_half(qh) * sin).astype(BF16)
        kr = (kh * cos + _rot_half(kh) * sin) * (hd ** -0.5)
        q_ref[0, :, sl] = qr
        k_ref[0, :, sl] = kr.astype(BF16)
        yb_ref[0, :, sl] = _retention_block(qr, kr, v_bf[:, sl], lg_ref[1, h], s_ref,
                                            decay_ref, xi_ref, zeta_ref, h, reverse=True)


def _mix_bwd(lg, x, gpre, sc1, sh1, w_in, conv_w, conv_b, cos, sin, s0_b):
    b, n, d = x.shape
    t = PROJ_BLOCK
    nb = n // t
    hd = LANES
    cw = conv_w.shape[1]
    rw = RET_HEADS * hd
    rev = lambda i, j: (i, nb - 1 - j, 0)
    vec = pl.BlockSpec((1, d), lambda i, j: (0, 0))
    bvec = pl.BlockSpec((1, 1, d), lambda i, j: (i, 0, 0))
    tok = lambda w: pl.BlockSpec((1, t, w), rev)
    rope = pl.BlockSpec((t, hd), lambda i, j: (nb - 1 - j, 0))
    out = lambda w, dt: jax.ShapeDtypeStruct((b, n, w), dt)
    return pl.pallas_call(
        _mix_bwd_kernel,
        grid=(b, nb),
        in_specs=[
            pl.BlockSpec(memory_space=pltpu.SMEM),
            tok(d), vec, bvec, bvec,
            pl.BlockSpec(w_in.shape, lambda i, j: (0, 0), pipeline_mode=pl.Buffered(1)),
            pl.BlockSpec(conv_w.shape, lambda i, j: (0, 0)),
            pl.BlockSpec((1, cw), lambda i, j: (0, 0)),
            rope, rope,
            pl.BlockSpec((1, RET_HEADS, hd, hd), lambda i, j: (i, 0, 0, 0)),
        ],
        out_specs=[tok(rw), tok(rw), tok(rw), tok(rw), tok(rw), tok(cw)],
        out_shape=[out(rw, BF16), out(rw, BF16), out(rw, BF16), out(rw, F32),
                   out(rw, F32), out(cw, BF16)],
        scratch_shapes=[pltpu.VMEM((RET_HEADS, hd, hd), F32),
                        pltpu.VMEM((RET_HEADS, RET_CHUNK, RET_CHUNK), F32),
                        pltpu.VMEM((RET_HEADS, RET_CHUNK, 1), F32),
                        pltpu.VMEM((RET_HEADS, RET_CHUNK, 1), F32)],
        compiler_params=pltpu.CompilerParams(
            dimension_semantics=("arbitrary", "arbitrary"),
            vmem_limit_bytes=_vmem(56 * 2 ** 20)),
        name="mix_bwd",
    )(lg, x, gpre, sc1, sh1, w_in, conv_w, conv_b, cos, sin, s0_b)


def _mix_fwd_kernel(lg_ref, x_ref, q_ref, k_ref, v_ref, g_ref, yb_ref, conv_ref,
                    wout_ref, gn_ref, gpost_ref, gffn_ref, g1_ref, sc_ref, sh_ref,
                    wr_ref, br_ref, s0_ref,
                    xmid_ref, h2_ref, logit_ref, s_ref, mix_ref, prev_ref,
                    decay_ref, xi_ref, zeta_ref, *, nb):
    hd = LANES
    cw = conv_ref.shape[2]
    step = pl.program_id(0)

    @pl.when(step == 0)
    def _():
        prev_ref[...] = jnp.zeros(prev_ref.shape, BF16)
        _fill_tables(lg_ref, 0, decay_ref, xi_ref, zeta_ref)

    @pl.when(step % nb == 0)
    def _():
        s_ref[...] = s0_ref[0]

    t = x_ref.shape[1]
    ne = logit_ref.shape[2]
    mix = jnp.dot(prev_ref[...], wout_ref[...], preferred_element_type=F32)

    def finish_rows(r0, r1):
        x_mid = x_ref[0, r0:r1] + g1_ref[0] * _rms(mix[r0:r1], gpost_ref[...])
        xmid_ref[0, r0:r1] = x_mid
        h2 = _rms(x_mid, gffn_ref[...]) * (1.0 + sc_ref[0]) + sh_ref[0]
        h_hi = h2.astype(BF16)
        h_lo = (h2 - h_hi.astype(F32)).astype(BF16)
        nt = (((1,), (1,)), ((), ()))
        p = (lax.dot_general(wr_ref[...], h_hi, nt, preferred_element_type=F32)
             + lax.dot_general(wr_ref[...], h_lo, nt, preferred_element_type=F32))
        logit_ref[0, r0 // LANES] = p[0:ne] + p[ne:2 * ne] + br_ref[...]
        for c in range(ROW_TILE):
            h2_ref[0, r0 // SUBLANES:r1 // SUBLANES, c * SUBLANES:(c + 1) * SUBLANES, :] = (
                h2[:, c * LANES:(c + 1) * LANES].reshape((r1 - r0) // SUBLANES, SUBLANES, LANES))

    mix_ref[:, 0:cw] = conv_ref[0]
    rows = t // RET_HEADS
    for h in range(RET_HEADS):
        sl = slice(h * hd, (h + 1) * hd)
        y = _retention_block(q_ref[0, :, sl], k_ref[0, :, sl].astype(F32), v_ref[0, :, sl],
                             lg_ref[0, h], s_ref, decay_ref, xi_ref, zeta_ref, h, reverse=False)
        finish_rows(h * rows, (h + 1) * rows)
        y = y + yb_ref[0, :, sl]
        mu = jnp.mean(y, axis=-1, keepdims=True)
        yc = y - mu
        var = jnp.mean(yc * yc, axis=-1, keepdims=True)
        yn = yc * lax.rsqrt(var + GN_EPS) * gn_ref[:, sl]
        mix_ref[:, cw + h * hd:cw + (h + 1) * hd] = (
            _silu(g_ref[0, :, sl]) * yn).astype(BF16)
    prev_ref[...] = mix_ref[...]


def _mix_fwd(lg, x, q, k, v, g, yb, conv, w_out, gn, gpost, gffn, g1, sc2, sh2, wr, br, s0_f):
    b, n, d = x.shape
    t = TOKEN_BLOCK
    nb = n // t
    hd = LANES
    cw = conv.shape[2]
    rw = RET_HEADS * hd
    ne = br.shape[0]
    assert t // RET_HEADS == LANES
    steps = b * nb
    ret = lambda s: jnp.minimum(s, steps - 1)
    outp = lambda s: jnp.maximum(s - 1, 0)
    vec = lambda w: pl.BlockSpec((1, w), lambda s: (0, 0))
    bvec = pl.BlockSpec((1, 1, d), lambda s: (outp(s) // nb, 0, 0))
    rtok = lambda w: pl.BlockSpec((1, t, w), lambda s: (ret(s) // nb, ret(s) % nb, 0))
    otok = lambda w: pl.BlockSpec((1, t, w), lambda s: (outp(s) // nb, outp(s) % nb, 0))
    return pl.pallas_call(
        functools.partial(_mix_fwd_kernel, nb=nb),
        grid=(steps + 1,),
        in_specs=[
            pl.BlockSpec(memory_space=pltpu.SMEM),
            otok(d), rtok(rw), rtok(rw), rtok(rw), rtok(rw), rtok(rw), rtok(cw),
            pl.BlockSpec(w_out.shape, lambda s: (0, 0)),
            vec(rw), vec(d), vec(d), bvec, bvec, bvec,
            pl.BlockSpec(wr.shape, lambda s: (0, 0)),
            pl.BlockSpec((ne, 1), lambda s: (0, 0)),
            pl.BlockSpec((1, RET_HEADS, hd, hd), lambda s: (ret(s) // nb, 0, 0, 0)),
        ],
        out_specs=[otok(d),
                   pl.BlockSpec((1, t // SUBLANES, GROUP, LANES),
                                lambda s: (outp(s) // nb, outp(s) % nb, 0, 0)),
                   pl.BlockSpec((1, t // LANES, ne, LANES),
                                lambda s: (outp(s) // nb, outp(s) % nb, 0, 0))],
        out_shape=[jax.ShapeDtypeStruct((b, n, d), F32),
                   jax.ShapeDtypeStruct((b, n // SUBLANES, GROUP, LANES), F32),
                   jax.ShapeDtypeStruct((b, n // LANES, ne, LANES), F32)],
        scratch_shapes=[pltpu.VMEM((RET_HEADS, hd, hd), F32),
                        pltpu.VMEM((t, cw + rw), BF16),
                        pltpu.VMEM((t, cw + rw), BF16),
                        pltpu.VMEM((RET_HEADS, RET_CHUNK, RET_CHUNK), F32),
                        pltpu.VMEM((RET_HEADS, RET_CHUNK, 1), F32),
                        pltpu.VMEM((RET_HEADS, RET_CHUNK, 1), F32)],
        compiler_params=pltpu.CompilerParams(
            dimension_semantics=("arbitrary",),
            vmem_limit_bytes=_vmem(48 * 2 ** 20)),
        name="mix_fwd",
    )(lg, x, q, k, v, g, yb, conv, w_out, gn, gpost, gffn, g1, sc2, sh2, wr, br, s0_f)


def _route_kernel(logit_ref, gate_ref, idx_ref, *, cap):
    nv, ne, _ = logit_ref.shape[1:]
    grp = SUBLANES
    x = logit_ref[0]
    m = jnp.max(x, axis=1, keepdims=True)
    s = jnp.sum(jnp.exp(x - m), axis=1, keepdims=True)
    e0 = pl.multiple_of(pl.program_id(1) * grp, grp)
    xg = logit_ref[0, :, pl.ds(e0, grp), :]
    key = jnp.exp(xg - m) / s

    top = cap // LANES

    def coords(nvreg):
        shape = (nvreg, grp, LANES)
        vi = lax.broadcasted_iota(jnp.int32, shape, 0)
        li = lax.broadcasted_iota(jnp.int32, shape, 2)
        return vi, li, (vi // top) * cap + li * top + vi % top

    def better(pk, pi, k, i):
        return (pk > k) | ((pk == k) & (pi < i))

    def stage(k, i, li, pos, dist, span):
        if dist < top:
            halves = lambda a: a.reshape(a.shape[0] // (2 * dist), 2, dist, grp, LANES)
            k5, i5 = halves(k), halves(i)
            k_lo, k_hi, i_lo, i_hi = k5[:, 0], k5[:, 1], i5[:, 0], i5[:, 1]
            hi_wins = better(k_hi, i_hi, k_lo, i_lo)
            k_b, k_w = jnp.where(hi_wins, k_hi, k_lo), jnp.where(hi_wins, k_lo, k_hi)
            i_b, i_w = jnp.where(hi_wins, i_hi, i_lo), jnp.where(hi_wins, i_lo, i_hi)
            if span is not None:
                desc = (halves(pos)[:, 0] & span) == 0
                k_b, k_w = jnp.where(desc, k_b, k_w), jnp.where(desc, k_w, k_b)
                i_b, i_w = jnp.where(desc, i_b, i_w), jnp.where(desc, i_w, i_b)
            join = lambda lo, hi: jnp.concatenate([lo[:, None], hi[:, None]], axis=1).reshape(k.shape)
            return join(k_b, k_w), join(i_b, i_w)
        sh = dist // top
        low = (li & sh) == 0
        pk = jnp.where(low, pltpu.roll(k, LANES - sh, axis=2), pltpu.roll(k, sh, axis=2))
        pi = jnp.where(low, pltpu.roll(i, LANES - sh, axis=2), pltpu.roll(i, sh, axis=2))
        want_best = low if span is None else low == ((pos & span) == 0)
        take = better(pk, pi, k, i) == want_best
        return jnp.where(take, pk, k), jnp.where(take, pi, i)

    vi, li, pos = coords(nv)
    idx = vi * LANES + li
    span = 2
    while span <= cap:
        dist = span // 2
        while dist >= 1:
            key, idx = stage(key, idx, li, pos, dist, span)
            dist //= 2
        span *= 2
    nblk = nv // top
    while nblk > 1:
        k5 = key.reshape(nblk // 2, 2, top, grp, LANES)
        i5 = idx.reshape(nblk // 2, 2, top, grp, LANES)
        k0, k1, i0, i1 = k5[:, 0], k5[:, 1], i5[:, 0], i5[:, 1]
        take = better(k1, i1, k0, i0)
        nblk //= 2
        key = jnp.where(take, k1, k0).reshape(nblk * top, grp, LANES)
        idx = jnp.where(take, i1, i0).reshape(nblk * top, grp, LANES)
        vi, li, pos = coords(nblk * top)
        dist = cap // 2
        while dist >= 1:
            key, idx = stage(key, idx, li, pos, dist, None if nblk == 1 else cap)
            dist //= 2
    gate_ref[0, 0] = key[0:top]
    idx_ref[0, 0] = _row_base(idx[0:top])


def _route(logits_t, cap):
    b, nv, ne, _ = logits_t.shape
    grp = SUBLANES
    top = cap // LANES
    out_block = pl.BlockSpec((1, 1, top, grp, LANES), lambda i, j: (i, j, 0, 0, 0))
    return pl.pallas_call(
        functools.partial(_route_kernel, cap=cap),
        grid=(b, ne // grp),
        in_specs=[pl.BlockSpec((1, nv, ne, LANES), lambda i, j: (i, 0, 0, 0))],
        out_specs=[out_block, out_block],
        out_shape=[jax.ShapeDtypeStruct((b, ne // grp, top, grp, LANES), F32),
                   jax.ShapeDtypeStruct((b, ne // grp, top, grp, LANES), jnp.int32)],
        compiler_params=pltpu.CompilerParams(dimension_semantics=("arbitrary", "arbitrary")),
        name="route",
    )(logits_t)


BF16_ROWS = 2 * SUBLANES
GATHER_STAGES = 4


def _dispatch_kernel(idx_ref, h_ref, xs_ref, *stage_refs):
    cap = idx_ref.shape[2]
    per_iter = BF16_ROWS * len(stage_refs)

    def body(r, carry):
        for s, stage_ref in enumerate(stage_refs):
            for u in range(BF16_ROWS):
                src = idx_ref[0, 0, r * per_iter + s * BF16_ROWS + u]
                dst = (u // SUBLANES) * GROUP + u % SUBLANES
                stage_ref[pl.ds(dst, ROW_TILE, stride=SUBLANES), :] = (
                    h_ref[0, pl.ds(src, ROW_TILE, stride=SUBLANES), :])
        for s, stage_ref in enumerate(stage_refs):
            row0 = pl.multiple_of(r * per_iter + s * BF16_ROWS, BF16_ROWS)
            for c in range(ROW_TILE):
                lo = stage_ref[c * SUBLANES:(c + 1) * SUBLANES, :]
                hi = stage_ref[GROUP + c * SUBLANES:GROUP + (c + 1) * SUBLANES, :]
                xs_ref[0, 0, pl.ds(row0, BF16_ROWS), c * LANES:(c + 1) * LANES] = (
                    jnp.concatenate([lo, hi], axis=0).astype(BF16))
        return carry

    lax.fori_loop(0, cap // per_iter, body, 0)


def _dispatch(idx, h_rows, n_exp, cap):
    b = h_rows.shape[0]
    rows = h_rows.shape[1]
    d = ROW_TILE * LANES
    return pl.pallas_call(
        _dispatch_kernel,
        grid=(b, n_exp),
        in_specs=[
            pl.BlockSpec((1, 1, cap), lambda i, e: (i * n_exp + e, 0, 0),
                         memory_space=pltpu.SMEM),
            pl.BlockSpec((1, rows, LANES), lambda i, e: (i, 0, 0),
                         pipeline_mode=pl.Buffered(1)),
        ],
        out_specs=pl.BlockSpec((1, 1, cap, d), lambda i, e: (i, e, 0, 0)),
        out_shape=jax.ShapeDtypeStruct((b, n_exp, cap, d), BF16),
        scratch_shapes=[pltpu.VMEM((2 * GROUP, LANES), F32)] * GATHER_STAGES,
        compiler_params=pltpu.CompilerParams(
            dimension_semantics=("arbitrary", "arbitrary"),
            vmem_limit_bytes=_vmem(rows * LANES * 4 + 4 * cap * d * 2)),
        name="dispatch",
    )(idx, h_rows)


def _ffn_kernel(xs_ref, wg_ref, wu_ref, wd_ref, y_ref):
    nb = xs_ref.shape[0]
    tf = wg_ref.shape[2]

    @pl.when(pl.program_id(1) == 0)
    def _():
        y_ref[...] = jnp.zeros(y_ref.shape, F32)

    wgu = jnp.concatenate([wg_ref[0].astype(BF16), wu_ref[0].astype(BF16)], axis=1)
    wd = wd_ref[0].astype(BF16)
    for b in range(nb):
        au = jnp.dot(xs_ref[b, 0], wgu, preferred_element_type=F32)
        hmid = (_silu(au[:, 0:tf]) * au[:, tf:2 * tf]).astype(BF16)
        contrib = jnp.dot(hmid, wd, preferred_element_type=F32)
        rows = contrib.shape[0]
        for c in range(ROW_TILE):
            y_ref[b, 0, :, c * SUBLANES:(c + 1) * SUBLANES, :] += (
                contrib[:, c * LANES:(c + 1) * LANES].reshape(rows // SUBLANES, SUBLANES, LANES))


def _ffn(xs, w_gate, w_up, w_down):
    b, n_exp, cap, d = xs.shape
    dff = w_gate.shape[2]
    tf = FFN_TILE
    groups = cap // SUBLANES
    return pl.pallas_call(
        _ffn_kernel,
        grid=(n_exp, dff // tf),
        in_specs=[
            pl.BlockSpec((b, 1, cap, d), lambda e, f: (0, e, 0, 0)),
            pl.BlockSpec((1, d, tf), lambda e, f: (e, 0, f)),
            pl.BlockSpec((1, d, tf), lambda e, f: (e, 0, f)),
            pl.BlockSpec((1, tf, d), lambda e, f: (e, f, 0)),
        ],
        out_specs=pl.BlockSpec((b, 1, groups, GROUP, LANES), lambda e, f: (0, e, 0, 0, 0)),
        out_shape=jax.ShapeDtypeStruct((b, n_exp, groups, GROUP, LANES), F32),
        compiler_params=pltpu.CompilerParams(
            dimension_semantics=("arbitrary", "arbitrary"),
            vmem_limit_bytes=_vmem(56 * 2 ** 20)),
        name="ffn",
    )(xs, w_gate, w_up, w_down)


def _combine_kernel(idx_ref, gate_ref, y_ref, xmid_ref, gpost_ref, g2_ref, o_ref, acc_ref,
                    *, n_exp):
    cap = idx_ref.shape[2]
    t = xmid_ref.shape[1]
    step = pl.program_id(1)

    @pl.when(step == 0)
    def _():
        zrows = SUBLANES * GROUP

        def zero(r, carry):
            acc_ref[pl.ds(pl.multiple_of(r * zrows, zrows), zrows), :] = jnp.zeros(
                (zrows, LANES), F32)
            return carry

        lax.fori_loop(0, acc_ref.shape[0] // zrows, zero, 0)

    @pl.when(step < n_exp)
    def _():
        def body(r, carry):
            offs, vals = [], []
            for u in range(SCATTER_UNROLL):
                dst = idx_ref[0, 0, r * SCATTER_UNROLL + u]
                gate = gate_ref[0, 0, r * SCATTER_UNROLL + u]
                src = (r * (SCATTER_UNROLL // SUBLANES) + u // SUBLANES) * GROUP + u % SUBLANES
                offs.append(dst)
                vals.append(acc_ref[pl.ds(dst, ROW_TILE, stride=SUBLANES), :]
                            + gate * y_ref[0, 0, pl.ds(src, ROW_TILE, stride=SUBLANES), :])
            for dst, val in zip(offs, vals):
                acc_ref[pl.ds(dst, ROW_TILE, stride=SUBLANES), :] = val
            return carry

        lax.fori_loop(0, cap // SCATTER_UNROLL, body, 0)

    @pl.when(step >= n_exp)
    def _():
        rows = t * ROW_TILE
        start = pl.multiple_of((step - n_exp) * rows, rows)
        blk = acc_ref[pl.ds(start, rows), :].reshape(t // SUBLANES, GROUP, LANES)
        ffn = jnp.concatenate(
            [blk[:, c * SUBLANES:(c + 1) * SUBLANES, :].reshape(t, LANES)
             for c in range(ROW_TILE)], axis=-1)
        o_ref[0] = xmid_ref[0] + g2_ref[0] * _rms(ffn, gpost_ref[...])


def _combine(idx, gate, y, x_mid, gpost, g2):
    b, n_exp, rows, _ = y.shape
    n, d = x_mid.shape[1:]
    cap = rows // ROW_TILE
    t = TOKEN_BLOCK
    last = n_exp - 1
    tok_block = pl.BlockSpec((1, t, d), lambda i, s: (i, jnp.maximum(s - n_exp, 0), 0))
    per_expert = pl.BlockSpec((1, 1, cap), lambda i, s: (i * n_exp + jnp.minimum(s, last), 0, 0),
                              memory_space=pltpu.SMEM)
    return pl.pallas_call(
        functools.partial(_combine_kernel, n_exp=n_exp),
        grid=(b, n_exp + n // t),
        in_specs=[
            per_expert, per_expert,
            pl.BlockSpec((1, 1, rows, LANES), lambda i, s: (i, jnp.minimum(s, last), 0, 0)),
            tok_block,
            pl.BlockSpec((1, d), lambda i, s: (0, 0)),
            pl.BlockSpec((1, 1, d), lambda i, s: (i, 0, 0)),
        ],
        out_specs=tok_block,
        out_shape=jax.ShapeDtypeStruct((b, n, d), F32),
        scratch_shapes=[pltpu.VMEM((n * ROW_TILE, LANES), F32)],
        compiler_params=pltpu.CompilerParams(
            dimension_semantics=("arbitrary", "arbitrary"),
            vmem_limit_bytes=_vmem(n * d * 4 + 2 * rows * LANES * 4 + 8 * t * d * 4)),
        name="combine",
    )(idx, gate, y, x_mid, gpost, g2)


def _rope_tables(n_pos):
    half = LANES // 2
    inv = 1.0 / (ROPE_BASE ** (jnp.arange(half, dtype=F32) / half))
    ang_a = (jnp.arange(n_pos // LANES, dtype=F32) * LANES)[:, None] * inv[None, :]
    ang_b = jnp.arange(LANES, dtype=F32)[:, None] * inv[None, :]
    ca, sa = jnp.cos(ang_a)[:, None, :], jnp.sin(ang_a)[:, None, :]
    cb, sb = jnp.cos(ang_b)[None, :, :], jnp.sin(ang_b)[None, :, :]
    cos = (ca * cb - sa * sb).reshape(n_pos, half)
    sin = (sa * cb + ca * sb).reshape(n_pos, half)
    return jnp.concatenate([cos, cos], axis=-1), jnp.concatenate([-sin, sin], axis=-1)


def kernel(x, c, ctx, c_ctx, w_ada, b_ada, pre_mix_g, post_mix_g, pre_ffn_g, post_ffn_g,
           w_in, conv_w, conv_b, ret_decay_logit, ret_norm_g, w_out,
           w_router, b_router, w_gate, w_up, w_down):
    bsz, n, d = x.shape
    ctx_len = ctx.shape[1]
    depth = w_ada.shape[0]
    n_exp = w_router.shape[2]
    cw = conv_w.shape[2]
    rw = RET_HEADS * LANES
    cap = max(1, EC_CAPACITY_FACTOR * n // n_exp)
    assert depth == 1, "only the single-layer block is implemented"
    assert d == ROW_TILE * LANES and n % TOKEN_BLOCK == 0 and TOKEN_BLOCK % LANES == 0
    assert n % GRID_W == 0 and TOKEN_BLOCK % GRID_W == 0 and ctx_len % SUBLANES == 0
    assert n_exp % SUBLANES == 0 and cap % LANES == 0 and (n // LANES) & (n // LANES - 1) == 0
    assert w_in.shape[2] == 3 * cw + 4 * rw

    assert (ctx_len + n) % LANES == 0
    cos_all, sin_all = _rope_tables(ctx_len + n)
    cos_c, sin_c = cos_all[:ctx_len], sin_all[:ctx_len]
    cos_x, sin_x = cos_all[ctx_len:], sin_all[ctx_len:]
    i = 0

    pad = -(bsz + 1) % SUBLANES
    cond = jnp.concatenate([c, c_ctx[None, :], jnp.zeros((pad, d), F32)], axis=0)
    mod = _modulation(cond, pltpu.with_memory_space_constraint(w_ada[i], pltpu.HBM), b_ada[i])
    sh1, sc1, g1, sh2, sc2, g2 = [mod[:bsz, j * d:(j + 1) * d].reshape(bsz, 1, d) for j in range(6)]
    csh1 = mod[bsz:bsz + 1, 0:d]
    csc1 = mod[bsz:bsz + 1, d:2 * d]

    lg = jax.nn.log_sigmoid(ret_decay_logit[i].astype(F32))
    w_in_b = w_in[i].astype(BF16)
    w_out_b = w_out[i].astype(BF16)
    k_off = 3 * cw + rw
    row = lambda a: a.reshape(1, -1)

    assert k_off % rw == 0
    s_f, s_b = _context_states(lg, ctx, row(pre_mix_g[i]), csc1, csh1, w_in_b, k_off // rw,
                               cos_c, sin_c)

    q, k, v, g, yb, conv = _mix_bwd(lg, x, row(pre_mix_g[i]), sc1, sh1, w_in_b,
                                    conv_w[i], row(conv_b[i]), cos_x, sin_x, s_b)

    wr = w_router[i].astype(F32).T
    wr_hi = wr.astype(BF16)
    wr_lo = (wr - wr_hi.astype(F32)).astype(BF16)
    wr_cat = jnp.concatenate([wr_hi, wr_lo], axis=0)
    br = b_router[i].astype(F32).reshape(n_exp, 1)
    x_mid, h_tiles, logits_t = _mix_fwd(lg, x, q, k, v, g, yb, conv, w_out_b, row(ret_norm_g[i]),
                                        row(post_mix_g[i]), row(pre_ffn_g[i]), g1, sc2, sh2,
                                        wr_cat, br, s_f)
    gate5, idx5 = _route(logits_t, cap)
    gate = gate5.transpose(0, 1, 3, 2, 4).reshape(bsz * n_exp, 1, cap)
    idx = idx5.transpose(0, 1, 3, 2, 4).reshape(bsz * n_exp, 1, cap)

    xs = _dispatch(idx, h_tiles.reshape(bsz, n * ROW_TILE, LANES), n_exp, cap)
    y = _ffn(xs, w_gate[i], w_up[i], w_down[i])
    return _combine(idx, gate, y.reshape(bsz, n_exp, cap * ROW_TILE, LANES), x_mid,
                    row(post_ffn_g[i]), g2)
```

```python
import functools

import jax
import jax.numpy as jnp
from jax import lax
from jax.experimental import pallas as pl
from jax.experimental.pallas import tpu as pltpu

F32 = jnp.float32
BF16 = jnp.bfloat16

GRID_W = 64
RET_HEADS = 4
ROPE_BASE = 10000.0
EC_CAPACITY_FACTOR = 2
RMS_EPS = 1e-6
GN_EPS = 1e-6

LANES = 128
SUBLANES = 8
ROW_TILE = SUBLANES
GROUP = SUBLANES * ROW_TILE
VMEM_CAP = 60000 * 1024

TOKEN_BLOCK = 512
PROJ_BLOCK = 1024
RET_CHUNK = 256
FFN_TILE = 512
SCATTER_UNROLL = 2 * SUBLANES


def _vmem(nbytes):
    return min(int(nbytes), VMEM_CAP)


def _rms(x, g):
    return x * lax.rsqrt(jnp.mean(x * x, axis=-1, keepdims=True) + RMS_EPS) * g


def _silu(x):
    return x * (1.0 / (1.0 + jnp.exp(-x)))


def _rot_half(t):
    return pltpu.roll(t, LANES // 2, axis=1)


def _store_tile_major(ref, lead, val):
    rows = val.shape[0]
    for c in range(ROW_TILE):
        ref[lead + (slice(None), slice(c * SUBLANES, (c + 1) * SUBLANES), slice(None))] = (
            val[:, c * LANES:(c + 1) * LANES].reshape(rows // SUBLANES, SUBLANES, LANES))


def _row_base(r):
    return (r >> 3) * GROUP + (r & (SUBLANES - 1))


def _modulation_kernel(cond_ref, w_ref, b_ref, o_ref):
    s = _silu(cond_ref[...])
    o_ref[...] = jnp.dot(s, w_ref[...], preferred_element_type=F32,
                         precision=lax.Precision.HIGHEST) + b_ref[...]


def _modulation(cond, w_ada, b_ada):
    rows, d = cond.shape
    cols = w_ada.shape[1]
    tile = cols // 4
    return pl.pallas_call(
        _modulation_kernel,
        grid=(cols // tile,),
        in_specs=[
            pl.BlockSpec((rows, d), lambda j: (0, 0)),
            pl.BlockSpec((d, tile), lambda j: (0, j)),
            pl.BlockSpec((1, tile), lambda j: (0, j)),
        ],
        out_specs=pl.BlockSpec((rows, tile), lambda j: (0, j)),
        out_shape=jax.ShapeDtypeStruct((rows, cols), F32),
        compiler_params=pltpu.CompilerParams(
            dimension_semantics=("arbitrary",),
            vmem_limit_bytes=_vmem(3 * d * tile * 4)),
        name="modulation",
    )(cond, w_ada, b_ada.reshape(1, cols))


def _context_kernel(lg_ref, ctx_ref, gpre_ref, sc_ref, sh_ref, wk_ref, wv_ref,
                    cos_ref, sin_ref, sf_ref, sb_ref):
    n = ctx_ref.shape[1]
    hd = LANES
    hc = _rms(ctx_ref[0], gpre_ref[...]) * (1.0 + sc_ref[...]) + sh_ref[...]
    hc = hc.astype(BF16)
    k = jnp.dot(hc, wk_ref[...], preferred_element_type=F32)
    v = jnp.dot(hc, wv_ref[...], preferred_element_type=F32)
    cos = cos_ref[...]
    sin = sin_ref[...]
    pos = lax.broadcasted_iota(jnp.int32, (n, 1), 0).astype(F32)
    for h in range(RET_HEADS):
        kh = k[:, h * hd:(h + 1) * hd]
        kr = (kh * cos + _rot_half(kh) * sin) * (hd ** -0.5)
        vh = v[:, h * hd:(h + 1) * hd].astype(BF16)
        wf = jnp.exp(lg_ref[0, h] * (n - 1.0 - pos))
        wb = jnp.exp(lg_ref[1, h] * pos)
        kf = (kr * wf).T.astype(BF16)
        kb = (kr * wb).T.astype(BF16)
        sf_ref[0, h] = jnp.dot(kf, vh, preferred_element_type=F32)
        sb_ref[0, h] = jnp.dot(kb, vh, preferred_element_type=F32)


def _context_states(lg, ctx, gpre, csc1, csh1, w_in, k_col, cos, sin):
    b, n, d = ctx.shape
    hd = LANES
    rw = RET_HEADS * hd
    st = jax.ShapeDtypeStruct((b, RET_HEADS, hd, hd), F32)
    vec = pl.BlockSpec((1, d), lambda i: (0, 0))
    return pl.pallas_call(
        _context_kernel,
        grid=(b,),
        in_specs=[
            pl.BlockSpec(memory_space=pltpu.SMEM),
            pl.BlockSpec((1, n, d), lambda i: (i, 0, 0)),
            vec, vec, vec,
            pl.BlockSpec((d, rw), lambda i: (0, k_col)),
            pl.BlockSpec((d, rw), lambda i: (0, k_col + 1)),
            pl.BlockSpec((n, hd), lambda i: (0, 0)),
            pl.BlockSpec((n, hd), lambda i: (0, 0)),
        ],
        out_specs=[pl.BlockSpec((1, RET_HEADS, hd, hd), lambda i: (i, 0, 0, 0))] * 2,
        out_shape=[st, st],
        compiler_params=pltpu.CompilerParams(dimension_semantics=("arbitrary",)),
        name="context",
    )(lg, ctx, gpre, csc1, csh1, w_in, w_in, cos, sin)


def _fill_tables(lg_ref, direction, decay_ref, xi_ref, zeta_ref):
    c = decay_ref.shape[1]
    i = lax.broadcasted_iota(jnp.int32, (c, c), 0)
    j = lax.broadcasted_iota(jnp.int32, (c, c), 1)
    col = lax.broadcasted_iota(jnp.int32, (c, 1), 0).astype(F32)
    for h in range(RET_HEADS):
        lg = lg_ref[direction, h]
        if direction == 0:
            rel = (i - j).astype(F32)
            mask = i >= j
            xi_ref[h] = jnp.exp(lg * (col + 1.0))
            zeta_ref[h] = jnp.exp(lg * (c - 1.0 - col))
        else:
            rel = (j - i).astype(F32)
            mask = j > i
            xi_ref[h] = jnp.exp(lg * (c - col))
            zeta_ref[h] = jnp.exp(lg * col)
        decay_ref[h] = jnp.where(mask, jnp.exp(lg * jnp.where(mask, rel, 0.0)), 0.0)


def _retention_block(q_blk, k_blk_f32, v_blk, lg, s_ref, decay_ref, xi_ref, zeta_ref, h,
                     reverse):
    c = decay_ref.shape[1]
    n_chunks = q_blk.shape[0] // c
    outs = [None] * n_chunks
    for ci in (reversed(range(n_chunks)) if reverse else range(n_chunks)):
        rows = slice(ci * c, (ci + 1) * c)
        qh, kh_f32, vh = q_blk[rows], k_blk_f32[rows], v_blk[rows]
        kh = kh_f32.astype(BF16)
        scores = lax.dot_general(qh, kh, (((1,), (1,)), ((), ())), preferred_element_type=F32)
        s_prev = s_ref[h]
        lhs = jnp.concatenate([(scores * decay_ref[h]).astype(BF16),
                               (qh.astype(F32) * xi_ref[h]).astype(BF16)], axis=1)
        rhs = jnp.concatenate([vh, s_prev.astype(BF16)], axis=0)
        outs[ci] = jnp.dot(lhs, rhs, preferred_element_type=F32)
        kz = (kh_f32 * zeta_ref[h]).T.astype(BF16)
        s_ref[h] = jnp.exp(lg * c) * s_prev + jnp.dot(kz, vh, preferred_element_type=F32)
    return outs[0] if n_chunks == 1 else jnp.concatenate(outs, axis=0)


def _mix_bwd_kernel(lg_ref, x_ref, gpre_ref, sc_ref, sh_ref, win_ref, cw_ref, cb_ref,
                    cos_ref, sin_ref, s0_ref,
                    q_ref, k_ref, v_ref, g_ref, yb_ref, conv_ref,
                    s_ref, decay_ref, xi_ref, zeta_ref):
    t = x_ref.shape[1]
    hd = LANES
    cw = conv_ref.shape[2]
    rw = RET_HEADS * hd

    @pl.when((pl.program_id(0) == 0) & (pl.program_id(1) == 0))
    def _():
        _fill_tables(lg_ref, 1, decay_ref, xi_ref, zeta_ref)

    @pl.when(pl.program_id(1) == 0)
    def _():
        s_ref[...] = s0_ref[0]

    hx = (_rms(x_ref[0], gpre_ref[...]) * (1.0 + sc_ref[0]) + sh_ref[0]).astype(BF16)
    off = 3 * cw

    def project(c0, width):
        return jnp.dot(hx, win_ref[:, c0:c0 + width], preferred_element_type=F32)

    g_ref[0] = project(off + 3 * rw, rw)
    v_bf = project(off + 2 * rw, rw).astype(BF16)
    v_ref[0] = v_bf
    cos = cos_ref[...]
    sin = sin_ref[...]
    q = project(off, rw)
    qr = [(q[:, h * hd:(h + 1) * hd] * cos + _rot_half(q[:, h * hd:(h + 1) * hd]) * sin)
          .astype(BF16) for h in range(RET_HEADS)]
    k = project(off + rw, rw)
    kr = [(k[:, h * hd:(h + 1) * hd] * cos + _rot_half(k[:, h * hd:(h + 1) * hd]) * sin)
          * (hd ** -0.5) for h in range(RET_HEADS)]
    u = project(cw, cw) * project(2 * cw, cw)
    x_b = project(0, cw)

    cq = cw // RET_HEADS
    col = lax.broadcasted_iota(jnp.int32, (t, cq), 0) & (GRID_W - 1)
    for h in range(RET_HEADS):
        sl = slice(h * hd, (h + 1) * hd)
        q_ref[0, :, sl] = qr[h]
        k_ref[0, :, sl] = kr[h].astype(BF16)
        yb_ref[0, :, sl] = _retention_block(qr[h], kr[h], v_bf[:, sl], lg_ref[1, h], s_ref,
                                            decay_ref, xi_ref, zeta_ref, h, reverse=True)
        cs = slice(h * cq, (h + 1) * cq)
        uq = u[:, cs]
        u_prev = jnp.where(col == 0, 0.0, pltpu.roll(uq, 1, axis=0))
        u_next = jnp.where(col == GRID_W - 1, 0.0, pltpu.roll(uq, t - 1, axis=0))
        y = (u_prev * cw_ref[0:1, cs] + uq * cw_ref[1:2, cs] + u_next * cw_ref[2:3, cs]
             + cb_ref[:, cs])
        conv_ref[0, :, cs] = (x_b[:, cs] * y).astype(BF16)


def _mix_bwd(lg, x, gpre, sc1, sh1, w_in, conv_w, conv_b, cos, sin, s0_b):
    b, n, d = x.shape
    t = PROJ_BLOCK
    nb = n // t
    hd = LANES
    cw = conv_w.shape[1]
    rw = RET_HEADS * hd
    rev = lambda i, j: (i, nb - 1 - j, 0)
    vec = pl.BlockSpec((1, d), lambda i, j: (0, 0))
    bvec = pl.BlockSpec((1, 1, d), lambda i, j: (i, 0, 0))
    tok = lambda w: pl.BlockSpec((1, t, w), rev)
    rope = pl.BlockSpec((t, hd), lambda i, j: (nb - 1 - j, 0))
    out = lambda w, dt: jax.ShapeDtypeStruct((b, n, w), dt)
    return pl.pallas_call(
        _mix_bwd_kernel,
        grid=(b, nb),
        in_specs=[
            pl.BlockSpec(memory_space=pltpu.SMEM),
            tok(d), vec, bvec, bvec,
            pl.BlockSpec(w_in.shape, lambda i, j: (0, 0), pipeline_mode=pl.Buffered(1)),
            pl.BlockSpec(conv_w.shape, lambda i, j: (0, 0)),
            pl.BlockSpec((1, cw), lambda i, j: (0, 0)),
            rope, rope,
            pl.BlockSpec((1, RET_HEADS, hd, hd), lambda i, j: (i, 0, 0, 0)),
        ],
        out_specs=[tok(rw), tok(rw), tok(rw), tok(rw), tok(rw), tok(cw)],
        out_shape=[out(rw, BF16), out(rw, BF16), out(rw, BF16), out(rw, F32),
                   out(rw, F32), out(cw, BF16)],
        scratch_shapes=[pltpu.VMEM((RET_HEADS, hd, hd), F32),
                        pltpu.VMEM((RET_HEADS, RET_CHUNK, RET_CHUNK), F32),
                        pltpu.VMEM((RET_HEADS, RET_CHUNK, 1), F32),
                        pltpu.VMEM((RET_HEADS, RET_CHUNK, 1), F32)],
        compiler_params=pltpu.CompilerParams(
            dimension_semantics=("arbitrary", "arbitrary"),
            vmem_limit_bytes=_vmem(56 * 2 ** 20)),
        name="mix_bwd",
    )(lg, x, gpre, sc1, sh1, w_in, conv_w, conv_b, cos, sin, s0_b)


def _mix_fwd_kernel(lg_ref, x_ref, q_ref, k_ref, v_ref, g_ref, yb_ref, conv_ref,
                    wout_ref, gn_ref, gpost_ref, gffn_ref, g1_ref, sc_ref, sh_ref,
                    wr_ref, br_ref, s0_ref,
                    xmid_ref, h2_ref, logit_ref, s_ref, mix_ref, prev_ref,
                    decay_ref, xi_ref, zeta_ref, *, nb):
    hd = LANES
    cw = conv_ref.shape[2]
    step = pl.program_id(0)

    @pl.when(step == 0)
    def _():
        prev_ref[...] = jnp.zeros(prev_ref.shape, BF16)
        _fill_tables(lg_ref, 0, decay_ref, xi_ref, zeta_ref)

    @pl.when(step % nb == 0)
    def _():
        s_ref[...] = s0_ref[0]

    t = x_ref.shape[1]
    ne = logit_ref.shape[2]
    mix = jnp.dot(prev_ref[...], wout_ref[...], preferred_element_type=F32)

    def finish_rows(r0, r1):
        x_mid = x_ref[0, r0:r1] + g1_ref[0] * _rms(mix[r0:r1], gpost_ref[...])
        xmid_ref[0, r0:r1] = x_mid
        h2 = _rms(x_mid, gffn_ref[...]) * (1.0 + sc_ref[0]) + sh_ref[0]
        h_hi = h2.astype(BF16)
        h_lo = (h2 - h_hi.astype(F32)).astype(BF16)
        nt = (((1,), (1,)), ((), ()))
        p = (lax.dot_general(wr_ref[...], h_hi, nt, preferred_element_type=F32)
             + lax.dot_general(wr_ref[...], h_lo, nt, preferred_element_type=F32))
        logit_ref[0, r0 // LANES] = p[0:ne] + p[ne:2 * ne] + br_ref[...]
        for c in range(ROW_TILE):
            h2_ref[0, r0 // SUBLANES:r1 // SUBLANES, c * SUBLANES:(c + 1) * SUBLANES, :] = (
                h2[:, c * LANES:(c + 1) * LANES].reshape((r1 - r0) // SUBLANES, SUBLANES, LANES))

    mix_ref[:, 0:cw] = conv_ref[0]
    rows = t // RET_HEADS
    for h in range(RET_HEADS):
        sl = slice(h * hd, (h + 1) * hd)
        y = _retention_block(q_ref[0, :, sl], k_ref[0, :, sl].astype(F32), v_ref[0, :, sl],
                             lg_ref[0, h], s_ref, decay_ref, xi_ref, zeta_ref, h, reverse=False)
        finish_rows(h * rows, (h + 1) * rows)
        y = y + yb_ref[0, :, sl]
        mu = jnp.mean(y, axis=-1, keepdims=True)
        yc = y - mu
        var = jnp.mean(yc * yc, axis=-1, keepdims=True)
        yn = yc * lax.rsqrt(var + GN_EPS) * gn_ref[:, sl]
        mix_ref[:, cw + h * hd:cw + (h + 1) * hd] = (
            _silu(g_ref[0, :, sl]) * yn).astype(BF16)
    prev_ref[...] = mix_ref[...]


def _mix_fwd(lg, x, q, k, v, g, yb, conv, w_out, gn, gpost, gffn, g1, sc2, sh2, wr, br, s0_f):
    b, n, d = x.shape
    t = TOKEN_BLOCK
    nb = n // t
    hd = LANES
    cw = conv.shape[2]
    rw = RET_HEADS * hd
    ne = br.shape[0]
    assert t // RET_HEADS == LANES
    steps = b * nb
    ret = lambda s: jnp.minimum(s, steps - 1)
    outp = lambda s: jnp.maximum(s - 1, 0)
    vec = lambda w: pl.BlockSpec((1, w), lambda s: (0, 0))
    bvec = pl.BlockSpec((1, 1, d), lambda s: (outp(s) // nb, 0, 0))
    rtok = lambda w: pl.BlockSpec((1, t, w), lambda s: (ret(s) // nb, ret(s) % nb, 0))
    otok = lambda w: pl.BlockSpec((1, t, w), lambda s: (outp(s) // nb, outp(s) % nb, 0))
    return pl.pallas_call(
        functools.partial(_mix_fwd_kernel, nb=nb),
        grid=(steps + 1,),
        in_specs=[
            pl.BlockSpec(memory_space=pltpu.SMEM),
            otok(d), rtok(rw), rtok(rw), rtok(rw), rtok(rw), rtok(rw), rtok(cw),
            pl.BlockSpec(w_out.shape, lambda s: (0, 0)),
            vec(rw), vec(d), vec(d), bvec, bvec, bvec,
            pl.BlockSpec(wr.shape, lambda s: (0, 0)),
            pl.BlockSpec((ne, 1), lambda s: (0, 0)),
            pl.BlockSpec((1, RET_HEADS, hd, hd), lambda s: (ret(s) // nb, 0, 0, 0)),
        ],
        out_specs=[otok(d),
                   pl.BlockSpec((1, t // SUBLANES, GROUP, LANES),
                                lambda s: (outp(s) // nb, outp(s) % nb, 0, 0)),
                   pl.BlockSpec((1, t // LANES, ne, LANES),
                                lambda s: (outp(s) // nb, outp(s) % nb, 0, 0))],
        out_shape=[jax.ShapeDtypeStruct((b, n, d), F32),
                   jax.ShapeDtypeStruct((b, n // SUBLANES, GROUP, LANES), F32),
                   jax.ShapeDtypeStruct((b, n // LANES, ne, LANES), F32)],
        scratch_shapes=[pltpu.VMEM((RET_HEADS, hd, hd), F32),
                        pltpu.VMEM((t, cw + rw), BF16),
                        pltpu.VMEM((t, cw + rw), BF16),
                        pltpu.VMEM((RET_HEADS, RET_CHUNK, RET_CHUNK), F32),
                        pltpu.VMEM((RET_HEADS, RET_CHUNK, 1), F32),
                        pltpu.VMEM((RET_HEADS, RET_CHUNK, 1), F32)],
        compiler_params=pltpu.CompilerParams(
            dimension_semantics=("arbitrary",),
            vmem_limit_bytes=_vmem(48 * 2 ** 20)),
        name="mix_fwd",
    )(lg, x, q, k, v, g, yb, conv, w_out, gn, gpost, gffn, g1, sc2, sh2, wr, br, s0_f)


def _route_kernel(logit_ref, gate_ref, idx_ref, *, cap):
    nv, ne, _ = logit_ref.shape[1:]
    grp = SUBLANES
    x = logit_ref[0]
    m = jnp.max(x, axis=1, keepdims=True)
    s = jnp.sum(jnp.exp(x - m), axis=1, keepdims=True)
    e0 = pl.multiple_of(pl.program_id(1) * grp, grp)
    xg = logit_ref[0, :, pl.ds(e0, grp), :]
    key = jnp.exp(xg - m) / s

    top = cap // LANES

    def coords(nvreg):
        shape = (nvreg, grp, LANES)
        vi = lax.broadcasted_iota(jnp.int32, shape, 0)
        li = lax.broadcasted_iota(jnp.int32, shape, 2)
        return vi, li, (vi // top) * cap + li * top + vi % top

    def better(pk, pi, k, i):
        return (pk > k) | ((pk == k) & (pi < i))

    def stage(k, i, li, pos, dist, span):
        if dist < top:
            halves = lambda a: a.reshape(a.shape[0] // (2 * dist), 2, dist, grp, LANES)
            k5, i5 = halves(k), halves(i)
            k_lo, k_hi, i_lo, i_hi = k5[:, 0], k5[:, 1], i5[:, 0], i5[:, 1]
            hi_wins = better(k_hi, i_hi, k_lo, i_lo)
            k_b, k_w = jnp.where(hi_wins, k_hi, k_lo), jnp.where(hi_wins, k_lo, k_hi)
            i_b, i_w = jnp.where(hi_wins, i_hi, i_lo), jnp.where(hi_wins, i_lo, i_hi)
            if span is not None:
                desc = (halves(pos)[:, 0] & span) == 0
                k_b, k_w = jnp.where(desc, k_b, k_w), jnp.where(desc, k_w, k_b)
                i_b, i_w = jnp.where(desc, i_b, i_w), jnp.where(desc, i_w, i_b)
            join = lambda lo, hi: jnp.concatenate([lo[:, None], hi[:, None]], axis=1).reshape(k.shape)
            return join(k_b, k_w), join(i_b, i_w)
        sh = dist // top
        low = (li & sh) == 0
        pk = jnp.where(low, pltpu.roll(k, LANES - sh, axis=2), pltpu.roll(k, sh, axis=2))
        pi = jnp.where(low, pltpu.roll(i, LANES - sh, axis=2), pltpu.roll(i, sh, axis=2))
        want_best = low if span is None else low == ((pos & span) == 0)
        take = better(pk, pi, k, i) == want_best
        return jnp.where(take, pk, k), jnp.where(take, pi, i)

    vi, li, pos = coords(nv)
    idx = vi * LANES + li
    span = 2
    while span <= cap:
        dist = span // 2
        while dist >= 1:
            key, idx = stage(key, idx, li, pos, dist, span)
            dist //= 2
        span *= 2
    nblk = nv // top
    while nblk > 1:
        k5 = key.reshape(nblk // 2, 2, top, grp, LANES)
        i5 = idx.reshape(nblk // 2, 2, top, grp, LANES)
        k0, k1, i0, i1 = k5[:, 0], k5[:, 1], i5[:, 0], i5[:, 1]
        take = better(k1, i1, k0, i0)
        nblk //= 2
        key = jnp.where(take, k1, k0).reshape(nblk * top, grp, LANES)
        idx = jnp.where(take, i1, i0).reshape(nblk * top, grp, LANES)
        vi, li, pos = coords(nblk * top)
        dist = cap // 2
        while dist >= 1:
            key, idx = stage(key, idx, li, pos, dist, None if nblk == 1 else cap)
            dist //= 2
    gate_ref[0, 0] = key[0:top]
    idx_ref[0, 0] = _row_base(idx[0:top])


def _route(logits_t, cap):
    b, nv, ne, _ = logits_t.shape
    grp = SUBLANES
    top = cap // LANES
    out_block = pl.BlockSpec((1, 1, top, grp, LANES), lambda i, j: (i, j, 0, 0, 0))
    return pl.pallas_call(
        functools.partial(_route_kernel, cap=cap),
        grid=(b, ne // grp),
        in_specs=[pl.BlockSpec((1, nv, ne, LANES), lambda i, j: (i, 0, 0, 0))],
        out_specs=[out_block, out_block],
        out_shape=[jax.ShapeDtypeStruct((b, ne // grp, top, grp, LANES), F32),
                   jax.ShapeDtypeStruct((b, ne // grp, top, grp, LANES), jnp.int32)],
        compiler_params=pltpu.CompilerParams(dimension_semantics=("arbitrary", "arbitrary")),
        name="route",
    )(logits_t)


BF16_ROWS = 2 * SUBLANES
GATHER_STAGES = 4


def _dispatch_kernel(idx_ref, h_ref, xs_ref, *stage_refs):
    cap = idx_ref.shape[2]
    per_iter = BF16_ROWS * len(stage_refs)

    def body(r, carry):
        for s, stage_ref in enumerate(stage_refs):
            for u in range(BF16_ROWS):
                src = idx_ref[0, 0, r * per_iter + s * BF16_ROWS + u]
                dst = (u // SUBLANES) * GROUP + u % SUBLANES
                stage_ref[pl.ds(dst, ROW_TILE, stride=SUBLANES), :] = (
                    h_ref[0, pl.ds(src, ROW_TILE, stride=SUBLANES), :])
        for s, stage_ref in enumerate(stage_refs):
            row0 = pl.multiple_of(r * per_iter + s * BF16_ROWS, BF16_ROWS)
            for c in range(ROW_TILE):
                lo = stage_ref[c * SUBLANES:(c + 1) * SUBLANES, :]
                hi = stage_ref[GROUP + c * SUBLANES:GROUP + (c + 1) * SUBLANES, :]
                xs_ref[0, 0, pl.ds(row0, BF16_ROWS), c * LANES:(c + 1) * LANES] = (
                    jnp.concatenate([lo, hi], axis=0).astype(BF16))
        return carry

    lax.fori_loop(0, cap // per_iter, body, 0)


def _dispatch(idx, h_rows, n_exp, cap):
    b = h_rows.shape[0]
    rows = h_rows.shape[1]
    d = ROW_TILE * LANES
    return pl.pallas_call(
        _dispatch_kernel,
        grid=(b, n_exp),
        in_specs=[
            pl.BlockSpec((1, 1, cap), lambda i, e: (i * n_exp + e, 0, 0),
                         memory_space=pltpu.SMEM),
            pl.BlockSpec((1, rows, LANES), lambda i, e: (i, 0, 0),
                         pipeline_mode=pl.Buffered(1)),
        ],
        out_specs=pl.BlockSpec((1, 1, cap, d), lambda i, e: (i, e, 0, 0)),
        out_shape=jax.ShapeDtypeStruct((b, n_exp, cap, d), BF16),
        scratch_shapes=[pltpu.VMEM((2 * GROUP, LANES), F32)] * GATHER_STAGES,
        compiler_params=pltpu.CompilerParams(
            dimension_semantics=("arbitrary", "arbitrary"),
            vmem_limit_bytes=_vmem(rows * LANES * 4 + 4 * cap * d * 2)),
        name="dispatch",
    )(idx, h_rows)


def _ffn_kernel(xs_ref, wg_ref, wu_ref, wd_ref, y_ref):
    nb = xs_ref.shape[0]
    tf = wg_ref.shape[2]

    @pl.when(pl.program_id(1) == 0)
    def _():
        y_ref[...] = jnp.zeros(y_ref.shape, F32)

    wgu = jnp.concatenate([wg_ref[0].astype(BF16), wu_ref[0].astype(BF16)], axis=1)
    wd = wd_ref[0].astype(BF16)
    for b in range(nb):
        au = jnp.dot(xs_ref[b, 0], wgu, preferred_element_type=F32)
        hmid = (_silu(au[:, 0:tf]) * au[:, tf:2 * tf]).astype(BF16)
        contrib = jnp.dot(hmid, wd, preferred_element_type=F32)
        rows = contrib.shape[0]
        for c in range(ROW_TILE):
            y_ref[b, 0, :, c * SUBLANES:(c + 1) * SUBLANES, :] += (
                contrib[:, c * LANES:(c + 1) * LANES].reshape(rows // SUBLANES, SUBLANES, LANES))


def _ffn(xs, w_gate, w_up, w_down):
    b, n_exp, cap, d = xs.shape
    dff = w_gate.shape[2]
    tf = FFN_TILE
    groups = cap // SUBLANES
    return pl.pallas_call(
        _ffn_kernel,
        grid=(n_exp, dff // tf),
        in_specs=[
            pl.BlockSpec((b, 1, cap, d), lambda e, f: (0, e, 0, 0)),
            pl.BlockSpec((1, d, tf), lambda e, f: (e, 0, f)),
            pl.BlockSpec((1, d, tf), lambda e, f: (e, 0, f)),
            pl.BlockSpec((1, tf, d), lambda e, f: (e, f, 0)),
        ],
        out_specs=pl.BlockSpec((b, 1, groups, GROUP, LANES), lambda e, f: (0, e, 0, 0, 0)),
        out_shape=jax.ShapeDtypeStruct((b, n_exp, groups, GROUP, LANES), F32),
        compiler_params=pltpu.CompilerParams(
            dimension_semantics=("arbitrary", "arbitrary"),
            vmem_limit_bytes=_vmem(56 * 2 ** 20)),
        name="ffn",
    )(xs, w_gate, w_up, w_down)


def _combine_kernel(idx_ref, gate_ref, y_ref, xmid_ref, gpost_ref, g2_ref, o_ref, acc_ref,
                    *, n_exp):
    cap = idx_ref.shape[2]
    t = xmid_ref.shape[1]
    step = pl.program_id(1)

    @pl.when(step == 0)
    def _():
        zrows = SUBLANES * GROUP

        def zero(r, carry):
            acc_ref[pl.ds(pl.multiple_of(r * zrows, zrows), zrows), :] = jnp.zeros(
                (zrows, LANES), F32)
            return carry

        lax.fori_loop(0, acc_ref.shape[0] // zrows, zero, 0)

    @pl.when(step < n_exp)
    def _():
        def body(r, carry):
            offs, vals = [], []
            for u in range(SCATTER_UNROLL):
                dst = idx_ref[0, 0, r * SCATTER_UNROLL + u]
                gate = gate_ref[0, 0, r * SCATTER_UNROLL + u]
                src = (r * (SCATTER_UNROLL // SUBLANES) + u // SUBLANES) * GROUP + u % SUBLANES
                offs.append(dst)
                vals.append(acc_ref[pl.ds(dst, ROW_TILE, stride=SUBLANES), :]
                            + gate * y_ref[0, 0, pl.ds(src, ROW_TILE, stride=SUBLANES), :])
            for dst, val in zip(offs, vals):
                acc_ref[pl.ds(dst, ROW_TILE, stride=SUBLANES), :] = val
            return carry

        lax.fori_loop(0, cap // SCATTER_UNROLL, body, 0)

    @pl.when(step >= n_exp)
    def _():
        rows = t * ROW_TILE
        start = pl.multiple_of((step - n_exp) * rows, rows)
        blk = acc_ref[pl.ds(start, rows), :].reshape(t // SUBLANES, GROUP, LANES)
        ffn = jnp.concatenate(
            [blk[:, c * SUBLANES:(c + 1) * SUBLANES, :].reshape(t, LANES)
             for c in range(ROW_TILE)], axis=-1)
        o_ref[0] = xmid_ref[0] + g2_ref[0] * _rms(ffn, gpost_ref[...])


def _combine(idx, gate, y, x_mid, gpost, g2):
    b, n_exp, rows, _ = y.shape
    n, d = x_mid.shape[1:]
    cap = rows // ROW_TILE
    t = TOKEN_BLOCK
    last = n_exp - 1
    tok_block = pl.BlockSpec((1, t, d), lambda i, s: (i, jnp.maximum(s - n_exp, 0), 0))
    per_expert = pl.BlockSpec((1, 1, cap), lambda i, s: (i * n_exp + jnp.minimum(s, last), 0, 0),
                              memory_space=pltpu.SMEM)
    return pl.pallas_call(
        functools.partial(_combine_kernel, n_exp=n_exp),
        grid=(b, n_exp + n // t),
        in_specs=[
            per_expert, per_expert,
            pl.BlockSpec((1, 1, rows, LANES), lambda i, s: (i, jnp.minimum(s, last), 0, 0)),
            tok_block,
            pl.BlockSpec((1, d), lambda i, s: (0, 0)),
            pl.BlockSpec((1, 1, d), lambda i, s: (i, 0, 0)),
        ],
        out_specs=tok_block,
        out_shape=jax.ShapeDtypeStruct((b, n, d), F32),
        scratch_shapes=[pltpu.VMEM((n * ROW_TILE, LANES), F32)],
        compiler_params=pltpu.CompilerParams(
            dimension_semantics=("arbitrary", "arbitrary"),
            vmem_limit_bytes=_vmem(n * d * 4 + 2 * rows * LANES * 4 + 8 * t * d * 4)),
        name="combine",
    )(idx, gate, y, x_mid, gpost, g2)


def _rope_tables(n_pos):
    half = LANES // 2
    inv = 1.0 / (ROPE_BASE ** (jnp.arange(half, dtype=F32) / half))
    ang_a = (jnp.arange(n_pos // LANES, dtype=F32) * LANES)[:, None] * inv[None, :]
    ang_b = jnp.arange(LANES, dtype=F32)[:, None] * inv[None, :]
    ca, sa = jnp.cos(ang_a)[:, None, :], jnp.sin(ang_a)[:, None, :]
    cb, sb = jnp.cos(ang_b)[None, :, :], jnp.sin(ang_b)[None, :, :]
    cos = (ca * cb - sa * sb).reshape(n_pos, half)
    sin = (sa * cb + ca * sb).reshape(n_pos, half)
    return jnp.concatenate([cos, cos], axis=-1), jnp.concatenate([-sin, sin], axis=-1)


def kernel(x, c, ctx, c_ctx, w_ada, b_ada, pre_mix_g, post_mix_g, pre_ffn_g, post_ffn_g,
           w_in, conv_w, conv_b, ret_decay_logit, ret_norm_g, w_out,
           w_router, b_router, w_gate, w_up, w_down):
    bsz, n, d = x.shape
    ctx_len = ctx.shape[1]
    depth = w_ada.shape[0]
    n_exp = w_router.shape[2]
    cw = conv_w.shape[2]
    rw = RET_HEADS * LANES
    cap = max(1, EC_CAPACITY_FACTOR * n // n_exp)
    assert depth == 1, "only the single-layer block is implemented"
    assert d == ROW_TILE * LANES and n % TOKEN_BLOCK == 0 and TOKEN_BLOCK % LANES == 0
    assert n % GRID_W == 0 and TOKEN_BLOCK % GRID_W == 0 and ctx_len % SUBLANES == 0
    assert n_exp % SUBLANES == 0 and cap % LANES == 0 and (n // LANES) & (n // LANES - 1) == 0
    assert w_in.shape[2] == 3 * cw + 4 * rw

    assert (ctx_len + n) % LANES == 0
    cos_all, sin_all = _rope_tables(ctx_len + n)
    cos_c, sin_c = cos_all[:ctx_len], sin_all[:ctx_len]
    cos_x, sin_x = cos_all[ctx_len:], sin_all[ctx_len:]
    i = 0

    pad = -(bsz + 1) % SUBLANES
    cond = jnp.concatenate([c, c_ctx[None, :], jnp.zeros((pad, d), F32)], axis=0)
    mod = _modulation(cond, pltpu.with_memory_space_constraint(w_ada[i], pltpu.HBM), b_ada[i])
    sh1, sc1, g1, sh2, sc2, g2 = [mod[:bsz, j * d:(j + 1) * d].reshape(bsz, 1, d) for j in range(6)]
    csh1 = mod[bsz:bsz + 1, 0:d]
    csc1 = mod[bsz:bsz + 1, d:2 * d]

    lg = jax.nn.log_sigmoid(ret_decay_logit[i].astype(F32))
    w_in_b = w_in[i].astype(BF16)
    w_out_b = w_out[i].astype(BF16)
    k_off = 3 * cw + rw
    row = lambda a: a.reshape(1, -1)

    assert k_off % rw == 0
    s_f, s_b = _context_states(lg, ctx, row(pre_mix_g[i]), csc1, csh1, w_in_b, k_off // rw,
                               cos_c, sin_c)

    q, k, v, g, yb, conv = _mix_bwd(lg, x, row(pre_mix_g[i]), sc1, sh1, w_in_b,
                                    conv_w[i], row(conv_b[i]), cos_x, sin_x, s_b)

    wr = w_router[i].astype(F32).T
    wr_hi = wr.astype(BF16)
    wr_lo = (wr - wr_hi.astype(F32)).astype(BF16)
    wr_cat = jnp.concatenate([wr_hi, wr_lo], axis=0)
    br = b_router[i].astype(F32).reshape(n_exp, 1)
    x_mid, h_tiles, logits_t = _mix_fwd(lg, x, q, k, v, g, yb, conv, w_out_b, row(ret_norm_g[i]),
                                        row(post_mix_g[i]), row(pre_ffn_g[i]), g1, sc2, sh2,
                                        wr_cat, br, s_f)
    gate5, idx5 = _route(logits_t, cap)
    gate = gate5.transpose(0, 1, 3, 2, 4).reshape(bsz * n_exp, 1, cap)
    idx = idx5.transpose(0, 1, 3, 2, 4).reshape(bsz * n_exp, 1, cap)

    xs = _dispatch(idx, h_tiles.reshape(bsz, n * ROW_TILE, LANES), n_exp, cap)
    y = _ffn(xs, w_gate[i], w_up[i], w_down[i])
    return _combine(idx, gate, y.reshape(bsz, n_exp, cap * ROW_TILE, LANES), x_mid,
                    row(post_ffn_g[i]), g2)
```

```python
import functools

import jax
import jax.numpy as jnp
from jax import lax
from jax.experimental import pallas as pl
from jax.experimental.pallas import tpu as pltpu

F32 = jnp.float32
BF16 = jnp.bfloat16

GRID_W = 64
RET_HEADS = 4
ROPE_BASE = 10000.0
EC_CAPACITY_FACTOR = 2
RMS_EPS = 1e-6
GN_EPS = 1e-6

LANES = 128
SUBLANES = 8
ROW_TILE = SUBLANES
GROUP = SUBLANES * ROW_TILE
VMEM_CAP = 60000 * 1024

TOKEN_BLOCK = 512
PROJ_BLOCK = 1024
RET_CHUNK = 256
FFN_TILE = 512
SCATTER_UNROLL = 2 * SUBLANES


def _vmem(nbytes):
    return min(int(nbytes), VMEM_CAP)


def _rms(x, g):
    return x * lax.rsqrt(jnp.mean(x * x, axis=-1, keepdims=True) + RMS_EPS) * g


def _silu(x):
    return x * (1.0 / (1.0 + jnp.exp(-x)))


def _rot_half(t):
    return pltpu.roll(t, LANES // 2, axis=1)


def _store_tile_major(ref, lead, val):
    rows = val.shape[0]
    for c in range(ROW_TILE):
        ref[lead + (slice(None), slice(c * SUBLANES, (c + 1) * SUBLANES), slice(None))] = (
            val[:, c * LANES:(c + 1) * LANES].reshape(rows // SUBLANES, SUBLANES, LANES))


def _row_base(r):
    return (r >> 3) * GROUP + (r & (SUBLANES - 1))


def _modulation_kernel(cond_ref, w_ref, b_ref, o_ref):
    s = _silu(cond_ref[...])
    o_ref[...] = jnp.dot(s, w_ref[...], preferred_element_type=F32,
                         precision=lax.Precision.HIGHEST) + b_ref[...]


def _modulation(cond, w_ada, b_ada):
    rows, d = cond.shape
    cols = w_ada.shape[1]
    tile = cols // 4
    return pl.pallas_call(
        _modulation_kernel,
        grid=(cols // tile,),
        in_specs=[
            pl.BlockSpec((rows, d), lambda j: (0, 0)),
            pl.BlockSpec((d, tile), lambda j: (0, j)),
            pl.BlockSpec((1, tile), lambda j: (0, j)),
        ],
        out_specs=pl.BlockSpec((rows, tile), lambda j: (0, j)),
        out_shape=jax.ShapeDtypeStruct((rows, cols), F32),
        compiler_params=pltpu.CompilerParams(
            dimension_semantics=("arbitrary",),
            vmem_limit_bytes=_vmem(3 * d * tile * 4)),
        name="modulation",
    )(cond, w_ada, b_ada.reshape(1, cols))


def _context_kernel(lg_ref, ctx_ref, gpre_ref, sc_ref, sh_ref, wk_ref, wv_ref,
                    cos_ref, sin_ref, sf_ref, sb_ref):
    n = ctx_ref.shape[1]
    hd = LANES
    hc = _rms(ctx_ref[0], gpre_ref[...]) * (1.0 + sc_ref[...]) + sh_ref[...]
    hc = hc.astype(BF16)
    k = jnp.dot(hc, wk_ref[...], preferred_element_type=F32)
    v = jnp.dot(hc, wv_ref[...], preferred_element_type=F32)
    cos = cos_ref[...]
    sin = sin_ref[...]
    pos = lax.broadcasted_iota(jnp.int32, (n, 1), 0).astype(F32)
    for h in range(RET_HEADS):
        kh = k[:, h * hd:(h + 1) * hd]
        kr = (kh * cos + _rot_half(kh) * sin) * (hd ** -0.5)
        vh = v[:, h * hd:(h + 1) * hd].astype(BF16)
        wf = jnp.exp(lg_ref[0, h] * (n - 1.0 - pos))
        wb = jnp.exp(lg_ref[1, h] * pos)
        kf = (kr * wf).T.astype(BF16)
        kb = (kr * wb).T.astype(BF16)
        sf_ref[0, h] = jnp.dot(kf, vh, preferred_element_type=F32)
        sb_ref[0, h] = jnp.dot(kb, vh, preferred_element_type=F32)


def _context_states(lg, ctx, gpre, csc1, csh1, w_in, k_col, cos, sin):
    b, n, d = ctx.shape
    hd = LANES
    rw = RET_HEADS * hd
    st = jax.ShapeDtypeStruct((b, RET_HEADS, hd, hd), F32)
    vec = pl.BlockSpec((1, d), lambda i: (0, 0))
    return pl.pallas_call(
        _context_kernel,
        grid=(b,),
        in_specs=[
            pl.BlockSpec(memory_space=pltpu.SMEM),
            pl.BlockSpec((1, n, d), lambda i: (i, 0, 0)),
            vec, vec, vec,
            pl.BlockSpec((d, rw), lambda i: (0, k_col)),
            pl.BlockSpec((d, rw), lambda i: (0, k_col + 1)),
            pl.BlockSpec((n, hd), lambda i: (0, 0)),
            pl.BlockSpec((n, hd), lambda i: (0, 0)),
        ],
        out_specs=[pl.BlockSpec((1, RET_HEADS, hd, hd), lambda i: (i, 0, 0, 0))] * 2,
        out_shape=[st, st],
        compiler_params=pltpu.CompilerParams(dimension_semantics=("arbitrary",)),
        name="context",
    )(lg, ctx, gpre, csc1, csh1, w_in, w_in, cos, sin)


def _fill_tables(lg_ref, direction, decay_ref, xi_ref, zeta_ref):
    c = decay_ref.shape[1]
    i = lax.broadcasted_iota(jnp.int32, (c, c), 0)
    j = lax.broadcasted_iota(jnp.int32, (c, c), 1)
    col = lax.broadcasted_iota(jnp.int32, (c, 1), 0).astype(F32)
    for h in range(RET_HEADS):
        lg = lg_ref[direction, h]
        if direction == 0:
            rel = (i - j).astype(F32)
            mask = i >= j
            xi_ref[h] = jnp.exp(lg * (col + 1.0))
            zeta_ref[h] = jnp.exp(lg * (c - 1.0 - col))
        else:
            rel = (j - i).astype(F32)
            mask = j > i
            xi_ref[h] = jnp.exp(lg * (c - col))
            zeta_ref[h] = jnp.exp(lg * col)
        decay_ref[h] = jnp.where(mask, jnp.exp(lg * jnp.where(mask, rel, 0.0)), 0.0)


def _retention_block(q_blk, k_blk_f32, v_blk, lg, s_ref, decay_ref, xi_ref, zeta_ref, h,
                     reverse):
    c = decay_ref.shape[1]
    n_chunks = q_blk.shape[0] // c
    outs = [None] * n_chunks
    for ci in (reversed(range(n_chunks)) if reverse else range(n_chunks)):
        rows = slice(ci * c, (ci + 1) * c)
        qh, kh_f32, vh = q_blk[rows], k_blk_f32[rows], v_blk[rows]
        kh = kh_f32.astype(BF16)
        scores = lax.dot_general(qh, kh, (((1,), (1,)), ((), ())), preferred_element_type=F32)
        s_prev = s_ref[h]
        lhs = jnp.concatenate([(scores * decay_ref[h]).astype(BF16),
                               (qh.astype(F32) * xi_ref[h]).astype(BF16)], axis=1)
        rhs = jnp.concatenate([vh, s_prev.astype(BF16)], axis=0)
        outs[ci] = jnp.dot(lhs, rhs, preferred_element_type=F32)
        kz = (kh_f32 * zeta_ref[h]).T.astype(BF16)
        s_ref[h] = jnp.exp(lg * c) * s_prev + jnp.dot(kz, vh, preferred_element_type=F32)
    return outs[0] if n_chunks == 1 else jnp.concatenate(outs, axis=0)


def _mix_bwd_kernel(lg_ref, x_ref, gpre_ref, sc_ref, sh_ref, win_ref, cw_ref, cb_ref,
                    cos_ref, sin_ref, s0_ref,
                    q_ref, k_ref, v_ref, g_ref, yb_ref, conv_ref,
                    s_ref, decay_ref, xi_ref, zeta_ref):
    t = x_ref.shape[1]
    hd = LANES
    cw = conv_ref.shape[2]
    rw = RET_HEADS * hd

    @pl.when((pl.program_id(0) == 0) & (pl.program_id(1) == 0))
    def _():
        _fill_tables(lg_ref, 1, decay_ref, xi_ref, zeta_ref)

    @pl.when(pl.program_id(1) == 0)
    def _():
        s_ref[...] = s0_ref[0]

    hx = (_rms(x_ref[0], gpre_ref[...]) * (1.0 + sc_ref[0]) + sh_ref[0]).astype(BF16)
    off = 3 * cw

    def project(c0, width):
        return jnp.dot(hx, win_ref[:, c0:c0 + width], preferred_element_type=F32)

    g_ref[0] = project(off + 3 * rw, rw)
    v_bf = project(off + 2 * rw, rw).astype(BF16)
    v_ref[0] = v_bf
    cos = cos_ref[...]
    sin = sin_ref[...]
    q = project(off, rw)
    qr = [(q[:, h * hd:(h + 1) * hd] * cos + _rot_half(q[:, h * hd:(h + 1) * hd]) * sin)
          .astype(BF16) for h in range(RET_HEADS)]
    k = project(off + rw, rw)
    kr = [(k[:, h * hd:(h + 1) * hd] * cos + _rot_half(k[:, h * hd:(h + 1) * hd]) * sin)
          * (hd ** -0.5) for h in range(RET_HEADS)]
    u = project(cw, cw) * project(2 * cw, cw)
    x_b = project(0, cw)

    cq = cw // RET_HEADS
    col = lax.broadcasted_iota(jnp.int32, (t, cq), 0) & (GRID_W - 1)
    for h in range(RET_HEADS):
        sl = slice(h * hd, (h + 1) * hd)
        q_ref[0, :, sl] = qr[h]
        k_ref[0, :, sl] = kr[h].astype(BF16)
        yb_ref[0, :, sl] = _retention_block(qr[h], kr[h], v_bf[:, sl], lg_ref[1, h], s_ref,
                                            decay_ref, xi_ref, zeta_ref, h, reverse=True)
        cs = slice(h * cq, (h + 1) * cq)
        uq = u[:, cs]
        u_prev = jnp.where(col == 0, 0.0, pltpu.roll(uq, 1, axis=0))
        u_next = jnp.where(col == GRID_W - 1, 0.0, pltpu.roll(uq, t - 1, axis=0))
        y = (u_prev * cw_ref[0:1, cs] + uq * cw_ref[1:2, cs] + u_next * cw_ref[2:3, cs]
             + cb_ref[:, cs])
        conv_ref[0, :, cs] = (x_b[:, cs] * y).astype(BF16)


def _mix_bwd(lg, x, gpre, sc1, sh1, w_in, conv_w, conv_b, cos, sin, s0_b):
    b, n, d = x.shape
    t = PROJ_BLOCK
    nb = n // t
    hd = LANES
    cw = conv_w.shape[1]
    rw = RET_HEADS * hd
    rev = lambda i, j: (i, nb - 1 - j, 0)
    vec = pl.BlockSpec((1, d), lambda i, j: (0, 0))
    bvec = pl.BlockSpec((1, 1, d), lambda i, j: (i, 0, 0))
    tok = lambda w: pl.BlockSpec((1, t, w), rev)
    rope = pl.BlockSpec((t, hd), lambda i, j: (nb - 1 - j, 0))
    out = lambda w, dt: jax.ShapeDtypeStruct((b, n, w), dt)
    return pl.pallas_call(
        _mix_bwd_kernel,
        grid=(b, nb),
        in_specs=[
            pl.BlockSpec(memory_space=pltpu.SMEM),
            tok(d), vec, bvec, bvec,
            pl.BlockSpec(w_in.shape, lambda i, j: (0, 0), pipeline_mode=pl.Buffered(1)),
            pl.BlockSpec(conv_w.shape, lambda i, j: (0, 0)),
            pl.BlockSpec((1, cw), lambda i, j: (0, 0)),
            rope, rope,
            pl.BlockSpec((1, RET_HEADS, hd, hd), lambda i, j: (i, 0, 0, 0)),
        ],
        out_specs=[tok(rw), tok(rw), tok(rw), tok(rw), tok(rw), tok(cw)],
        out_shape=[out(rw, BF16), out(rw, BF16), out(rw, BF16), out(rw, F32),
                   out(rw, F32), out(cw, BF16)],
        scratch_shapes=[pltpu.VMEM((RET_HEADS, hd, hd), F32),
                        pltpu.VMEM((RET_HEADS, RET_CHUNK, RET_CHUNK), F32),
                        pltpu.VMEM((RET_HEADS, RET_CHUNK, 1), F32),
                        pltpu.VMEM((RET_HEADS, RET_CHUNK, 1), F32)],
        compiler_params=pltpu.CompilerParams(
            dimension_semantics=("arbitrary", "arbitrary"),
            vmem_limit_bytes=_vmem(56 * 2 ** 20)),
        name="mix_bwd",
    )(lg, x, gpre, sc1, sh1, w_in, conv_w, conv_b, cos, sin, s0_b)


def _mix_fwd_kernel(lg_ref, x_ref, q_ref, k_ref, v_ref, g_ref, yb_ref, conv_ref,
                    wout_ref, gn_ref, gpost_ref, gffn_ref, g1_ref, sc_ref, sh_ref,
                    wr_ref, br_ref, s0_ref,
                    xmid_ref, h2_ref, logit_ref, s_ref, mix_ref, prev_ref,
                    decay_ref, xi_ref, zeta_ref, *, nb):
    hd = LANES
    cw = conv_ref.shape[2]
    step = pl.program_id(0)

    @pl.when(step == 0)
    def _():
        prev_ref[...] = jnp.zeros(prev_ref.shape, BF16)
        _fill_tables(lg_ref, 0, decay_ref, xi_ref, zeta_ref)

    @pl.when(step % nb == 0)
    def _():
        s_ref[...] = s0_ref[0]

    t = x_ref.shape[1]
    ne = logit_ref.shape[2]
    mix = jnp.dot(prev_ref[...], wout_ref[...], preferred_element_type=F32)

    def finish_rows(r0, r1):
        x_mid = x_ref[0, r0:r1] + g1_ref[0] * _rms(mix[r0:r1], gpost_ref[...])
        xmid_ref[0, r0:r1] = x_mid
        h2 = _rms(x_mid, gffn_ref[...]) * (1.0 + sc_ref[0]) + sh_ref[0]
        h_hi = h2.astype(BF16)
        h_lo = (h2 - h_hi.astype(F32)).astype(BF16)
        nt = (((1,), (1,)), ((), ()))
        p = (lax.dot_general(wr_ref[...], h_hi, nt, preferred_element_type=F32)
             + lax.dot_general(wr_ref[...], h_lo, nt, preferred_element_type=F32))
        logit_ref[0, r0 // LANES] = p[0:ne] + p[ne:2 * ne] + br_ref[...]
        for c in range(ROW_TILE):
            h2_ref[0, r0 // SUBLANES:r1 // SUBLANES, c * SUBLANES:(c + 1) * SUBLANES, :] = (
                h2[:, c * LANES:(c + 1) * LANES].reshape((r1 - r0) // SUBLANES, SUBLANES, LANES))

    mix_ref[:, 0:cw] = conv_ref[0]
    rows = t // RET_HEADS
    for h in range(RET_HEADS):
        sl = slice(h * hd, (h + 1) * hd)
        y = _retention_block(q_ref[0, :, sl], k_ref[0, :, sl].astype(F32), v_ref[0, :, sl],
                             lg_ref[0, h], s_ref, decay_ref, xi_ref, zeta_ref, h, reverse=False)
        finish_rows(h * rows, (h + 1) * rows)
        y = y + yb_ref[0, :, sl]
        mu = jnp.mean(y, axis=-1, keepdims=True)
        yc = y - mu
        var = jnp.mean(yc * yc, axis=-1, keepdims=True)
        yn = yc * lax.rsqrt(var + GN_EPS) * gn_ref[:, sl]
        mix_ref[:, cw + h * hd:cw + (h + 1) * hd] = (
            _silu(g_ref[0, :, sl]) * yn).astype(BF16)
    prev_ref[...] = mix_ref[...]


def _mix_fwd(lg, x, q, k, v, g, yb, conv, w_out, gn, gpost, gffn, g1, sc2, sh2, wr, br, s0_f):
    b, n, d = x.shape
    t = TOKEN_BLOCK
    nb = n // t
    hd = LANES
    cw = conv.shape[2]
    rw = RET_HEADS * hd
    ne = br.shape[0]
    assert t // RET_HEADS == LANES
    steps = b * nb
    ret = lambda s: jnp.minimum(s, steps - 1)
    outp = lambda s: jnp.maximum(s - 1, 0)
    vec = lambda w: pl.BlockSpec((1, w), lambda s: (0, 0))
    bvec = pl.BlockSpec((1, 1, d), lambda s: (outp(s) // nb, 0, 0))
    rtok = lambda w: pl.BlockSpec((1, t, w), lambda s: (ret(s) // nb, ret(s) % nb, 0))
    otok = lambda w: pl.BlockSpec((1, t, w), lambda s: (outp(s) // nb, outp(s) % nb, 0))
    return pl.pallas_call(
        functools.partial(_mix_fwd_kernel, nb=nb),
        grid=(steps + 1,),
        in_specs=[
            pl.BlockSpec(memory_space=pltpu.SMEM),
            otok(d), rtok(rw), rtok(rw), rtok(rw), rtok(rw), rtok(rw), rtok(cw),
            pl.BlockSpec(w_out.shape, lambda s: (0, 0)),
            vec(rw), vec(d), vec(d), bvec, bvec, bvec,
            pl.BlockSpec(wr.shape, lambda s: (0, 0)),
            pl.BlockSpec((ne, 1), lambda s: (0, 0)),
            pl.BlockSpec((1, RET_HEADS, hd, hd), lambda s: (ret(s) // nb, 0, 0, 0)),
        ],
        out_specs=[otok(d),
                   pl.BlockSpec((1, t // SUBLANES, GROUP, LANES),
                                lambda s: (outp(s) // nb, outp(s) % nb, 0, 0)),
                   pl.BlockSpec((1, t // LANES, ne, LANES),
                                lambda s: (outp(s) // nb, outp(s) % nb, 0, 0))],
        out_shape=[jax.ShapeDtypeStruct((b, n, d), F32),
                   jax.ShapeDtypeStruct((b, n // SUBLANES, GROUP, LANES), F32),
                   jax.ShapeDtypeStruct((b, n // LANES, ne, LANES), F32)],
        scratch_shapes=[pltpu.VMEM((RET_HEADS, hd, hd), F32),
                        pltpu.VMEM((t, cw + rw), BF16),
                        pltpu.VMEM((t, cw + rw), BF16),
                        pltpu.VMEM((RET_HEADS, RET_CHUNK, RET_CHUNK), F32),
                        pltpu.VMEM((RET_HEADS, RET_CHUNK, 1), F32),
                        pltpu.VMEM((RET_HEADS, RET_CHUNK, 1), F32)],
        compiler_params=pltpu.CompilerParams(
            dimension_semantics=("arbitrary",),
            vmem_limit_bytes=_vmem(48 * 2 ** 20)),
        name="mix_fwd",
    )(lg, x, q, k, v, g, yb, conv, w_out, gn, gpost, gffn, g1, sc2, sh2, wr, br, s0_f)


def _route_kernel(logit_ref, gate_ref, idx_ref, *, cap):
    nsamp, nv, ne, _ = logit_ref.shape
    grp = gate_ref.shape[2]
    n_grp = nsamp * (ne // grp)
    keys = []
    for b in range(nsamp):
        x = logit_ref[b]
        m = jnp.max(x, axis=1, keepdims=True)
        s = jnp.sum(jnp.exp(x - m), axis=1, keepdims=True)
        keys += [jnp.exp(logit_ref[b, :, g * grp:(g + 1) * grp, :] - m) / s
                 for g in range(ne // grp)]

    top = cap // LANES

    def coords(nvreg):
        shape = (nvreg, grp, LANES)
        vi = lax.broadcasted_iota(jnp.int32, shape, 0)
        li = lax.broadcasted_iota(jnp.int32, shape, 2)
        return vi, li, (vi // top) * cap + li * top + vi % top

    def better(pk, pi, k, i):
        return (pk > k) | ((pk == k) & (pi < i))

    def stage(k, i, li, pos, dist, span):
        if dist < top:
            halves = lambda a: a.reshape(a.shape[0] // (2 * dist), 2, dist, grp, LANES)
            k5, i5 = halves(k), halves(i)
            k_lo, k_hi, i_lo, i_hi = k5[:, 0], k5[:, 1], i5[:, 0], i5[:, 1]
            hi_wins = better(k_hi, i_hi, k_lo, i_lo)
            k_b, k_w = jnp.where(hi_wins, k_hi, k_lo), jnp.where(hi_wins, k_lo, k_hi)
            i_b, i_w = jnp.where(hi_wins, i_hi, i_lo), jnp.where(hi_wins, i_lo, i_hi)
            if span is not None:
                desc = (halves(pos)[:, 0] & span) == 0
                k_b, k_w = jnp.where(desc, k_b, k_w), jnp.where(desc, k_w, k_b)
                i_b, i_w = jnp.where(desc, i_b, i_w), jnp.where(desc, i_w, i_b)
            join = lambda lo, hi: jnp.concatenate([lo[:, None], hi[:, None]], axis=1).reshape(k.shape)
            return join(k_b, k_w), join(i_b, i_w)
        sh = dist // top
        low = (li & sh) == 0
        pk = jnp.where(low, pltpu.roll(k, LANES - sh, axis=2), pltpu.roll(k, sh, axis=2))
        pi = jnp.where(low, pltpu.roll(i, LANES - sh, axis=2), pltpu.roll(i, sh, axis=2))
        want_best = low if span is None else low == ((pos & span) == 0)
        take = better(pk, pi, k, i) == want_best
        return jnp.where(take, pk, k), jnp.where(take, pi, i)

    def stage_all(ks, ids, li, pos, dist, span):
        out = [stage(k, i, li, pos, dist, span) for k, i in zip(ks, ids)]
        return [o[0] for o in out], [o[1] for o in out]

    vi, li, pos = coords(nv)
    idxs = [vi * LANES + li] * n_grp
    span = 2
    while span <= cap:
        dist = span // 2
        while dist >= 1:
            keys, idxs = stage_all(keys, idxs, li, pos, dist, span)
            dist //= 2
        span *= 2
    nblk = nv // top
    while nblk > 1:
        for g in range(n_grp):
            k5 = keys[g].reshape(nblk // 2, 2, top, grp, LANES)
            i5 = idxs[g].reshape(nblk // 2, 2, top, grp, LANES)
            k0, k1, i0, i1 = k5[:, 0], k5[:, 1], i5[:, 0], i5[:, 1]
            take = better(k1, i1, k0, i0)
            keys[g] = jnp.where(take, k1, k0).reshape(nblk // 2 * top, grp, LANES)
            idxs[g] = jnp.where(take, i1, i0).reshape(nblk // 2 * top, grp, LANES)
        nblk //= 2
        vi, li, pos = coords(nblk * top)
        dist = cap // 2
        while dist >= 1:
            keys, idxs = stage_all(keys, idxs, li, pos, dist, None if nblk == 1 else cap)
            dist //= 2
    for g in range(n_grp):
        gate_ref[g] = keys[g][0:top]
        idx_ref[g] = _row_base(idxs[g][0:top])


def _route(logits_t, cap):
    b, nv, ne, _ = logits_t.shape
    grp = SUBLANES
    top = cap // LANES
    n_grp = b * (ne // grp)
    out_block = pl.BlockSpec((n_grp, top, grp, LANES), lambda i: (0, 0, 0, 0))
    return pl.pallas_call(
        functools.partial(_route_kernel, cap=cap),
        grid=(1,),
        in_specs=[pl.BlockSpec((b, nv, ne, LANES), lambda i: (0, 0, 0, 0))],
        out_specs=[out_block, out_block],
        out_shape=[jax.ShapeDtypeStruct((n_grp, top, grp, LANES), F32),
                   jax.ShapeDtypeStruct((n_grp, top, grp, LANES), jnp.int32)],
        compiler_params=pltpu.CompilerParams(dimension_semantics=("arbitrary",)),
        name="route",
    )(logits_t)


BF16_ROWS = 2 * SUBLANES
GATHER_STAGES = 4


def _dispatch_kernel(idx_ref, h_ref, xs_ref, *stage_refs):
    cap = idx_ref.shape[2]
    per_iter = BF16_ROWS * len(stage_refs)

    def body(r, carry):
        for s, stage_ref in enumerate(stage_refs):
            for u in range(BF16_ROWS):
                src = idx_ref[0, 0, r * per_iter + s * BF16_ROWS + u]
                dst = (u // SUBLANES) * GROUP + u % SUBLANES
                stage_ref[pl.ds(dst, ROW_TILE, stride=SUBLANES), :] = (
                    h_ref[0, pl.ds(src, ROW_TILE, stride=SUBLANES), :])
        for s, stage_ref in enumerate(stage_refs):
            row0 = pl.multiple_of(r * per_iter + s * BF16_ROWS, BF16_ROWS)
            for c in range(ROW_TILE):
                lo = stage_ref[c * SUBLANES:(c + 1) * SUBLANES, :]
                hi = stage_ref[GROUP + c * SUBLANES:GROUP + (c + 1) * SUBLANES, :]
                xs_ref[0, 0, pl.ds(row0, BF16_ROWS), c * LANES:(c + 1) * LANES] = (
                    jnp.concatenate([lo, hi], axis=0).astype(BF16))
        return carry

    lax.fori_loop(0, cap // per_iter, body, 0)


def _dispatch(idx, h_rows, n_exp, cap):
    b = h_rows.shape[0]
    rows = h_rows.shape[1]
    d = ROW_TILE * LANES
    return pl.pallas_call(
        _dispatch_kernel,
        grid=(b, n_exp),
        in_specs=[
            pl.BlockSpec((1, 1, cap), lambda i, e: (i * n_exp + e, 0, 0),
                         memory_space=pltpu.SMEM),
            pl.BlockSpec((1, rows, LANES), lambda i, e: (i, 0, 0),
                         pipeline_mode=pl.Buffered(1)),
        ],
        out_specs=pl.BlockSpec((1, 1, cap, d), lambda i, e: (i, e, 0, 0)),
        out_shape=jax.ShapeDtypeStruct((b, n_exp, cap, d), BF16),
        scratch_shapes=[pltpu.VMEM((2 * GROUP, LANES), F32)] * GATHER_STAGES,
        compiler_params=pltpu.CompilerParams(
            dimension_semantics=("arbitrary", "arbitrary"),
            vmem_limit_bytes=_vmem(rows * LANES * 4 + 4 * cap * d * 2)),
        name="dispatch",
    )(idx, h_rows)


def _ffn_kernel(xs_ref, wg_ref, wu_ref, wd_ref, y_ref):
    nb = xs_ref.shape[0]
    tf = wg_ref.shape[2]

    @pl.when(pl.program_id(1) == 0)
    def _():
        y_ref[...] = jnp.zeros(y_ref.shape, F32)

    wgu = jnp.concatenate([wg_ref[0].astype(BF16), wu_ref[0].astype(BF16)], axis=1)
    wd = wd_ref[0].astype(BF16)
    for b in range(nb):
        au = jnp.dot(xs_ref[b, 0], wgu, preferred_element_type=F32)
        hmid = (_silu(au[:, 0:tf]) * au[:, tf:2 * tf]).astype(BF16)
        contrib = jnp.dot(hmid, wd, preferred_element_type=F32)
        rows = contrib.shape[0]
        for c in range(ROW_TILE):
            y_ref[b, 0, :, c * SUBLANES:(c + 1) * SUBLANES, :] += (
                contrib[:, c * LANES:(c + 1) * LANES].reshape(rows // SUBLANES, SUBLANES, LANES))


def _ffn(xs, w_gate, w_up, w_down):
    b, n_exp, cap, d = xs.shape
    dff = w_gate.shape[2]
    tf = FFN_TILE
    groups = cap // SUBLANES
    return pl.pallas_call(
        _ffn_kernel,
        grid=(n_exp, dff // tf),
        in_specs=[
            pl.BlockSpec((b, 1, cap, d), lambda e, f: (0, e, 0, 0)),
            pl.BlockSpec((1, d, tf), lambda e, f: (e, 0, f)),
            pl.BlockSpec((1, d, tf), lambda e, f: (e, 0, f)),
            pl.BlockSpec((1, tf, d), lambda e, f: (e, f, 0)),
        ],
        out_specs=pl.BlockSpec((b, 1, groups, GROUP, LANES), lambda e, f: (0, e, 0, 0, 0)),
        out_shape=jax.ShapeDtypeStruct((b, n_exp, groups, GROUP, LANES), F32),
        compiler_params=pltpu.CompilerParams(
            dimension_semantics=("arbitrary", "arbitrary"),
            vmem_limit_bytes=_vmem(56 * 2 ** 20)),
        name="ffn",
    )(xs, w_gate, w_up, w_down)


def _combine_kernel(idx_ref, gate_ref, y_ref, xmid_ref, gpost_ref, g2_ref, o_ref, acc_ref,
                    *, n_exp):
    cap = idx_ref.shape[2]
    t = xmid_ref.shape[1]
    step = pl.program_id(1)

    @pl.when(step == 0)
    def _():
        zrows = SUBLANES * GROUP

        def zero(r, carry):
            acc_ref[pl.ds(pl.multiple_of(r * zrows, zrows), zrows), :] = jnp.zeros(
                (zrows, LANES), F32)
            return carry

        lax.fori_loop(0, acc_ref.shape[0] // zrows, zero, 0)

    @pl.when(step < n_exp)
    def _():
        def body(r, carry):
            offs, vals = [], []
            for u in range(SCATTER_UNROLL):
                dst = idx_ref[0, 0, r * SCATTER_UNROLL + u]
                gate = gate_ref[0, 0, r * SCATTER_UNROLL + u]
                src = (r * (SCATTER_UNROLL // SUBLANES) + u // SUBLANES) * GROUP + u % SUBLANES
                offs.append(dst)
                vals.append(acc_ref[pl.ds(dst, ROW_TILE, stride=SUBLANES), :]
                            + gate * y_ref[0, 0, pl.ds(src, ROW_TILE, stride=SUBLANES), :])
            for dst, val in zip(offs, vals):
                acc_ref[pl.ds(dst, ROW_TILE, stride=SUBLANES), :] = val
            return carry

        lax.fori_loop(0, cap // SCATTER_UNROLL, body, 0)

    @pl.when(step >= n_exp)
    def _():
        rows = t * ROW_TILE
        start = pl.multiple_of((step - n_exp) * rows, rows)
        blk = acc_ref[pl.ds(start, rows), :].reshape(t // SUBLANES, GROUP, LANES)
        ffn = jnp.concatenate(
            [blk[:, c * SUBLANES:(c + 1) * SUBLANES, :].reshape(t, LANES)
             for c in range(ROW_TILE)], axis=-1)
        o_ref[0] = xmid_ref[0] + g2_ref[0] * _rms(ffn, gpost_ref[...])


def _combine(idx, gate, y, x_mid, gpost, g2):
    b, n_exp, rows, _ = y.shape
    n, d = x_mid.shape[1:]
    cap = rows // ROW_TILE
    t = TOKEN_BLOCK
    last = n_exp - 1
    tok_block = pl.BlockSpec((1, t, d), lambda i, s: (i, jnp.maximum(s - n_exp, 0), 0))
    per_expert = pl.BlockSpec((1, 1, cap), lambda i, s: (i * n_exp + jnp.minimum(s, last), 0, 0),
                              memory_space=pltpu.SMEM)
    return pl.pallas_call(
        functools.partial(_combine_kernel, n_exp=n_exp),
        grid=(b, n_exp + n // t),
        in_specs=[
            per_expert, per_expert,
            pl.BlockSpec((1, 1, rows, LANES), lambda i, s: (i, jnp.minimum(s, last), 0, 0)),
            tok_block,
            pl.BlockSpec((1, d), lambda i, s: (0, 0)),
            pl.BlockSpec((1, 1, d), lambda i, s: (i, 0, 0)),
        ],
        out_specs=tok_block,
        out_shape=jax.ShapeDtypeStruct((b, n, d), F32),
        scratch_shapes=[pltpu.VMEM((n * ROW_TILE, LANES), F32)],
        compiler_params=pltpu.CompilerParams(
            dimension_semantics=("arbitrary", "arbitrary"),
            vmem_limit_bytes=_vmem(n * d * 4 + 2 * rows * LANES * 4 + 8 * t * d * 4)),
        name="combine",
    )(idx, gate, y, x_mid, gpost, g2)


def _rope_tables(n_pos):
    half = LANES // 2
    inv = 1.0 / (ROPE_BASE ** (jnp.arange(half, dtype=F32) / half))
    ang_a = (jnp.arange(n_pos // LANES, dtype=F32) * LANES)[:, None] * inv[None, :]
    ang_b = jnp.arange(LANES, dtype=F32)[:, None] * inv[None, :]
    ca, sa = jnp.cos(ang_a)[:, None, :], jnp.sin(ang_a)[:, None, :]
    cb, sb = jnp.cos(ang_b)[None, :, :], jnp.sin(ang_b)[None, :, :]
    cos = (ca * cb - sa * sb).reshape(n_pos, half)
    sin = (sa * cb + ca * sb).reshape(n_pos, half)
    return jnp.concatenate([cos, cos], axis=-1), jnp.concatenate([-sin, sin], axis=-1)


def kernel(x, c, ctx, c_ctx, w_ada, b_ada, pre_mix_g, post_mix_g, pre_ffn_g, post_ffn_g,
           w_in, conv_w, conv_b, ret_decay_logit, ret_norm_g, w_out,
           w_router, b_router, w_gate, w_up, w_down):
    bsz, n, d = x.shape
    ctx_len = ctx.shape[1]
    depth = w_ada.shape[0]
    n_exp = w_router.shape[2]
    cw = conv_w.shape[2]
    rw = RET_HEADS * LANES
    cap = max(1, EC_CAPACITY_FACTOR * n // n_exp)
    assert depth == 1, "only the single-layer block is implemented"
    assert d == ROW_TILE * LANES and n % TOKEN_BLOCK == 0 and TOKEN_BLOCK % LANES == 0
    assert n % GRID_W == 0 and TOKEN_BLOCK % GRID_W == 0 and ctx_len % SUBLANES == 0
    assert n_exp % SUBLANES == 0 and cap % LANES == 0 and (n // LANES) & (n // LANES - 1) == 0
    assert w_in.shape[2] == 3 * cw + 4 * rw

    assert (ctx_len + n) % LANES == 0
    cos_all, sin_all = _rope_tables(ctx_len + n)
    cos_c, sin_c = cos_all[:ctx_len], sin_all[:ctx_len]
    cos_x, sin_x = cos_all[ctx_len:], sin_all[ctx_len:]
    i = 0

    pad = -(bsz + 1) % SUBLANES
    cond = jnp.concatenate([c, c_ctx[None, :], jnp.zeros((pad, d), F32)], axis=0)
    mod = _modulation(cond, pltpu.with_memory_space_constraint(w_ada[i], pltpu.HBM), b_ada[i])
    sh1, sc1, g1, sh2, sc2, g2 = [mod[:bsz, j * d:(j + 1) * d].reshape(bsz, 1, d) for j in range(6)]
    csh1 = mod[bsz:bsz + 1, 0:d]
    csc1 = mod[bsz:bsz + 1, d:2 * d]

    lg = jax.nn.log_sigmoid(ret_decay_logit[i].astype(F32))
    w_in_b = w_in[i].astype(BF16)
    w_out_b = w_out[i].astype(BF16)
    k_off = 3 * cw + rw
    row = lambda a: a.reshape(1, -1)

    assert k_off % rw == 0
    s_f, s_b = _context_states(lg, ctx, row(pre_mix_g[i]), csc1, csh1, w_in_b, k_off // rw,
                               cos_c, sin_c)

    q, k, v, g, yb, conv = _mix_bwd(lg, x, row(pre_mix_g[i]), sc1, sh1, w_in_b,
                                    conv_w[i], row(conv_b[i]), cos_x, sin_x, s_b)

    wr = w_router[i].astype(F32).T
    wr_hi = wr.astype(BF16)
    wr_lo = (wr - wr_hi.astype(F32)).astype(BF16)
    wr_cat = jnp.concatenate([wr_hi, wr_lo], axis=0)
    br = b_router[i].astype(F32).reshape(n_exp, 1)
    x_mid, h_tiles, logits_t = _mix_fwd(lg, x, q, k, v, g, yb, conv, w_out_b, row(ret_norm_g[i]),
                                        row(post_mix_g[i]), row(pre_ffn_g[i]), g1, sc2, sh2,
                                        wr_cat, br, s_f)
    gate4, idx4 = _route(logits_t, cap)
    gate = gate4.transpose(0, 2, 1, 3).reshape(bsz * n_exp, 1, cap)
    idx = idx4.transpose(0, 2, 1, 3).reshape(bsz * n_exp, 1, cap)

    xs = _dispatch(idx, h_tiles.reshape(bsz, n * ROW_TILE, LANES), n_exp, cap)
    y = _ffn(xs, w_gate[i], w_up[i], w_down[i])
    return _combine(idx, gate, y.reshape(bsz, n_exp, cap * ROW_TILE, LANES), x_mid,
                    row(post_ffn_g[i]), g2)
```

```python
import functools

import jax
import jax.numpy as jnp
from jax import lax
from jax.experimental import pallas as pl
from jax.experimental.pallas import tpu as pltpu

F32 = jnp.float32
BF16 = jnp.bfloat16

GRID_W = 64
RET_HEADS = 4
ROPE_BASE = 10000.0
EC_CAPACITY_FACTOR = 2
RMS_EPS = 1e-6
GN_EPS = 1e-6

LANES = 128
SUBLANES = 8
ROW_TILE = SUBLANES
GROUP = SUBLANES * ROW_TILE
VMEM_CAP = 60000 * 1024

TOKEN_BLOCK = 512
PROJ_BLOCK = 1024
RET_CHUNK = 256
FFN_TILE = 512
SCATTER_UNROLL = 2 * SUBLANES


def _vmem(nbytes):
    return min(int(nbytes), VMEM_CAP)


def _rms(x, g):
    return x * lax.rsqrt(jnp.mean(x * x, axis=-1, keepdims=True) + RMS_EPS) * g


def _silu(x):
    return x * (1.0 / (1.0 + jnp.exp(-x)))


def _rot_half(t):
    return pltpu.roll(t, LANES // 2, axis=1)


def _row_base(r):
    return (r // SUBLANES) * GROUP + r % SUBLANES


def _modulation_kernel(cond_ref, w_ref, b_ref, o_ref):
    s = _silu(cond_ref[...])
    o_ref[...] = jnp.dot(s, w_ref[...], preferred_element_type=F32,
                         precision=lax.Precision.HIGHEST) + b_ref[...]


def _modulation(cond, w_ada, b_ada):
    rows, d = cond.shape
    cols = w_ada.shape[1]
    tile = cols // 4
    return pl.pallas_call(
        _modulation_kernel,
        grid=(cols // tile,),
        in_specs=[
            pl.BlockSpec((rows, d), lambda j: (0, 0)),
            pl.BlockSpec((d, tile), lambda j: (0, j)),
            pl.BlockSpec((1, tile), lambda j: (0, j)),
        ],
        out_specs=pl.BlockSpec((rows, tile), lambda j: (0, j)),
        out_shape=jax.ShapeDtypeStruct((rows, cols), F32),
        compiler_params=pltpu.CompilerParams(
            dimension_semantics=("arbitrary",),
            vmem_limit_bytes=_vmem(3 * d * tile * 4)),
        name="modulation",
    )(cond, w_ada, b_ada.reshape(1, cols))


def _context_kernel(lg_ref, ctx_ref, gpre_ref, sc_ref, sh_ref, wk_ref, wv_ref,
                    cos_ref, sin_ref, sf_ref, sb_ref):
    n = ctx_ref.shape[1]
    hd = LANES
    hc = _rms(ctx_ref[0], gpre_ref[...]) * (1.0 + sc_ref[...]) + sh_ref[...]
    hc = hc.astype(BF16)
    k = jnp.dot(hc, wk_ref[...], preferred_element_type=F32)
    v = jnp.dot(hc, wv_ref[...], preferred_element_type=F32)
    cos = cos_ref[...]
    sin = sin_ref[...]
    pos = lax.broadcasted_iota(jnp.int32, (n, 1), 0).astype(F32)
    for h in range(RET_HEADS):
        kh = k[:, h * hd:(h + 1) * hd]
        kr = (kh * cos + _rot_half(kh) * sin) * (hd ** -0.5)
        vh = v[:, h * hd:(h + 1) * hd].astype(BF16)
        wf = jnp.exp(lg_ref[0, h] * (n - 1.0 - pos))
        wb = jnp.exp(lg_ref[1, h] * pos)
        kf = (kr * wf).T.astype(BF16)
        kb = (kr * wb).T.astype(BF16)
        sf_ref[0, h] = jnp.dot(kf, vh, preferred_element_type=F32)
        sb_ref[0, h] = jnp.dot(kb, vh, preferred_element_type=F32)


def _context_states(lg, ctx, gpre, csc1, csh1, w_in, k_col, cos, sin):
    b, n, d = ctx.shape
    hd = LANES
    rw = RET_HEADS * hd
    st = jax.ShapeDtypeStruct((b, RET_HEADS, hd, hd), F32)
    vec = pl.BlockSpec((1, d), lambda i: (0, 0))
    return pl.pallas_call(
        _context_kernel,
        grid=(b,),
        in_specs=[
            pl.BlockSpec(memory_space=pltpu.SMEM),
            pl.BlockSpec((1, n, d), lambda i: (i, 0, 0)),
            vec, vec, vec,
            pl.BlockSpec((d, rw), lambda i: (0, k_col)),
            pl.BlockSpec((d, rw), lambda i: (0, k_col + 1)),
            pl.BlockSpec((n, hd), lambda i: (0, 0)),
            pl.BlockSpec((n, hd), lambda i: (0, 0)),
        ],
        out_specs=[pl.BlockSpec((1, RET_HEADS, hd, hd), lambda i: (i, 0, 0, 0))] * 2,
        out_shape=[st, st],
        compiler_params=pltpu.CompilerParams(dimension_semantics=("arbitrary",)),
        name="context",
    )(lg, ctx, gpre, csc1, csh1, w_in, w_in, cos, sin)


def _fill_tables(lg_ref, direction, decay_ref, xi_ref, zeta_ref):
    c = decay_ref.shape[1]
    i = lax.broadcasted_iota(jnp.int32, (c, c), 0)
    j = lax.broadcasted_iota(jnp.int32, (c, c), 1)
    col = lax.broadcasted_iota(jnp.int32, (c, 1), 0).astype(F32)
    for h in range(RET_HEADS):
        lg = lg_ref[direction, h]
        if direction == 0:
            rel = (i - j).astype(F32)
            mask = i >= j
            xi_ref[h] = jnp.exp(lg * (col + 1.0))
            zeta_ref[h] = jnp.exp(lg * (c - 1.0 - col))
        else:
            rel = (j - i).astype(F32)
            mask = j > i
            xi_ref[h] = jnp.exp(lg * (c - col))
            zeta_ref[h] = jnp.exp(lg * col)
        decay_ref[h] = jnp.where(mask, jnp.exp(lg * jnp.where(mask, rel, 0.0)), 0.0)


def _retention_block(q_blk, k_blk_f32, v_blk, lg, s_ref, decay_ref, xi_ref, zeta_ref, h,
                     reverse):
    c = decay_ref.shape[1]
    n_chunks = q_blk.shape[0] // c
    outs = [None] * n_chunks
    for ci in (reversed(range(n_chunks)) if reverse else range(n_chunks)):
        rows = slice(ci * c, (ci + 1) * c)
        qh, kh_f32, vh = q_blk[rows], k_blk_f32[rows], v_blk[rows]
        kh = kh_f32.astype(BF16)
        scores = lax.dot_general(qh, kh, (((1,), (1,)), ((), ())), preferred_element_type=F32)
        s_prev = s_ref[h]
        lhs = jnp.concatenate([(scores * decay_ref[h]).astype(BF16),
                               (qh.astype(F32) * xi_ref[h]).astype(BF16)], axis=1)
        rhs = jnp.concatenate([vh, s_prev.astype(BF16)], axis=0)
        outs[ci] = jnp.dot(lhs, rhs, preferred_element_type=F32)
        kz = (kh_f32 * zeta_ref[h]).T.astype(BF16)
        s_ref[h] = jnp.exp(lg * c) * s_prev + jnp.dot(kz, vh, preferred_element_type=F32)
    return outs[0] if n_chunks == 1 else jnp.concatenate(outs, axis=0)


def _mix_bwd_kernel(lg_ref, x_ref, gpre_ref, sc_ref, sh_ref, win_ref, cw_ref, cb_ref,
                    cos_ref, sin_ref, s0_ref,
                    q_ref, k_ref, v_ref, g_ref, yb_ref, conv_ref,
                    s_ref, decay_ref, xi_ref, zeta_ref):
    t = x_ref.shape[1]
    hd = LANES
    cw = conv_ref.shape[2]
    rw = RET_HEADS * hd

    @pl.when((pl.program_id(0) == 0) & (pl.program_id(1) == 0))
    def _():
        _fill_tables(lg_ref, 1, decay_ref, xi_ref, zeta_ref)

    @pl.when(pl.program_id(1) == 0)
    def _():
        s_ref[...] = s0_ref[0]

    hx = (_rms(x_ref[0], gpre_ref[...]) * (1.0 + sc_ref[0]) + sh_ref[0]).astype(BF16)
    off = 3 * cw

    def project(c0, width):
        return jnp.dot(hx, win_ref[:, c0:c0 + width], preferred_element_type=F32)

    g_ref[0] = project(off + 3 * rw, rw)
    v_bf = project(off + 2 * rw, rw).astype(BF16)
    v_ref[0] = v_bf
    cos = cos_ref[...]
    sin = sin_ref[...]
    q = project(off, rw)
    qr = [(q[:, h * hd:(h + 1) * hd] * cos + _rot_half(q[:, h * hd:(h + 1) * hd]) * sin)
          .astype(BF16) for h in range(RET_HEADS)]
    k = project(off + rw, rw)
    kr = [(k[:, h * hd:(h + 1) * hd] * cos + _rot_half(k[:, h * hd:(h + 1) * hd]) * sin)
          * (hd ** -0.5) for h in range(RET_HEADS)]
    u = project(cw, cw) * project(2 * cw, cw)
    x_b = project(0, cw)

    cq = cw // RET_HEADS
    col = lax.broadcasted_iota(jnp.int32, (t, cq), 0) & (GRID_W - 1)
    for h in range(RET_HEADS):
        sl = slice(h * hd, (h + 1) * hd)
        q_ref[0, :, sl] = qr[h]
        k_ref[0, :, sl] = kr[h].astype(BF16)
        yb_ref[0, :, sl] = _retention_block(qr[h], kr[h], v_bf[:, sl], lg_ref[1, h], s_ref,
                                            decay_ref, xi_ref, zeta_ref, h, reverse=True)
        cs = slice(h * cq, (h + 1) * cq)
        uq = u[:, cs]
        u_prev = jnp.where(col == 0, 0.0, pltpu.roll(uq, 1, axis=0))
        u_next = jnp.where(col == GRID_W - 1, 0.0, pltpu.roll(uq, t - 1, axis=0))
        y = (u_prev * cw_ref[0:1, cs] + uq * cw_ref[1:2, cs] + u_next * cw_ref[2:3, cs]
             + cb_ref[:, cs])
        conv_ref[0, :, cs] = (x_b[:, cs] * y).astype(BF16)


def _mix_bwd(lg, x, gpre, sc1, sh1, w_in, conv_w, conv_b, cos, sin, s0_b):
    b, n, d = x.shape
    t = PROJ_BLOCK
    nb = n // t
    hd = LANES
    cw = conv_w.shape[1]
    rw = RET_HEADS * hd
    rev = lambda i, j: (i, nb - 1 - j, 0)
    vec = pl.BlockSpec((1, d), lambda i, j: (0, 0))
    bvec = pl.BlockSpec((1, 1, d), lambda i, j: (i, 0, 0))
    tok = lambda w: pl.BlockSpec((1, t, w), rev)
    rope = pl.BlockSpec((t, hd), lambda i, j: (nb - 1 - j, 0))
    out = lambda w, dt: jax.ShapeDtypeStruct((b, n, w), dt)
    return pl.pallas_call(
        _mix_bwd_kernel,
        grid=(b, nb),
        in_specs=[
            pl.BlockSpec(memory_space=pltpu.SMEM),
            tok(d), vec, bvec, bvec,
            pl.BlockSpec(w_in.shape, lambda i, j: (0, 0), pipeline_mode=pl.Buffered(1)),
            pl.BlockSpec(conv_w.shape, lambda i, j: (0, 0)),
            pl.BlockSpec((1, cw), lambda i, j: (0, 0)),
            rope, rope,
            pl.BlockSpec((1, RET_HEADS, hd, hd), lambda i, j: (i, 0, 0, 0)),
        ],
        out_specs=[tok(rw), tok(rw), tok(rw), tok(rw), tok(rw), tok(cw)],
        out_shape=[out(rw, BF16), out(rw, BF16), out(rw, BF16), out(rw, F32),
                   out(rw, F32), out(cw, BF16)],
        scratch_shapes=[pltpu.VMEM((RET_HEADS, hd, hd), F32),
                        pltpu.VMEM((RET_HEADS, RET_CHUNK, RET_CHUNK), F32),
                        pltpu.VMEM((RET_HEADS, RET_CHUNK, 1), F32),
                        pltpu.VMEM((RET_HEADS, RET_CHUNK, 1), F32)],
        compiler_params=pltpu.CompilerParams(
            dimension_semantics=("arbitrary", "arbitrary"),
            vmem_limit_bytes=_vmem(56 * 2 ** 20)),
        name="mix_bwd",
    )(lg, x, gpre, sc1, sh1, w_in, conv_w, conv_b, cos, sin, s0_b)


def _mix_fwd_kernel(lg_ref, x_ref, q_ref, k_ref, v_ref, g_ref, yb_ref, conv_ref,
                    wout_ref, gn_ref, gpost_ref, gffn_ref, g1_ref, sc_ref, sh_ref,
                    wr_ref, br_ref, s0_ref,
                    xmid_ref, h2_ref, logit_ref, s_ref, mix_ref,
                    decay_ref, xi_ref, zeta_ref, *, nb):
    hd = LANES
    cw = conv_ref.shape[2]
    step = pl.program_id(0)

    @pl.when(step == 0)
    def _():
        mix_ref[...] = jnp.zeros(mix_ref.shape, BF16)
        _fill_tables(lg_ref, 0, decay_ref, xi_ref, zeta_ref)

    @pl.when(step % nb == 0)
    def _():
        s_ref[...] = s0_ref[0]

    t = x_ref.shape[1]
    ne = logit_ref.shape[2]
    mix = jnp.dot(mix_ref[...], wout_ref[...], preferred_element_type=F32)

    def finish_rows(r0, r1):
        x_mid = x_ref[0, r0:r1] + g1_ref[0] * _rms(mix[r0:r1], gpost_ref[...])
        xmid_ref[0, r0:r1] = x_mid
        h2 = _rms(x_mid, gffn_ref[...]) * (1.0 + sc_ref[0]) + sh_ref[0]
        h_hi = h2.astype(BF16)
        h_lo = (h2 - h_hi.astype(F32)).astype(BF16)
        nt = (((1,), (1,)), ((), ()))
        p = (lax.dot_general(wr_ref[...], h_hi, nt, preferred_element_type=F32)
             + lax.dot_general(wr_ref[...], h_lo, nt, preferred_element_type=F32))
        logit_ref[0, r0 // LANES] = p[0:ne] + p[ne:2 * ne] + br_ref[...]
        for c in range(ROW_TILE):
            h2_ref[0, r0 // SUBLANES:r1 // SUBLANES, c * SUBLANES:(c + 1) * SUBLANES, :] = (
                h2[:, c * LANES:(c + 1) * LANES].reshape((r1 - r0) // SUBLANES, SUBLANES, LANES))

    mix_ref[:, 0:cw] = conv_ref[0]
    rows = t // RET_HEADS
    for h in range(RET_HEADS):
        sl = slice(h * hd, (h + 1) * hd)
        y = _retention_block(q_ref[0, :, sl], k_ref[0, :, sl].astype(F32), v_ref[0, :, sl],
                             lg_ref[0, h], s_ref, decay_ref, xi_ref, zeta_ref, h, reverse=False)
        finish_rows(h * rows, (h + 1) * rows)
        y = y + yb_ref[0, :, sl]
        mu = jnp.mean(y, axis=-1, keepdims=True)
        yc = y - mu
        var = jnp.mean(yc * yc, axis=-1, keepdims=True)
        yn = yc * lax.rsqrt(var + GN_EPS) * gn_ref[:, sl]
        mix_ref[:, cw + h * hd:cw + (h + 1) * hd] = (
            _silu(g_ref[0, :, sl]) * yn).astype(BF16)


def _mix_fwd(lg, x, q, k, v, g, yb, conv, w_out, gn, gpost, gffn, g1, sc2, sh2, wr, br, s0_f):
    b, n, d = x.shape
    t = TOKEN_BLOCK
    nb = n // t
    hd = LANES
    cw = conv.shape[2]
    rw = RET_HEADS * hd
    ne = br.shape[0]
    assert t // RET_HEADS == LANES
    steps = b * nb
    ret = lambda s: jnp.minimum(s, steps - 1)
    outp = lambda s: jnp.maximum(s - 1, 0)
    vec = lambda w: pl.BlockSpec((1, w), lambda s: (0, 0))
    bvec = pl.BlockSpec((1, 1, d), lambda s: (outp(s) // nb, 0, 0))
    rtok = lambda w: pl.BlockSpec((1, t, w), lambda s: (ret(s) // nb, ret(s) % nb, 0))
    otok = lambda w: pl.BlockSpec((1, t, w), lambda s: (outp(s) // nb, outp(s) % nb, 0))
    return pl.pallas_call(
        functools.partial(_mix_fwd_kernel, nb=nb),
        grid=(steps + 1,),
        in_specs=[
            pl.BlockSpec(memory_space=pltpu.SMEM),
            otok(d), rtok(rw), rtok(rw), rtok(rw), rtok(rw), rtok(rw), rtok(cw),
            pl.BlockSpec(w_out.shape, lambda s: (0, 0)),
            vec(rw), vec(d), vec(d), bvec, bvec, bvec,
            pl.BlockSpec(wr.shape, lambda s: (0, 0)),
            pl.BlockSpec((ne, 1), lambda s: (0, 0)),
            pl.BlockSpec((1, RET_HEADS, hd, hd), lambda s: (ret(s) // nb, 0, 0, 0)),
        ],
        out_specs=[otok(d),
                   pl.BlockSpec((1, t // SUBLANES, GROUP, LANES),
                                lambda s: (outp(s) // nb, outp(s) % nb, 0, 0)),
                   pl.BlockSpec((1, t // LANES, ne, LANES),
                                lambda s: (outp(s) // nb, outp(s) % nb, 0, 0))],
        out_shape=[jax.ShapeDtypeStruct((b, n, d), F32),
                   jax.ShapeDtypeStruct((b, n // SUBLANES, GROUP, LANES), F32),
                   jax.ShapeDtypeStruct((b, n // LANES, ne, LANES), F32)],
        scratch_shapes=[pltpu.VMEM((RET_HEADS, hd, hd), F32),
                        pltpu.VMEM((t, cw + rw), BF16),
                        pltpu.VMEM((RET_HEADS, RET_CHUNK, RET_CHUNK), F32),
                        pltpu.VMEM((RET_HEADS, RET_CHUNK, 1), F32),
                        pltpu.VMEM((RET_HEADS, RET_CHUNK, 1), F32)],
        compiler_params=pltpu.CompilerParams(
            dimension_semantics=("arbitrary",),
            vmem_limit_bytes=_vmem(48 * 2 ** 20)),
        name="mix_fwd",
    )(lg, x, q, k, v, g, yb, conv, w_out, gn, gpost, gffn, g1, sc2, sh2, wr, br, s0_f)


def _route_kernel(logit_ref, gate_ref, idx_ref, *, cap):
    nsamp, nv, ne, _ = logit_ref.shape
    grp = gate_ref.shape[2]
    n_grp = nsamp * (ne // grp)
    keys = []
    for b in range(nsamp):
        x = logit_ref[b]
        m = jnp.max(x, axis=1, keepdims=True)
        s = jnp.sum(jnp.exp(x - m), axis=1, keepdims=True)
        keys += [jnp.exp(logit_ref[b, :, g * grp:(g + 1) * grp, :] - m) / s
                 for g in range(ne // grp)]

    top = cap // LANES

    def coords(nvreg):
        shape = (nvreg, grp, LANES)
        vi = lax.broadcasted_iota(jnp.int32, shape, 0)
        li = lax.broadcasted_iota(jnp.int32, shape, 2)
        return vi, li, (vi // top) * cap + li * top + vi % top

    def better(pk, pi, k, i):
        return (pk > k) | ((pk == k) & (pi < i))

    def stage(k, i, li, pos, dist, span):
        if dist < top:
            halves = lambda a: a.reshape(a.shape[0] // (2 * dist), 2, dist, grp, LANES)
            k5, i5 = halves(k), halves(i)
            k_lo, k_hi, i_lo, i_hi = k5[:, 0], k5[:, 1], i5[:, 0], i5[:, 1]
            hi_wins = better(k_hi, i_hi, k_lo, i_lo)
            k_b, k_w = jnp.where(hi_wins, k_hi, k_lo), jnp.where(hi_wins, k_lo, k_hi)
            i_b, i_w = jnp.where(hi_wins, i_hi, i_lo), jnp.where(hi_wins, i_lo, i_hi)
            if span is not None:
                desc = (halves(pos)[:, 0] & span) == 0
                k_b, k_w = jnp.where(desc, k_b, k_w), jnp.where(desc, k_w, k_b)
                i_b, i_w = jnp.where(desc, i_b, i_w), jnp.where(desc, i_w, i_b)
            join = lambda lo, hi: jnp.concatenate([lo[:, None], hi[:, None]], axis=1).reshape(k.shape)
            return join(k_b, k_w), join(i_b, i_w)
        sh = dist // top
        low = (li & sh) == 0
        pk = jnp.where(low, pltpu.roll(k, LANES - sh, axis=2), pltpu.roll(k, sh, axis=2))
        pi = jnp.where(low, pltpu.roll(i, LANES - sh, axis=2), pltpu.roll(i, sh, axis=2))
        want_best = low if span is None else low == ((pos & span) == 0)
        take = better(pk, pi, k, i) == want_best
        return jnp.where(take, pk, k), jnp.where(take, pi, i)

    def stage_all(ks, ids, li, pos, dist, span):
        out = [stage(k, i, li, pos, dist, span) for k, i in zip(ks, ids)]
        return [o[0] for o in out], [o[1] for o in out]

    vi, li, pos = coords(nv)
    idxs = [vi * LANES + li] * n_grp
    span = 2
    while span <= cap:
        dist = span // 2
        while dist >= 1:
            keys, idxs = stage_all(keys, idxs, li, pos, dist, span)
            dist //= 2
        span *= 2
    nblk = nv // top
    while nblk > 1:
        for g in range(n_grp):
            k5 = keys[g].reshape(nblk // 2, 2, top, grp, LANES)
            i5 = idxs[g].reshape(nblk // 2, 2, top, grp, LANES)
            k0, k1, i0, i1 = k5[:, 0], k5[:, 1], i5[:, 0], i5[:, 1]
            take = better(k1, i1, k0, i0)
            keys[g] = jnp.where(take, k1, k0).reshape(nblk // 2 * top, grp, LANES)
            idxs[g] = jnp.where(take, i1, i0).reshape(nblk // 2 * top, grp, LANES)
        nblk //= 2
        vi, li, pos = coords(nblk * top)
        dist = cap // 2
        while dist >= 1:
            keys, idxs = stage_all(keys, idxs, li, pos, dist, None if nblk == 1 else cap)
            dist //= 2
    for g in range(n_grp):
        gate_ref[g] = keys[g][0:top]
        idx_ref[g] = _row_base(idxs[g][0:top])


def _route(logits_t, cap):
    b, nv, ne, _ = logits_t.shape
    grp = SUBLANES
    top = cap // LANES
    n_grp = b * (ne // grp)
    out_block = pl.BlockSpec((n_grp, top, grp, LANES), lambda i: (0, 0, 0, 0))
    return pl.pallas_call(
        functools.partial(_route_kernel, cap=cap),
        grid=(1,),
        in_specs=[pl.BlockSpec((b, nv, ne, LANES), lambda i: (0, 0, 0, 0))],
        out_specs=[out_block, out_block],
        out_shape=[jax.ShapeDtypeStruct((n_grp, top, grp, LANES), F32),
                   jax.ShapeDtypeStruct((n_grp, top, grp, LANES), jnp.int32)],
        compiler_params=pltpu.CompilerParams(dimension_semantics=("arbitrary",)),
        name="route",
    )(logits_t)


BF16_ROWS = 2 * SUBLANES
GATHER_STAGES = 4


def _dispatch_kernel(idx_ref, h_ref, xs_ref, *stage_refs):
    cap = idx_ref.shape[2]
    per_iter = BF16_ROWS * len(stage_refs)

    def body(r, carry):
        for s, stage_ref in enumerate(stage_refs):
            for u in range(BF16_ROWS):
                src = idx_ref[0, 0, r * per_iter + s * BF16_ROWS + u]
                dst = (u // SUBLANES) * GROUP + u % SUBLANES
                stage_ref[pl.ds(dst, ROW_TILE, stride=SUBLANES), :] = (
                    h_ref[0, pl.ds(src, ROW_TILE, stride=SUBLANES), :])
        for s, stage_ref in enumerate(stage_refs):
            row0 = pl.multiple_of(r * per_iter + s * BF16_ROWS, BF16_ROWS)
            for c in range(ROW_TILE):
                lo = stage_ref[c * SUBLANES:(c + 1) * SUBLANES, :]
                hi = stage_ref[GROUP + c * SUBLANES:GROUP + (c + 1) * SUBLANES, :]
                xs_ref[0, 0, pl.ds(row0, BF16_ROWS), c * LANES:(c + 1) * LANES] = (
                    jnp.concatenate([lo, hi], axis=0).astype(BF16))
        return carry

    lax.fori_loop(0, cap // per_iter, body, 0)


def _dispatch(idx, h_rows, n_exp, cap):
    b = h_rows.shape[0]
    rows = h_rows.shape[1]
    d = ROW_TILE * LANES
    return pl.pallas_call(
        _dispatch_kernel,
        grid=(b, n_exp),
        in_specs=[
            pl.BlockSpec((1, 1, cap), lambda i, e: (i * n_exp + e, 0, 0),
                         memory_space=pltpu.SMEM),
            pl.BlockSpec((1, rows, LANES), lambda i, e: (i, 0, 0),
                         pipeline_mode=pl.Buffered(1)),
        ],
        out_specs=pl.BlockSpec((1, 1, cap, d), lambda i, e: (i, e, 0, 0)),
        out_shape=jax.ShapeDtypeStruct((b, n_exp, cap, d), BF16),
        scratch_shapes=[pltpu.VMEM((2 * GROUP, LANES), F32)] * GATHER_STAGES,
        compiler_params=pltpu.CompilerParams(
            dimension_semantics=("arbitrary", "arbitrary"),
            vmem_limit_bytes=_vmem(rows * LANES * 4 + 4 * cap * d * 2)),
        name="dispatch",
    )(idx, h_rows)


def _ffn_kernel(xs_ref, wg_ref, wu_ref, wd_ref, y_ref):
    nb = xs_ref.shape[0]
    tf = wg_ref.shape[2]

    @pl.when(pl.program_id(1) == 0)
    def _():
        y_ref[...] = jnp.zeros(y_ref.shape, F32)

    wgu = jnp.concatenate([wg_ref[0].astype(BF16), wu_ref[0].astype(BF16)], axis=1)
    wd = wd_ref[0].astype(BF16)
    for b in range(nb):
        au = jnp.dot(xs_ref[b, 0], wgu, preferred_element_type=F32)
        hmid = (_silu(au[:, 0:tf]) * au[:, tf:2 * tf]).astype(BF16)
        contrib = jnp.dot(hmid, wd, preferred_element_type=F32)
        rows = contrib.shape[0]
        for c in range(ROW_TILE):
            y_ref[b, 0, :, c * SUBLANES:(c + 1) * SUBLANES, :] += (
                contrib[:, c * LANES:(c + 1) * LANES].reshape(rows // SUBLANES, SUBLANES, LANES))


def _ffn(xs, w_gate, w_up, w_down):
    b, n_exp, cap, d = xs.shape
    dff = w_gate.shape[2]
    tf = FFN_TILE
    groups = cap // SUBLANES
    return pl.pallas_call(
        _ffn_kernel,
        grid=(n_exp, dff // tf),
        in_specs=[
            pl.BlockSpec((b, 1, cap, d), lambda e, f: (0, e, 0, 0)),
            pl.BlockSpec((1, d, tf), lambda e, f: (e, 0, f)),
            pl.BlockSpec((1, d, tf), lambda e, f: (e, 0, f)),
            pl.BlockSpec((1, tf, d), lambda e, f: (e, f, 0)),
        ],
        out_specs=pl.BlockSpec((b, 1, groups, GROUP, LANES), lambda e, f: (0, e, 0, 0, 0)),
        out_shape=jax.ShapeDtypeStruct((b, n_exp, groups, GROUP, LANES), F32),
        compiler_params=pltpu.CompilerParams(
            dimension_semantics=("arbitrary", "arbitrary"),
            vmem_limit_bytes=_vmem(56 * 2 ** 20)),
        name="ffn",
    )(xs, w_gate, w_up, w_down)


def _combine_kernel(idx_ref, gate_ref, y_ref, xmid_ref, gpost_ref, g2_ref, o_ref, acc_ref,
                    *, n_exp):
    cap = idx_ref.shape[2]
    t = xmid_ref.shape[1]
    step = pl.program_id(1)

    @pl.when(step == 0)
    def _():
        zrows = SUBLANES * GROUP

        def zero(r, carry):
            acc_ref[pl.ds(pl.multiple_of(r * zrows, zrows), zrows), :] = jnp.zeros(
                (zrows, LANES), F32)
            return carry

        lax.fori_loop(0, acc_ref.shape[0] // zrows, zero, 0)

    @pl.when(step < n_exp)
    def _():
        def body(r, carry):
            offs, vals = [], []
            for u in range(SCATTER_UNROLL):
                dst = idx_ref[0, 0, r * SCATTER_UNROLL + u]
                gate = gate_ref[0, 0, r * SCATTER_UNROLL + u]
                src = (r * (SCATTER_UNROLL // SUBLANES) + u // SUBLANES) * GROUP + u % SUBLANES
                offs.append(dst)
                vals.append(acc_ref[pl.ds(dst, ROW_TILE, stride=SUBLANES), :]
                            + gate * y_ref[0, 0, pl.ds(src, ROW_TILE, stride=SUBLANES), :])
            for dst, val in zip(offs, vals):
                acc_ref[pl.ds(dst, ROW_TILE, stride=SUBLANES), :] = val
            return carry

        lax.fori_loop(0, cap // SCATTER_UNROLL, body, 0)

    @pl.when(step >= n_exp)
    def _():
        rows = t * ROW_TILE
        start = pl.multiple_of((step - n_exp) * rows, rows)
        blk = acc_ref[pl.ds(start, rows), :].reshape(t // SUBLANES, GROUP, LANES)
        ffn = jnp.concatenate(
            [blk[:, c * SUBLANES:(c + 1) * SUBLANES, :].reshape(t, LANES)
             for c in range(ROW_TILE)], axis=-1)
        o_ref[0] = xmid_ref[0] + g2_ref[0] * _rms(ffn, gpost_ref[...])


def _combine(idx, gate, y, x_mid, gpost, g2):
    b, n_exp, rows, _ = y.shape
    n, d = x_mid.shape[1:]
    cap = rows // ROW_TILE
    t = TOKEN_BLOCK
    last = n_exp - 1
    tok_block = pl.BlockSpec((1, t, d), lambda i, s: (i, jnp.maximum(s - n_exp, 0), 0))
    per_expert = pl.BlockSpec((1, 1, cap), lambda i, s: (i * n_exp + jnp.minimum(s, last), 0, 0),
                              memory_space=pltpu.SMEM)
    return pl.pallas_call(
        functools.partial(_combine_kernel, n_exp=n_exp),
        grid=(b, n_exp + n // t),
        in_specs=[
            per_expert, per_expert,
            pl.BlockSpec((1, 1, rows, LANES), lambda i, s: (i, jnp.minimum(s, last), 0, 0)),
            tok_block,
            pl.BlockSpec((1, d), lambda i, s: (0, 0)),
            pl.BlockSpec((1, 1, d), lambda i, s: (i, 0, 0)),
        ],
        out_specs=tok_block,
        out_shape=jax.ShapeDtypeStruct((b, n, d), F32),
        scratch_shapes=[pltpu.VMEM((n * ROW_TILE, LANES), F32)],
        compiler_params=pltpu.CompilerParams(
            dimension_semantics=("arbitrary", "arbitrary"),
            vmem_limit_bytes=_vmem(n * d * 4 + 2 * rows * LANES * 4 + 8 * t * d * 4)),
        name="combine",
    )(idx, gate, y, x_mid, gpost, g2)


def _rope_tables(n_pos):
    half = LANES // 2
    inv = 1.0 / (ROPE_BASE ** (jnp.arange(half, dtype=F32) / half))
    ang_a = (jnp.arange(n_pos // LANES, dtype=F32) * LANES)[:, None] * inv[None, :]
    ang_b = jnp.arange(LANES, dtype=F32)[:, None] * inv[None, :]
    ca, sa = jnp.cos(ang_a)[:, None, :], jnp.sin(ang_a)[:, None, :]
    cb, sb = jnp.cos(ang_b)[None, :, :], jnp.sin(ang_b)[None, :, :]
    cos = (ca * cb - sa * sb).reshape(n_pos, half)
    sin = (sa * cb + ca * sb).reshape(n_pos, half)
    return jnp.concatenate([cos, cos], axis=-1), jnp.concatenate([-sin, sin], axis=-1)


def kernel(x, c, ctx, c_ctx, w_ada, b_ada, pre_mix_g, post_mix_g, pre_ffn_g, post_ffn_g,
           w_in, conv_w, conv_b, ret_decay_logit, ret_norm_g, w_out,
           w_router, b_router, w_gate, w_up, w_down):
    bsz, n, d = x.shape
    ctx_len = ctx.shape[1]
    depth = w_ada.shape[0]
    n_exp = w_router.shape[2]
    cw = conv_w.shape[2]
    rw = RET_HEADS * LANES
    cap = max(1, EC_CAPACITY_FACTOR * n // n_exp)
    assert depth == 1, "only the single-layer block is implemented"
    assert d == ROW_TILE * LANES and n % TOKEN_BLOCK == 0 and TOKEN_BLOCK % LANES == 0
    assert n % GRID_W == 0 and TOKEN_BLOCK % GRID_W == 0 and ctx_len % SUBLANES == 0
    assert n_exp % SUBLANES == 0 and cap % LANES == 0 and (n // LANES) & (n // LANES - 1) == 0
    assert w_in.shape[2] == 3 * cw + 4 * rw

    assert (ctx_len + n) % LANES == 0
    cos_all, sin_all = _rope_tables(ctx_len + n)
    cos_c, sin_c = cos_all[:ctx_len], sin_all[:ctx_len]
    cos_x, sin_x = cos_all[ctx_len:], sin_all[ctx_len:]
    i = 0

    pad = -(bsz + 1) % SUBLANES
    cond = jnp.concatenate([c, c_ctx[None, :], jnp.zeros((pad, d), F32)], axis=0)
    mod = _modulation(cond, pltpu.with_memory_space_constraint(w_ada[i], pltpu.HBM), b_ada[i])
    sh1, sc1, g1, sh2, sc2, g2 = [mod[:bsz, j * d:(j + 1) * d].reshape(bsz, 1, d) for j in range(6)]
    csh1 = mod[bsz:bsz + 1, 0:d]
    csc1 = mod[bsz:bsz + 1, d:2 * d]

    lg = jax.nn.log_sigmoid(ret_decay_logit[i].astype(F32))
    w_in_b = w_in[i].astype(BF16)
    w_out_b = w_out[i].astype(BF16)
    k_off = 3 * cw + rw
    row = lambda a: a.reshape(1, -1)

    assert k_off % rw == 0
    s_f, s_b = _context_states(lg, ctx, row(pre_mix_g[i]), csc1, csh1, w_in_b, k_off // rw,
                               cos_c, sin_c)

    q, k, v, g, yb, conv = _mix_bwd(lg, x, row(pre_mix_g[i]), sc1, sh1, w_in_b,
                                    conv_w[i], row(conv_b[i]), cos_x, sin_x, s_b)

    wr = w_router[i].astype(F32).T
    wr_hi = wr.astype(BF16)
    wr_lo = (wr - wr_hi.astype(F32)).astype(BF16)
    wr_cat = jnp.concatenate([wr_hi, wr_lo], axis=0)
    br = b_router[i].astype(F32).reshape(n_exp, 1)
    x_mid, h_tiles, logits_t = _mix_fwd(lg, x, q, k, v, g, yb, conv, w_out_b, row(ret_norm_g[i]),
                                        row(post_mix_g[i]), row(pre_ffn_g[i]), g1, sc2, sh2,
                                        wr_cat, br, s_f)
    gate4, idx4 = _route(logits_t, cap)
    gate = gate4.transpose(0, 2, 1, 3).reshape(bsz * n_exp, 1, cap)
    idx = idx4.transpose(0, 2, 1, 3).reshape(bsz * n_exp, 1, cap)

    xs = _dispatch(idx, h_tiles.reshape(bsz, n * ROW_TILE, LANES), n_exp, cap)
    y = _ffn(xs, w_gate[i], w_up[i], w_down[i])
    return _combine(idx, gate, y.reshape(bsz, n_exp, cap * ROW_TILE, LANES), x_mid,
                    row(post_ffn_g[i]), g2)
```
